```python
import math, functools
import jax, jax.numpy as jnp
from jax import lax
import numpy as np

D_MODEL = 1024
BATCH = 2
SEQ = 8192
DEPTH = 1

GRID_W = 64
CTX_LEN = 256
HEAD_DIM = 64
A_HEADS = 8
A_KV_HEADS = 2
A_GROUP = A_HEADS // A_KV_HEADS
A_WIDTH = A_HEADS * HEAD_DIM
B_HEADS = 4
B_V_DIM = 2 * HEAD_DIM
B_WIDTH = B_HEADS * B_V_DIM
N_BRANCH = 2
A_Q_COLS = A_HEADS * HEAD_DIM
A_KV_COLS = A_KV_HEADS * HEAD_DIM
B_QK_COLS = B_HEADS * 2 * HEAD_DIM
IN_SPLIT = [A_Q_COLS, A_KV_COLS, A_KV_COLS, B_QK_COLS, B_QK_COLS, B_WIDTH, D_MODEL, D_MODEL]
IN_COLS = sum(IN_SPLIT)
IN_OFFSETS = [int(v) for v in np.cumsum(IN_SPLIT)[:-1]]
ROPE_THETA = 10000.0
AXIS_PAIRS = HEAD_DIM // 4
Q_BLOCK = 128
N_EXPERTS = 32
TOP_K = 4
D_EXPERT = D_MODEL
SWIGLU_LIMIT = 7.0
SWIGLU_ALPHA = 1.702
MOE_BLOCK = 128
N_MOD = 6
EPS = 1e-6
SUBLN_EPS = 1e-5

kernel_name = "hybrid_gqa_diffattn_moe_dit_block"


def _rmsnorm(x, gain, eps=EPS):
    xf = x.astype(jnp.float32)
    y = xf * lax.rsqrt(jnp.mean(xf * xf, axis=-1, keepdims=True) + eps)
    return (y * gain.astype(jnp.float32)).astype(x.dtype)


def _modulate(h, shift, scale):
    return h * (1 + scale) + shift


def _axial_rope(rows):
    row = jnp.repeat(jnp.arange(rows, dtype=jnp.float32), GRID_W)
    col = jnp.tile(jnp.arange(GRID_W, dtype=jnp.float32), rows)
    inv = 1.0 / (ROPE_THETA ** (jnp.arange(AXIS_PAIRS, dtype=jnp.float32) / AXIS_PAIRS))
    ang = jnp.concatenate([row[:, None] * inv, col[:, None] * inv], axis=-1)
    return jnp.cos(ang), jnp.sin(ang)


def _apply_rope(x, cos, sin):
    extra = x.ndim - 3
    cos = cos.reshape(cos.shape[0], *([1] * extra), cos.shape[1])
    sin = sin.reshape(sin.shape[0], *([1] * extra), sin.shape[1])
    xp = x.reshape(*x.shape[:-1], HEAD_DIM // 2, 2)
    x0, x1 = xp[..., 0], xp[..., 1]
    out = jnp.stack([x0 * cos - x1 * sin, x0 * sin + x1 * cos], axis=-1)
    return out.reshape(x.shape).astype(x.dtype)


def _project(h, w_in, q_norm_a, k_norm_a, q_norm_b, k_norm_b):
    b, n, _ = h.shape
    qa, ka, va, qb, kb, vb, ga, gb = jnp.split(h @ w_in, IN_OFFSETS, axis=-1)
    qa = _rmsnorm(qa.reshape(b, n, A_HEADS, HEAD_DIM), q_norm_a)
    ka = _rmsnorm(ka.reshape(b, n, A_KV_HEADS, HEAD_DIM), k_norm_a)
    va = va.reshape(b, n, A_KV_HEADS, HEAD_DIM)
    qb = _rmsnorm(qb.reshape(b, n, B_HEADS, 2, HEAD_DIM), q_norm_b)
    kb = _rmsnorm(kb.reshape(b, n, B_HEADS, 2, HEAD_DIM), k_norm_b)
    vb = vb.reshape(b, n, B_HEADS, B_V_DIM)
    return qa, ka, va, qb, kb, vb, ga, gb


def _gqa_attend(q, k, v):
    b, n = q.shape[:2]
    qg = q.reshape(b, n, A_KV_HEADS, A_GROUP, HEAD_DIM)
    s = jnp.einsum('bqhgd,bkhd->bhgqk', qg, k).astype(jnp.float32) * (HEAD_DIM ** -0.5)
    p = jax.nn.softmax(s, axis=-1).astype(v.dtype)
    o = jnp.einsum('bhgqk,bkhd->bqhgd', p, v)
    return o.reshape(b, n, A_WIDTH)


def _diff_attend(q, k, v, lam):
    s = jnp.einsum('bqhmd,bkhmd->bhmqk', q, k).astype(jnp.float32) * (HEAD_DIM ** -0.5)
    p = jax.nn.softmax(s, axis=-1)
    a = (p[:, :, 0] - lam * p[:, :, 1]).astype(v.dtype)
    return jnp.einsum('bhqk,bkhe->bqhe', a, v)


def _sweep_query_blocks(attend, q, k, v):
    b, s = q.shape[:2]
    nb = s // Q_BLOCK
    qb = jnp.moveaxis(q.reshape(b, nb, Q_BLOCK, *q.shape[2:]), 1, 0)
    o = lax.map(lambda qq: attend(qq, k, v), qb)
    o = jnp.moveaxis(o, 0, 1)
    return o.reshape(b, s, *o.shape[3:])


def _diff_out(o, subln, lam_init):
    o = _rmsnorm(o, subln, SUBLN_EPS) * (1.0 - lam_init)
    return o.reshape(o.shape[0], o.shape[1], B_WIDTH)


def _merge(ya, yb, ga, gb, w_oa, w_ob, w_out):
    merged = jax.nn.sigmoid(ga) * (ya @ w_oa) + jax.nn.sigmoid(gb) * (yb @ w_ob)
    return merged @ w_out


def _moe(h, router_w, router_b, w_gate, b_gate, w_up, b_up, w_down, b_down):
    t = h.shape[0]
    logits = (h @ router_w + router_b).astype(jnp.float32)
    top_logit, top_e = lax.top_k(logits, TOP_K)
    top_w = jax.nn.softmax(top_logit, axis=-1)
    n_assign = t * TOP_K
    flat_e = top_e.reshape(-1)
    order = jnp.argsort(flat_e)
    sorted_e = flat_e[order]
    sorted_tok = order // TOP_K
    sorted_w = top_w.reshape(-1)[order]
    counts = jnp.bincount(flat_e, length=N_EXPERTS)
    padded = (counts + MOE_BLOCK - 1) // MOE_BLOCK * MOE_BLOCK
    pad_end = jnp.cumsum(padded)
    pad_start = pad_end - padded
    start = jnp.cumsum(counts) - counts
    dest = pad_start[sorted_e] + jnp.arange(n_assign) - start[sorted_e]
    n_blocks = -(-n_assign // MOE_BLOCK) + N_EXPERTS
    buf_tok = jnp.zeros((n_blocks * MOE_BLOCK,), jnp.int32).at[dest].set(sorted_tok.astype(jnp.int32))
    block_e = jnp.minimum(jnp.searchsorted(pad_end, jnp.arange(n_blocks) * MOE_BLOCK, side='right'),
                          N_EXPERTS - 1)
    xb = h[buf_tok].reshape(n_blocks, MOE_BLOCK, h.shape[-1])

    def expert_block(args):
        xe, e = args
        gate = jnp.minimum(xe @ w_gate[e] + b_gate[e], SWIGLU_LIMIT)
        up = jnp.clip(xe @ w_up[e] + b_up[e], -SWIGLU_LIMIT, SWIGLU_LIMIT)
        act = (up + 1) * (gate * jax.nn.sigmoid(SWIGLU_ALPHA * gate))
        return act @ w_down[e] + b_down[e]

    yb = lax.map(expert_block, (xb, block_e)).reshape(-1, h.shape[-1])
    y = yb[dest] * sorted_w[:, None].astype(h.dtype)
    return jax.ops.segment_sum(y, sorted_tok, num_segments=t)


def setup_inputs(seed: int = 0) -> dict:
    key = jax.random.key(seed)
    ks = jax.random.split(key, 32)
    f32 = jnp.float32
    L = DEPTH

    def nrm(k, shape, scale):
        return jax.random.normal(k, shape, f32) * scale

    def gain(k, shape):
        return 1.0 + 0.05 * jax.random.normal(k, shape, f32)

    return {
        "x": nrm(ks[0], (BATCH, SEQ, D_MODEL), 1.0),
        "c": nrm(ks[1], (BATCH, D_MODEL), 1.0),
        "ctx": nrm(ks[2], (BATCH, CTX_LEN, D_MODEL), 1.0),
        "c_ctx": nrm(ks[3], (D_MODEL,), 1.0),
        "w_ada": nrm(ks[4], (L, D_MODEL, N_MOD * D_MODEL), 0.5 * D_MODEL ** -0.5),
        "b_ada": nrm(ks[5], (L, N_MOD * D_MODEL), 0.02),
        "norm_attn": gain(ks[6], (L, D_MODEL)),
        "w_in": nrm(ks[7], (L, D_MODEL, IN_COLS), D_MODEL ** -0.5),
        "q_norm_a": gain(ks[8], (L, HEAD_DIM)),
        "k_norm_a": gain(ks[9], (L, HEAD_DIM)),
        "q_norm_b": gain(ks[10], (L, HEAD_DIM)),
        "k_norm_b": gain(ks[11], (L, HEAD_DIM)),
        "lambda_q1": nrm(ks[12], (L, HEAD_DIM), 0.1),
        "lambda_k1": nrm(ks[13], (L, HEAD_DIM), 0.1),
        "lambda_q2": nrm(ks[14], (L, HEAD_DIM), 0.1),
        "lambda_k2": nrm(ks[15], (L, HEAD_DIM), 0.1),
        "subln_b": gain(ks[16], (L, B_V_DIM)),
        "w_oa": nrm(ks[17], (L, A_WIDTH, D_MODEL), A_WIDTH ** -0.5),
        "w_ob": nrm(ks[18], (L, B_WIDTH, D_MODEL), B_WIDTH ** -0.5),
        "w_out": nrm(ks[19], (L, D_MODEL, D_MODEL), D_MODEL ** -0.5),
        "norm_mlp": gain(ks[20], (L, D_MODEL)),
        "router_w": nrm(ks[21], (L, D_MODEL, N_EXPERTS), D_MODEL ** -0.5),
        "router_b": nrm(ks[22], (L, N_EXPERTS), 0.01),
        "w_gate": nrm(ks[23], (L, N_EXPERTS, D_MODEL, D_EXPERT), D_MODEL ** -0.5),
        "b_gate": nrm(ks[24], (L, N_EXPERTS, D_EXPERT), 0.02),
        "w_up": nrm(ks[25], (L, N_EXPERTS, D_MODEL, D_EXPERT), D_MODEL ** -0.5),
        "b_up": nrm(ks[26], (L, N_EXPERTS, D_EXPERT), 0.02),
        "w_down": nrm(ks[27], (L, N_EXPERTS, D_EXPERT, D_MODEL), D_EXPERT ** -0.5),
        "b_down": nrm(ks[28], (L, N_EXPERTS, D_MODEL), 0.02),
    }


def reference(x, c, ctx, c_ctx, w_ada, b_ada, norm_attn, w_in, q_norm_a, k_norm_a, q_norm_b,
              k_norm_b, lambda_q1, lambda_k1, lambda_q2, lambda_k2, subln_b, w_oa, w_ob, w_out,
              norm_mlp, router_w, router_b, w_gate, b_gate, w_up, b_up, w_down, b_down):
    b, s, d = x.shape
    rows = s // GRID_W
    cos, sin = _axial_rope(rows)
    cx = ctx
    for l in range(DEPTH):
        last = l == DEPTH - 1
        lam_init = 0.8 - 0.6 * math.exp(-0.3 * l)
        lam = (jnp.exp(jnp.sum(lambda_q1[l].astype(jnp.float32) * lambda_k1[l].astype(jnp.float32)))
               - jnp.exp(jnp.sum(lambda_q2[l].astype(jnp.float32) * lambda_k2[l].astype(jnp.float32)))
               + lam_init)
        mod_x = (jax.nn.silu(c) @ w_ada[l] + b_ada[l])[:, None, :]
        mod_c = (jax.nn.silu(c_ctx) @ w_ada[l] + b_ada[l])[None, None, :]
        sh_a, sc_a, g_a, sh_m, sc_m, g_m = jnp.split(mod_x, N_MOD, axis=-1)
        csh_a, csc_a, cg_a, csh_m, csc_m, cg_m = jnp.split(mod_c, N_MOD, axis=-1)
        proj = functools.partial(_project, w_in=w_in[l], q_norm_a=q_norm_a[l], k_norm_a=k_norm_a[l],
                                 q_norm_b=q_norm_b[l], k_norm_b=k_norm_b[l])

        hx = _modulate(_rmsnorm(x, norm_attn[l]), sh_a, sc_a)
        hc = _modulate(_rmsnorm(cx, norm_attn[l]), csh_a, csc_a)
        qa, ka, va, qb, kb, vb, ga, gb = proj(hx)
        cqa, cka, cva, cqb, ckb, cvb, cga, cgb = proj(hc)
        qa, ka = _apply_rope(qa, cos, sin), _apply_rope(ka, cos, sin)
        qb, kb = _apply_rope(qb, cos, sin), _apply_rope(kb, cos, sin)
        ka_all = jnp.concatenate([ka, cka], axis=1)
        va_all = jnp.concatenate([va, cva], axis=1)
        kb_all = jnp.concatenate([kb, ckb], axis=1)
        vb_all = jnp.concatenate([vb, cvb], axis=1)
        ya = _sweep_query_blocks(_gqa_attend, qa, ka_all, va_all)
        yb = _sweep_query_blocks(lambda qq, kk, vv: _diff_attend(qq, kk, vv, lam), qb, kb_all, vb_all)
        yb = _diff_out(yb, subln_b[l], lam_init)
        x = x + g_a * _merge(ya, yb, ga, gb, w_oa[l], w_ob[l], w_out[l])
        if not last:
            cya = _gqa_attend(cqa, cka, cva)
            cyb = _diff_out(_diff_attend(cqb, ckb, cvb, lam), subln_b[l], lam_init)
            cx = cx + cg_a * _merge(cya, cyb, cga, cgb, w_oa[l], w_ob[l], w_out[l])

        moe = functools.partial(_moe, router_w=router_w[l], router_b=router_b[l], w_gate=w_gate[l],
                                b_gate=b_gate[l], w_up=w_up[l], b_up=b_up[l], w_down=w_down[l],
                                b_down=b_down[l])
        hx = _modulate(_rmsnorm(x, norm_mlp[l]), sh_m, sc_m)
        x = x + g_m * moe(hx.reshape(b * s, d)).reshape(b, s, d)
        if not last:
            hc = _modulate(_rmsnorm(cx, norm_mlp[l]), csh_m, csc_m)
            cx = cx + cg_m * moe(hc.reshape(-1, d)).reshape(cx.shape)
    return x
```

```python
import functools
import math

import jax
import jax.numpy as jnp
from jax import lax
from jax.experimental import pallas as pl
from jax.experimental.pallas import tpu as pltpu

F32 = jnp.float32
BF16 = jnp.bfloat16
I32 = jnp.int32

GRID_W = 64
HEAD_DIM = 64
A_HEADS = 8
A_KV_HEADS = 2
A_GROUP = A_HEADS // A_KV_HEADS
B_HEADS = 4
B_V_DIM = 2 * HEAD_DIM
N_EXPERTS = 32
TOP_K = 4
N_MOD = 6
ROPE_THETA = 10000.0
AXIS_PAIRS = HEAD_DIM // 4
SWIGLU_LIMIT = 7.0
SWIGLU_ALPHA = 1.702
EPS = 1e-6
SUBLN_EPS = 1e-5
LAM_INIT = 0.8 - 0.6 * math.exp(-0.3 * 0)

A_Q_COLS = A_HEADS * HEAD_DIM
A_KV_COLS = A_KV_HEADS * HEAD_DIM
B_QK_COLS = B_HEADS * 2 * HEAD_DIM
B_WIDTH = B_HEADS * B_V_DIM
QKV_COLS = A_Q_COLS + 2 * A_KV_COLS + 2 * B_QK_COLS + B_WIDTH

LANES = 128
MXU_DIM = 256
TOK_TILE = 256
TQ = 256
ADA_COLS = 1536
FFN_ROWS = 256
CMB_TOK = 128
VMEM_LIMIT = 56 * 1024 * 1024


def _dot(a, b):
    return jnp.dot(a, b, preferred_element_type=F32)


def _split_bf16(a):
    hi = a.astype(BF16)
    lo = (a - hi.astype(F32)).astype(BF16)
    return hi, lo


def _dot3(a, b):
    a_hi, a_lo = _split_bf16(a)
    b_hi, b_lo = _split_bf16(b)
    return _dot(a_hi, b_hi) + (_dot(a_hi, b_lo) + _dot(a_lo, b_hi))


def _rms_rows(x, gain, eps):
    ms = jnp.mean(x * x, axis=-1, keepdims=True)
    return x * lax.rsqrt(ms + eps) * gain


def _adaln_kernel(c_ref, w_ref, b_ref, o_ref):
    c = c_ref[...]
    a = c * jax.nn.sigmoid(c)
    o_ref[...] = _dot3(a, w_ref[...]) + b_ref[...]


def _adaln(cvec, w_ada, b_ada):
    rows, d = cvec.shape
    n = w_ada.shape[1]
    return pl.pallas_call(
        _adaln_kernel,
        grid=(n // ADA_COLS,),
        in_specs=[
            pl.BlockSpec((rows, d), lambda j: (0, 0)),
            pl.BlockSpec((d, ADA_COLS), lambda j: (0, j)),
            pl.BlockSpec((1, ADA_COLS), lambda j: (0, j)),
        ],
        out_specs=pl.BlockSpec((rows, ADA_COLS), lambda j: (0, j)),
        out_shape=jax.ShapeDtypeStruct((rows, n), F32),
        compiler_params=pltpu.CompilerParams(vmem_limit_bytes=VMEM_LIMIT),
        name="adaln",
    )(cvec, w_ada, b_ada)


def _group_rms(y, gmat, gain):
    w_total = y.shape[1]
    outs = []
    for c0 in range(0, w_total, MXU_DIM):
        w = min(MXU_DIM, w_total - c0)
        yc = y[:, c0:c0 + w]
        ms = _dot((yc * yc).astype(BF16), gmat[:w, :w])
        reps = w // LANES
        g = gain if reps == 1 else jnp.concatenate([gain] * reps, axis=1)
        outs.append(yc * lax.rsqrt(ms + EPS) * g)
    return outs[0] if len(outs) == 1 else jnp.concatenate(outs, axis=1)


def _rope(y, cos, sin):
    rows = y.shape[0]
    lane = lax.broadcasted_iota(I32, (rows, LANES), 1)
    even = (lane % 2) == 0
    outs = []
    for c in range(y.shape[1] // LANES):
        ch = y[:, c * LANES:(c + 1) * LANES]
        partner = jnp.where(even, pltpu.roll(ch, LANES - 1, 1), pltpu.roll(ch, 1, 1))
        outs.append(ch * cos + partner * sin)
    return outs[0] if len(outs) == 1 else jnp.concatenate(outs, axis=1)


def _proj_kernel(x_ref, ctx_ref, mod_ref, gn_ref, w_ref, cos_ref, sin_ref, qna_ref, kna_ref, qnb_ref, knb_ref,
                 gmat_ref, qat_ref, ka_ref, vat_ref, qbt_ref, kb_ref, vbt_ref):
    b = pl.program_id(0)
    j = pl.program_id(1)
    is_ctx = j == 0
    xt = jnp.where(is_ctx, ctx_ref[0], x_ref[0])
    mrow = jnp.where(is_ctx, pl.num_programs(0), b)
    mod = mod_ref[mrow]
    h = _rms_rows(xt, gn_ref[...], EPS) * (1.0 + mod[1:2, :]) + mod[0:1, :]
    y = _dot(h.astype(BF16), w_ref[...])
    cos = jnp.where(is_ctx, 1.0, cos_ref[...])
    sin = jnp.where(is_ctx, 0.0, sin_ref[...])
    gmat = gmat_ref[...]
    scale = HEAD_DIM ** -0.5

    o = 0
    qa = _rope(_group_rms(y[:, o:o + A_Q_COLS], gmat, qna_ref[...]), cos, sin) * scale
    o += A_Q_COLS
    ka = _rope(_group_rms(y[:, o:o + A_KV_COLS], gmat, kna_ref[...]), cos, sin)
    o += A_KV_COLS
    va = y[:, o:o + A_KV_COLS]
    o += A_KV_COLS
    qb = _rope(_group_rms(y[:, o:o + B_QK_COLS], gmat, qnb_ref[...]), cos, sin) * scale
    o += B_QK_COLS
    kb = _rope(_group_rms(y[:, o:o + B_QK_COLS], gmat, knb_ref[...]), cos, sin)
    o += B_QK_COLS
    vb = y[:, o:o + B_WIDTH]

    @pl.when(j > 0)
    def _():
        qat_ref[0] = qa.T.astype(BF16)
        qbt_ref[0] = qb.T.astype(BF16)

    for g in range(A_KV_HEADS):
        ka_ref[0, g] = ka[:, g * HEAD_DIM:(g + 1) * HEAD_DIM].astype(BF16)
    for g in range(2 * B_HEADS):
        kb_ref[0, g] = kb[:, g * HEAD_DIM:(g + 1) * HEAD_DIM].astype(BF16)
    vat_ref[0, 0] = va.T.astype(BF16)
    vbt_ref[0, 0] = vb.T.astype(BF16)


def _project(x, ctx, mod, gn, w_qkv, cos, sin, qna, kna, qnb, knb, gmat):
    bsz, s, d = x.shape
    n_lat = s // TOK_TILE
    n_key = n_lat + 1
    lk = n_key * TOK_TILE
    lat = lambda b, j: (b, jnp.maximum(j - 1, 0), 0)
    const2 = lambda b, j: (0, 0)
    tab = lambda b, j: (jnp.maximum(j - 1, 0), 0)
    return pl.pallas_call(
        _proj_kernel,
        grid=(bsz, n_key),
        in_specs=[
            pl.BlockSpec((1, TOK_TILE, d), lat),
            pl.BlockSpec((1, TOK_TILE, d), lambda b, j: (b, 0, 0)),
            pl.BlockSpec(mod.shape, lambda b, j: (0, 0, 0)),
            pl.BlockSpec((1, d), const2),
            pl.BlockSpec((d, QKV_COLS), const2),
            pl.BlockSpec((TOK_TILE, LANES), tab),
            pl.BlockSpec((TOK_TILE, LANES), tab),
            pl.BlockSpec((1, LANES), const2),
            pl.BlockSpec((1, LANES), const2),
            pl.BlockSpec((1, LANES), const2),
            pl.BlockSpec((1, LANES), const2),
            pl.BlockSpec((MXU_DIM, MXU_DIM), const2),
        ],
        out_specs=[
            pl.BlockSpec((1, A_Q_COLS, TOK_TILE), lambda b, j: (b, 0, jnp.maximum(j - 1, 0))),
            pl.BlockSpec((1, A_KV_HEADS, TOK_TILE, HEAD_DIM), lambda b, j: (b, 0, j, 0)),
            pl.BlockSpec((1, 1, A_KV_COLS, TOK_TILE), lambda b, j: (b, j, 0, 0)),
            pl.BlockSpec((1, B_QK_COLS, TOK_TILE), lambda b, j: (b, 0, jnp.maximum(j - 1, 0))),
            pl.BlockSpec((1, 2 * B_HEADS, TOK_TILE, HEAD_DIM), lambda b, j: (b, 0, j, 0)),
            pl.BlockSpec((1, 1, B_WIDTH, TOK_TILE), lambda b, j: (b, j, 0, 0)),
        ],
        out_shape=[
            jax.ShapeDtypeStruct((bsz, A_Q_COLS, s), BF16),
            jax.ShapeDtypeStruct((bsz, A_KV_HEADS, lk, HEAD_DIM), BF16),
            jax.ShapeDtypeStruct((bsz, n_key, A_KV_COLS, TOK_TILE), BF16),
            jax.ShapeDtypeStruct((bsz, B_QK_COLS, s), BF16),
            jax.ShapeDtypeStruct((bsz, 2 * B_HEADS, lk, HEAD_DIM), BF16),
            jax.ShapeDtypeStruct((bsz, n_key, B_WIDTH, TOK_TILE), BF16),
        ],
        compiler_params=pltpu.CompilerParams(vmem_limit_bytes=VMEM_LIMIT),
        name="proj",
    )(x, ctx, mod, gn, w_qkv, cos, sin, qna, kna, qnb, knb, gmat)


def _flash_update(kc, qt, vc, m, l, acc):
    s = _dot(kc, qt)
    m_new = jnp.maximum(m, jnp.max(s, axis=0, keepdims=True))
    alpha = jnp.exp(m - m_new)
    p = jnp.exp(s - m_new)
    l = alpha * l + jnp.sum(p, axis=0, keepdims=True)
    acc = alpha * acc + _dot(vc, p.astype(BF16))
    return m_new, l, acc


def _flash_init(dv, tq):
    return (jnp.full((1, tq), -jnp.inf, F32), jnp.zeros((1, tq), F32), jnp.zeros((dv, tq), F32))


def _attn_a_kernel(qt_ref, k_ref, vt_ref, o_ref):
    n_chunks = vt_ref.shape[1]
    tq = qt_ref.shape[2]
    qts = [qt_ref[0, h * HEAD_DIM:(h + 1) * HEAD_DIM, :] for h in range(A_GROUP)]

    def body(c, carry):
        start = pl.multiple_of(c * TOK_TILE, TOK_TILE)
        kc = k_ref[0, 0, pl.ds(start, TOK_TILE), :]
        vc = vt_ref[0, c]
        return tuple(_flash_update(kc, qts[h], vc, *carry[h]) for h in range(A_GROUP))

    init = tuple(_flash_init(HEAD_DIM, tq) for _ in range(A_GROUP))
    res = lax.fori_loop(0, n_chunks, body, init)
    outs = [(acc / l).T for (_, l, acc) in res]
    o_ref[0] = jnp.concatenate(outs, axis=1).astype(o_ref.dtype)


def _attn_a(qat, ka, vat):
    bsz, _, s = qat.shape
    lk = ka.shape[2]
    n_chunks = vat.shape[1]
    gcols = A_GROUP * HEAD_DIM
    return pl.pallas_call(
        _attn_a_kernel,
        grid=(bsz, A_KV_HEADS, s // TQ),
        in_specs=[
            pl.BlockSpec((1, gcols, TQ), lambda b, g, i: (b, g, i)),
            pl.BlockSpec((1, 1, lk, HEAD_DIM), lambda b, g, i: (b, g, 0, 0)),
            pl.BlockSpec((1, n_chunks, HEAD_DIM, TOK_TILE), lambda b, g, i: (b, 0, g, 0)),
        ],
        out_specs=pl.BlockSpec((1, TQ, gcols), lambda b, g, i: (b, i, g)),
        out_shape=jax.ShapeDtypeStruct((bsz, s, A_Q_COLS), BF16),
        compiler_params=pltpu.CompilerParams(vmem_limit_bytes=VMEM_LIMIT),
        name="attn_a",
    )(qat, ka, vat)


def _attn_b_kernel(qt_ref, k_ref, vt_ref, lq1_ref, lk1_ref, lq2_ref, lk2_ref, sub_ref, o_ref):
    n_chunks = vt_ref.shape[1]
    tq = qt_ref.shape[2]
    qts = [qt_ref[0, m * HEAD_DIM:(m + 1) * HEAD_DIM, :] for m in range(2)]

    def body(c, carry):
        start = pl.multiple_of(c * TOK_TILE, TOK_TILE)
        vc = vt_ref[0, c]
        return tuple(_flash_update(k_ref[0, m, pl.ds(start, TOK_TILE), :], qts[m], vc, *carry[m]) for m in range(2))

    init = tuple(_flash_init(B_V_DIM, tq) for _ in range(2))
    (_, l1, acc1), (_, l2, acc2) = lax.fori_loop(0, n_chunks, body, init)
    lam = (jnp.exp(jnp.sum(lq1_ref[...] * lk1_ref[...], axis=1, keepdims=True))
           - jnp.exp(jnp.sum(lq2_ref[...] * lk2_ref[...], axis=1, keepdims=True)) + LAM_INIT)
    o = acc1 / l1 - lam * (acc2 / l2)
    ms = jnp.mean(o * o, axis=0, keepdims=True)
    o = o * lax.rsqrt(ms + SUBLN_EPS) * sub_ref[...] * (1.0 - LAM_INIT)
    o_ref[0] = o.T.astype(o_ref.dtype)


def _attn_b(qbt, kb, vbt, lq1, lk1, lq2, lk2, subln):
    bsz, _, s = qbt.shape
    lk = kb.shape[2]
    n_chunks = vbt.shape[1]
    vec = pl.BlockSpec((1, HEAD_DIM), lambda b, h, i: (0, 0))
    return pl.pallas_call(
        _attn_b_kernel,
        grid=(bsz, B_HEADS, s // TQ),
        in_specs=[
            pl.BlockSpec((1, 2 * HEAD_DIM, TQ), lambda b, h, i: (b, h, i)),
            pl.BlockSpec((1, 2, lk, HEAD_DIM), lambda b, h, i: (b, h, 0, 0)),
            pl.BlockSpec((1, n_chunks, B_V_DIM, TOK_TILE), lambda b, h, i: (b, 0, h, 0)),
            vec, vec, vec, vec,
            pl.BlockSpec((B_V_DIM, 1), lambda b, h, i: (0, 0)),
        ],
        out_specs=pl.BlockSpec((1, TQ, B_V_DIM), lambda b, h, i: (b, i, h)),
        out_shape=jax.ShapeDtypeStruct((bsz, s, B_WIDTH), BF16),
        compiler_params=pltpu.CompilerParams(vmem_limit_bytes=VMEM_LIMIT),
        name="attn_b",
    )(qbt, kb, vbt, lq1, lk1, lq2, lk2, subln)


def _merge_kernel(tiles_per_sample, x_ref, ya_ref, yb_ref, mod_ref, gna_ref, gnm_ref, wg_ref, woa_ref, wob_ref,
                  wout_ref, rw_ref, rb_ref, ltri_ref, x1_ref, h2_ref, route_ref, cnt_ref, carry_ref):
    i = pl.program_id(0)
    b = i // tiles_per_sample
    mod = mod_ref[b]
    x = x_ref[...]
    d = x.shape[1]
    h = _rms_rows(x, gna_ref[...], EPS) * (1.0 + mod[1:2, :]) + mod[0:1, :]
    gates = _dot(h.astype(BF16), wg_ref[...])
    ga = jax.nn.sigmoid(gates[:, :d])
    gb = jax.nn.sigmoid(gates[:, d:])
    merged = ga * _dot(ya_ref[...], woa_ref[...]) + gb * _dot(yb_ref[...], wob_ref[...])
    x1 = x + mod[2:3, :] * _dot(merged.astype(BF16), wout_ref[...])
    x1_ref[...] = x1
    h2 = _rms_rows(x1, gnm_ref[...], EPS) * (1.0 + mod[4:5, :]) + mod[3:4, :]
    h2_ref[...] = h2

    logits = _dot3(h2, rw_ref[...]) + rb_ref[...]
    rows = logits.shape[0]
    eiota = lax.broadcasted_iota(I32, (rows, N_EXPERTS), 1)
    work = logits
    idxs, vals = [], []
    for _ in range(TOP_K):
        mx = jnp.max(work, axis=1, keepdims=True)
        idx = jnp.min(jnp.where(work == mx, eiota, N_EXPERTS), axis=1, keepdims=True)
        idxs.append(idx)
        vals.append(mx)
        work = jnp.where(eiota == idx, -jnp.inf, work)
    exps = [jnp.exp(v - vals[0]) for v in vals]
    denom = exps[0] + exps[1] + exps[2] + exps[3]
    weights = [e / denom for e in exps]

    @pl.when(i == 0)
    def _():
        carry_ref[...] = jnp.zeros_like(carry_ref)

    hits = [(eiota == idx) for idx in idxs]
    onehot = (hits[0] | hits[1] | hits[2] | hits[3]).astype(F32)
    before = _dot(ltri_ref[...], onehot.astype(BF16)) + carry_ref[...]
    ranks = [jnp.sum(jnp.where(hit, before, 0.0), axis=1, keepdims=True) for hit in hits]
    carry = carry_ref[...] + jnp.sum(onehot, axis=0, keepdims=True)
    carry_ref[...] = carry
    cnt_ref[...] = carry

    cols = [c.astype(F32) for c in idxs] + weights + ranks
    riota = lax.broadcasted_iota(I32, route_ref.shape, 1)
    route = jnp.zeros(route_ref.shape, F32)
    for k, col in enumerate(cols):
        route = jnp.where(riota == k, col, route)
    route_ref[...] = route


def _merge(x2, ya2, yb2, mod, gna, gnm, wg, woa, wob, wout, rw, rb, ltri, tiles_per_sample):
    t, d = x2.shape
    const2 = lambda i: (0, 0)
    row = lambda i: (i, 0)
    full = lambda a: pl.BlockSpec(a.shape, const2)
    return pl.pallas_call(
        functools.partial(_merge_kernel, tiles_per_sample),
        grid=(t // TOK_TILE,),
        in_specs=[
            pl.BlockSpec((TOK_TILE, d), row),
            pl.BlockSpec((TOK_TILE, ya2.shape[1]), row),
            pl.BlockSpec((TOK_TILE, yb2.shape[1]), row),
            pl.BlockSpec(mod.shape, lambda i: (0, 0, 0)),
            full(gna), full(gnm), full(wg), full(woa), full(wob), full(wout), full(rw), full(rb), full(ltri),
        ],
        out_specs=[
            pl.BlockSpec((TOK_TILE, d), row),
            pl.BlockSpec((TOK_TILE, d), row),
            pl.BlockSpec((TOK_TILE, 3 * TOP_K + 4), row),
            pl.BlockSpec((1, N_EXPERTS), const2),
        ],
        out_shape=[
            jax.ShapeDtypeStruct((t, d), F32),
            jax.ShapeDtypeStruct((t, d), F32),
            jax.ShapeDtypeStruct((t, 3 * TOP_K + 4), F32),
            jax.ShapeDtypeStruct((1, N_EXPERTS), F32),
        ],
        scratch_shapes=[pltpu.VMEM((1, N_EXPERTS), F32)],
        compiler_params=pltpu.CompilerParams(vmem_limit_bytes=VMEM_LIMIT),
        name="merge",
    )(x2, ya2, yb2, mod, gna, gnm, wg, woa, wob, wout, rw, rb, ltri)


def _row_copy(src_hbm, src_row, dst, dst_row, sem):
    return pltpu.make_async_copy(src_hbm.at[pl.ds(src_row, 1)], dst.at[pl.ds(dst_row, 1)], sem)


def _dispatch_kernel(pos_ref, h_hbm, xb_in, xb_out, sem):
    del xb_in
    base = pl.program_id(0) * TOK_TILE

    def issue(t, carry):
        tok = base + t
        for k in range(TOP_K):
            _row_copy(h_hbm, tok, xb_out, pos_ref[tok * TOP_K + k], sem).start()
        return carry

    lax.fori_loop(0, TOK_TILE, issue, 0)

    def drain(t, carry):
        for k in range(TOP_K):
            _row_copy(h_hbm, 0, xb_out, 0, sem).wait()
        return carry

    lax.fori_loop(0, TOK_TILE, drain, 0)


def _dispatch(pos_flat, h2, xb_init):
    t = h2.shape[0]
    any_spec = pl.BlockSpec(memory_space=pl.ANY)
    return pl.pallas_call(
        _dispatch_kernel,
        grid_spec=pltpu.PrefetchScalarGridSpec(
            num_scalar_prefetch=1,
            grid=(t // TOK_TILE,),
            in_specs=[any_spec, any_spec],
            out_specs=any_spec,
            scratch_shapes=[pltpu.SemaphoreType.DMA(())],
        ),
        out_shape=jax.ShapeDtypeStruct(xb_init.shape, xb_init.dtype),
        input_output_aliases={2: 0},
        name="dispatch",
    )(pos_flat, h2, xb_init)


def _ffn_kernel(be_ref, nu_ref, x_ref, wg_ref, bg_ref, wu_ref, bu_ref, wd_ref, bd_ref, o_ref, wg_s, wu_s, wd_s):
    j = pl.program_id(0)
    prev = be_ref[jnp.maximum(j - 1, 0)]
    fresh = (j == 0) | (be_ref[j] != prev)

    @pl.when(fresh)
    def _():
        wg_s[...] = wg_ref[0].astype(BF16)
        wu_s[...] = wu_ref[0].astype(BF16)
        wd_s[...] = wd_ref[0].astype(BF16)

    @pl.when(j < nu_ref[0])
    def _():
        x = x_ref[...].astype(BF16)
        gate = jnp.minimum(_dot(x, wg_s[...]) + bg_ref[0], SWIGLU_LIMIT)
        up = jnp.clip(_dot(x, wu_s[...]) + bu_ref[0], -SWIGLU_LIMIT, SWIGLU_LIMIT)
        act = (up + 1.0) * (gate * jax.nn.sigmoid(SWIGLU_ALPHA * gate))
        o_ref[...] = _dot(act.astype(BF16), wd_s[...]) + bd_ref[0]

    @pl.when(j >= nu_ref[0])
    def _():
        o_ref[...] = jnp.zeros_like(o_ref)


def _ffn(block_e, n_used, xb, w_gate, b_gate, w_up, b_up, w_down, b_down):
    rows, d = xb.shape
    de = w_gate.shape[2]
    n_blocks = rows // FFN_ROWS
    xrow = lambda j, be, nu: (jnp.minimum(j, nu[0] - 1), 0)
    wsel = lambda j, be, nu: (be[j], 0, 0)
    return pl.pallas_call(
        _ffn_kernel,
        grid_spec=pltpu.PrefetchScalarGridSpec(
            num_scalar_prefetch=2,
            grid=(n_blocks,),
            in_specs=[
                pl.BlockSpec((FFN_ROWS, d), xrow),
                pl.BlockSpec((1, d, de), wsel),
                pl.BlockSpec((1, 1, de), wsel),
                pl.BlockSpec((1, d, de), wsel),
                pl.BlockSpec((1, 1, de), wsel),
                pl.BlockSpec((1, de, d), wsel),
                pl.BlockSpec((1, 1, d), wsel),
            ],
            out_specs=pl.BlockSpec((FFN_ROWS, d), lambda j, be, nu: (j, 0)),
            scratch_shapes=[pltpu.VMEM((d, de), BF16), pltpu.VMEM((d, de), BF16), pltpu.VMEM((de, d), BF16)],
        ),
        out_shape=jax.ShapeDtypeStruct((rows, d), F32),
        compiler_params=pltpu.CompilerParams(vmem_limit_bytes=VMEM_LIMIT),
        name="ffn",
    )(block_e, n_used, xb, w_gate, b_gate, w_up, b_up, w_down, b_down)


def _combine_kernel(tiles_per_sample, pos_ref, x1_ref, route_ref, mod_ref, yb_hbm, o_ref, buf, sem):
    i = pl.program_id(0)
    n = pl.num_programs(0)
    rows = TOP_K * CMB_TOK

    def issue(tile, slot):
        def body(r, carry):
            _row_copy(yb_hbm, pos_ref[tile * rows + r], buf.at[slot], r, sem.at[slot]).start()
            return carry
        lax.fori_loop(0, rows, body, 0)

    @pl.when(i == 0)
    def _():
        issue(0, 0)

    @pl.when(i + 1 < n)
    def _():
        issue(i + 1, (i + 1) % 2)

    slot = i % 2

    def drain(r, carry):
        _row_copy(yb_hbm, 0, buf.at[slot], 0, sem.at[slot]).wait()
        return carry

    lax.fori_loop(0, rows, drain, 0)

    b = i // tiles_per_sample
    g_m = mod_ref[b][5:6, :]
    route = route_ref[...]
    acc = jnp.zeros(x1_ref.shape, F32)
    for k in range(TOP_K):
        acc = acc + route[:, TOP_K + k:TOP_K + k + 1] * buf[slot, k * CMB_TOK:(k + 1) * CMB_TOK, :]
    o_ref[...] = x1_ref[...] + g_m * acc


def _combine(pos_tiles, x1, route, mod, yb, tiles_per_sample):
    t, d = x1.shape
    row = lambda i, pos: (i, 0)
    return pl.pallas_call(
        functools.partial(_combine_kernel, tiles_per_sample),
        grid_spec=pltpu.PrefetchScalarGridSpec(
            num_scalar_prefetch=1,
            grid=(t // CMB_TOK,),
            in_specs=[
                pl.BlockSpec((CMB_TOK, d), row),
                pl.BlockSpec((CMB_TOK, route.shape[1]), row),
                pl.BlockSpec(mod.shape, lambda i, pos: (0, 0, 0)),
                pl.BlockSpec(memory_space=pl.ANY),
            ],
            out_specs=pl.BlockSpec((CMB_TOK, d), row),
            scratch_shapes=[pltpu.VMEM((2, TOP_K * CMB_TOK, d), F32), pltpu.SemaphoreType.DMA((2,))],
        ),
        out_shape=jax.ShapeDtypeStruct((t, d), F32),
        compiler_params=pltpu.CompilerParams(vmem_limit_bytes=VMEM_LIMIT),
        name="combine",
    )(pos_tiles, x1, route, mod, yb)


def _rope_tables(s):
    rows = s // GRID_W
    row = jnp.repeat(jnp.arange(rows, dtype=F32), GRID_W)
    col = jnp.tile(jnp.arange(GRID_W, dtype=F32), rows)
    inv = 1.0 / (ROPE_THETA ** (jnp.arange(AXIS_PAIRS, dtype=F32) / AXIS_PAIRS))
    ang = jnp.concatenate([row[:, None] * inv, col[:, None] * inv], axis=-1)
    cos, sin = jnp.cos(ang), jnp.sin(ang)
    cos_i = jnp.repeat(cos, 2, axis=-1)
    sin_i = jnp.stack([-sin, sin], axis=-1).reshape(s, HEAD_DIM)
    reps = LANES // HEAD_DIM
    return jnp.tile(cos_i, (1, reps)), jnp.tile(sin_i, (1, reps))


def kernel(x, c, ctx, c_ctx, w_ada, b_ada, norm_attn, w_in, q_norm_a, k_norm_a, q_norm_b, k_norm_b, lambda_q1,
           lambda_k1, lambda_q2, lambda_k2, subln_b, w_oa, w_ob, w_out, norm_mlp, router_w, router_b, w_gate,
           b_gate, w_up, b_up, w_down, b_down):
    bsz, s, d = x.shape
    assert ctx.shape[1] == TOK_TILE and s % TOK_TILE == 0 and s % TQ == 0 and s % GRID_W == 0
    assert w_ada.shape[0] == 1, "single-layer block"
    t = bsz * s
    assert t % CMB_TOK == 0 and (t * TOP_K) % FFN_ROWS == 0

    mod_rows = 8
    assert bsz + 1 <= mod_rows
    cvec = jnp.concatenate([c, c_ctx[None, :], jnp.zeros((mod_rows - bsz - 1, d), F32)], axis=0)
    mod = _adaln(cvec, w_ada[0], b_ada[0][None, :]).reshape(mod_rows, N_MOD, d)

    cos, sin = _rope_tables(s)
    tile_gain = lambda g: jnp.tile(g, LANES // HEAD_DIM)[None, :]
    gidx = jnp.arange(MXU_DIM) // HEAD_DIM
    gmat = jnp.where(gidx[:, None] == gidx[None, :], 1.0 / HEAD_DIM, 0.0).astype(BF16)
    w_qkv = w_in[0][:, :QKV_COLS].astype(BF16)
    w_g = w_in[0][:, QKV_COLS:].astype(BF16)

    qat, ka, vat, qbt, kb, vbt = _project(
        x, ctx, mod, norm_attn, w_qkv, cos, sin, tile_gain(q_norm_a[0]), tile_gain(k_norm_a[0]),
        tile_gain(q_norm_b[0]), tile_gain(k_norm_b[0]), gmat)

    ya = _attn_a(qat, ka, vat)
    yb = _attn_b(qbt, kb, vbt, lambda_q1, lambda_k1, lambda_q2, lambda_k2, subln_b[0][:, None])

    tri = jnp.arange(TOK_TILE)
    ltri = (tri[None, :] < tri[:, None]).astype(BF16)
    x1, h2, route, counts = _merge(
        x.reshape(t, d), ya.reshape(t, A_Q_COLS), yb.reshape(t, B_WIDTH), mod, norm_attn, norm_mlp, w_g,
        w_oa[0].astype(BF16), w_ob[0].astype(BF16), w_out[0].astype(BF16), router_w[0], router_b, ltri,
        s // TOK_TILE)

    counts = counts[0].astype(I32)
    padded = (counts + FFN_ROWS - 1) // FFN_ROWS * FFN_ROWS
    pad_end = jnp.cumsum(padded)
    pad_start = pad_end - padded
    top_e = route[:, :TOP_K].astype(I32)
    rank = route[:, 2 * TOP_K:3 * TOP_K].astype(I32)
    pos = pad_start[top_e] + rank
    n_blocks = t * TOP_K // FFN_ROWS + N_EXPERTS
    block_e = jnp.minimum(jnp.searchsorted(pad_end, jnp.arange(n_blocks, dtype=I32) * FFN_ROWS, side='right'),
                          N_EXPERTS - 1).astype(I32)
    n_used = (pad_end[-1:] // FFN_ROWS).astype(I32)

    xb = _dispatch(pos.reshape(-1), h2, jnp.zeros((n_blocks * FFN_ROWS, d), F32))
    de = w_gate.shape[-1]
    yb_rows = _ffn(block_e, n_used, xb, w_gate[0], b_gate[0].reshape(N_EXPERTS, 1, de), w_up[0],
                   b_up[0].reshape(N_EXPERTS, 1, de), w_down[0], b_down[0].reshape(N_EXPERTS, 1, d))
    pos_tiles = pos.reshape(t // CMB_TOK, CMB_TOK, TOP_K).transpose(0, 2, 1).reshape(-1)
    out = _combine(pos_tiles, x1, route, mod, yb_rows, s // CMB_TOK)
    return out.reshape(bsz, s, d)
```

```python
import functools
import math

import jax
import jax.numpy as jnp
from jax import lax
from jax.experimental import pallas as pl
from jax.experimental.pallas import tpu as pltpu

F32 = jnp.float32
BF16 = jnp.bfloat16
I32 = jnp.int32

GRID_W = 64
HEAD_DIM = 64
A_HEADS = 8
A_KV_HEADS = 2
A_GROUP = A_HEADS // A_KV_HEADS
B_HEADS = 4
B_V_DIM = 2 * HEAD_DIM
N_EXPERTS = 32
TOP_K = 4
N_MOD = 6
ROPE_THETA = 10000.0
AXIS_PAIRS = HEAD_DIM // 4
SWIGLU_LIMIT = 7.0
SWIGLU_ALPHA = 1.702
EPS = 1e-6
SUBLN_EPS = 1e-5
LAM_INIT = 0.8 - 0.6 * math.exp(-0.3 * 0)

A_Q_COLS = A_HEADS * HEAD_DIM
A_KV_COLS = A_KV_HEADS * HEAD_DIM
B_QK_COLS = B_HEADS * 2 * HEAD_DIM
B_WIDTH = B_HEADS * B_V_DIM
QKV_COLS = A_Q_COLS + 2 * A_KV_COLS + 2 * B_QK_COLS + B_WIDTH

LANES = 128
MXU_DIM = 256
TOK_TILE = 256
TQ_A = 128
TQ_B = 512
ADA_COLS = 1536
FFN_ROWS = 256
CMB_TOK = 128
VMEM_LIMIT = 56 * 1024 * 1024


def _dot(a, b):
    return jnp.dot(a, b, preferred_element_type=F32)


def _split_bf16(a):
    hi = a.astype(BF16)
    lo = (a - hi.astype(F32)).astype(BF16)
    return hi, lo


def _dot3(a, b):
    a_hi, a_lo = _split_bf16(a)
    b_hi, b_lo = _split_bf16(b)
    return _dot(a_hi, b_hi) + (_dot(a_hi, b_lo) + _dot(a_lo, b_hi))


def _rms_rows(x, gain, eps):
    ms = jnp.mean(x * x, axis=-1, keepdims=True)
    return x * lax.rsqrt(ms + eps) * gain


def _adaln_kernel(c_ref, w_ref, b_ref, o_ref):
    c = c_ref[...]
    a = c * jax.nn.sigmoid(c)
    o_ref[...] = _dot3(a, w_ref[...]) + b_ref[...]


def _adaln(cvec, w_ada, b_ada):
    rows, d = cvec.shape
    n = w_ada.shape[1]
    return pl.pallas_call(
        _adaln_kernel,
        grid=(n // ADA_COLS,),
        in_specs=[
            pl.BlockSpec((rows, d), lambda j: (0, 0)),
            pl.BlockSpec((d, ADA_COLS), lambda j: (0, j)),
            pl.BlockSpec((1, ADA_COLS), lambda j: (0, j)),
        ],
        out_specs=pl.BlockSpec((rows, ADA_COLS), lambda j: (0, j)),
        out_shape=jax.ShapeDtypeStruct((rows, n), F32),
        compiler_params=pltpu.CompilerParams(vmem_limit_bytes=VMEM_LIMIT),
        name="adaln",
    )(cvec, w_ada, b_ada)


def _group_rms(y, gmat, gain):
    w_total = y.shape[1]
    outs = []
    for c0 in range(0, w_total, MXU_DIM):
        w = min(MXU_DIM, w_total - c0)
        yc = y[:, c0:c0 + w]
        ms = _dot((yc * yc).astype(BF16), gmat[:w, :w])
        reps = w // LANES
        g = gain if reps == 1 else jnp.concatenate([gain] * reps, axis=1)
        outs.append(yc * lax.rsqrt(ms + EPS) * g)
    return outs[0] if len(outs) == 1 else jnp.concatenate(outs, axis=1)


def _rope(y, cos, sin):
    rows = y.shape[0]
    lane = lax.broadcasted_iota(I32, (rows, LANES), 1)
    even = (lane % 2) == 0
    outs = []
    for c in range(y.shape[1] // LANES):
        ch = y[:, c * LANES:(c + 1) * LANES]
        partner = jnp.where(even, pltpu.roll(ch, LANES - 1, 1), pltpu.roll(ch, 1, 1))
        outs.append(ch * cos + partner * sin)
    return outs[0] if len(outs) == 1 else jnp.concatenate(outs, axis=1)


def _proj_kernel(x_ref, ctx_ref, mod_ref, gn_ref, w_ref, cos_ref, sin_ref, qna_ref, kna_ref, qnb_ref, knb_ref,
                 gmat_ref, qat_ref, ka_ref, vat_ref, qbt_ref, kb_ref, vbt_ref):
    b = pl.program_id(0)
    j = pl.program_id(1)
    is_ctx = j == 0
    xt = jnp.where(is_ctx, ctx_ref[0], x_ref[0])
    mrow = jnp.where(is_ctx, pl.num_programs(0), b)
    mod = mod_ref[mrow]
    h = _rms_rows(xt, gn_ref[...], EPS) * (1.0 + mod[1:2, :]) + mod[0:1, :]
    y = _dot(h.astype(BF16), w_ref[...])
    cos = jnp.where(is_ctx, 1.0, cos_ref[...])
    sin = jnp.where(is_ctx, 0.0, sin_ref[...])
    gmat = gmat_ref[...]
    scale = HEAD_DIM ** -0.5 * math.log2(math.e)

    o = 0
    qa = _rope(_group_rms(y[:, o:o + A_Q_COLS], gmat, qna_ref[...]), cos, sin) * scale
    o += A_Q_COLS
    ka = _rope(_group_rms(y[:, o:o + A_KV_COLS], gmat, kna_ref[...]), cos, sin)
    o += A_KV_COLS
    va = y[:, o:o + A_KV_COLS]
    o += A_KV_COLS
    qb = _rope(_group_rms(y[:, o:o + B_QK_COLS], gmat, qnb_ref[...]), cos, sin) * scale
    o += B_QK_COLS
    kb = _rope(_group_rms(y[:, o:o + B_QK_COLS], gmat, knb_ref[...]), cos, sin)
    o += B_QK_COLS
    vb = y[:, o:o + B_WIDTH]

    @pl.when(j > 0)
    def _():
        qat_ref[0] = qa.T.astype(BF16)
        qbt_ref[0] = qb.T.astype(BF16)

    ka_ref[0] = ka.astype(BF16)
    kb_ref[0] = kb.astype(BF16)
    vat_ref[0, 0] = va.T.astype(BF16)
    vbt_ref[0, 0] = vb.T.astype(BF16)


def _project(x, ctx, mod, gn, w_qkv, cos, sin, qna, kna, qnb, knb, gmat):
    bsz, s, d = x.shape
    n_lat = s // TOK_TILE
    n_key = n_lat + 1
    lk = n_key * TOK_TILE
    lat = lambda b, j: (b, jnp.maximum(j - 1, 0), 0)
    const2 = lambda b, j: (0, 0)
    tab = lambda b, j: (jnp.maximum(j - 1, 0), 0)
    return pl.pallas_call(
        _proj_kernel,
        grid=(bsz, n_key),
        in_specs=[
            pl.BlockSpec((1, TOK_TILE, d), lat),
            pl.BlockSpec((1, TOK_TILE, d), lambda b, j: (b, 0, 0)),
            pl.BlockSpec(mod.shape, lambda b, j: (0, 0, 0)),
            pl.BlockSpec((1, d), const2),
            pl.BlockSpec((d, QKV_COLS), const2),
            pl.BlockSpec((TOK_TILE, LANES), tab),
            pl.BlockSpec((TOK_TILE, LANES), tab),
            pl.BlockSpec((1, LANES), const2),
            pl.BlockSpec((1, LANES), const2),
            pl.BlockSpec((1, LANES), const2),
            pl.BlockSpec((1, LANES), const2),
            pl.BlockSpec((MXU_DIM, MXU_DIM), const2),
        ],
        out_specs=[
            pl.BlockSpec((1, A_Q_COLS, TOK_TILE), lambda b, j: (b, 0, jnp.maximum(j - 1, 0))),
            pl.BlockSpec((1, TOK_TILE, A_KV_COLS), lambda b, j: (b, j, 0)),
            pl.BlockSpec((1, 1, A_KV_COLS, TOK_TILE), lambda b, j: (b, j, 0, 0)),
            pl.BlockSpec((1, B_QK_COLS, TOK_TILE), lambda b, j: (b, 0, jnp.maximum(j - 1, 0))),
            pl.BlockSpec((1, TOK_TILE, B_QK_COLS), lambda b, j: (b, j, 0)),
            pl.BlockSpec((1, 1, B_WIDTH, TOK_TILE), lambda b, j: (b, j, 0, 0)),
        ],
        out_shape=[
            jax.ShapeDtypeStruct((bsz, A_Q_COLS, s), BF16),
            jax.ShapeDtypeStruct((bsz, lk, A_KV_COLS), BF16),
            jax.ShapeDtypeStruct((bsz, n_key, A_KV_COLS, TOK_TILE), BF16),
            jax.ShapeDtypeStruct((bsz, B_QK_COLS, s), BF16),
            jax.ShapeDtypeStruct((bsz, lk, B_QK_COLS), BF16),
            jax.ShapeDtypeStruct((bsz, n_key, B_WIDTH, TOK_TILE), BF16),
        ],
        compiler_params=pltpu.CompilerParams(vmem_limit_bytes=VMEM_LIMIT),
        name="proj",
    )(x, ctx, mod, gn, w_qkv, cos, sin, qna, kna, qnb, knb, gmat)


def _flash(k_ref, vt_ref, wq, pv, dv):
    n = wq.shape[1]

    def body(c, carry):
        m, l, acc = carry
        start = pl.multiple_of(c * TOK_TILE, TOK_TILE)
        s = _dot(k_ref[0, pl.ds(start, TOK_TILE), :], wq)
        m_new = jnp.maximum(m, jnp.max(s, axis=0, keepdims=True))
        alpha = jnp.exp2(m - m_new)
        p = jnp.exp2(s - m_new)
        l = alpha * l + jnp.sum(p, axis=0, keepdims=True)
        acc = alpha * acc + pv(vt_ref[0, c], p.astype(BF16))
        return m_new, l, acc

    init = (jnp.full((1, n), -jnp.inf, F32), jnp.zeros((1, n), F32), jnp.zeros((dv, n), F32))
    _, l, acc = lax.fori_loop(0, vt_ref.shape[1], body, init)
    return l, acc


def _attn_a_kernel(qt_ref, k_ref, vt_ref, o_ref):
    tq = qt_ref.shape[2]
    half = A_GROUP * tq
    zeros = jnp.zeros((HEAD_DIM, half), BF16)
    rows = []
    for g in range(A_KV_HEADS):
        heads = range(g * A_GROUP, (g + 1) * A_GROUP)
        qg = jnp.concatenate([qt_ref[0, h * HEAD_DIM:(h + 1) * HEAD_DIM, :] for h in heads], axis=1)
        rows.append(jnp.concatenate([qg if gg == g else zeros for gg in range(A_KV_HEADS)], axis=1))
    wq = jnp.concatenate(rows, axis=0)

    def pv(vc, p):
        return jnp.concatenate(
            [_dot(vc[g * HEAD_DIM:(g + 1) * HEAD_DIM], p[:, g * half:(g + 1) * half]) for g in range(A_KV_HEADS)],
            axis=1)

    l, acc = _flash(k_ref, vt_ref, wq, pv, HEAD_DIM)
    o = acc / l
    outs = [o[:, h * tq:(h + 1) * tq].T for h in range(A_HEADS)]
    o_ref[0] = jnp.concatenate(outs, axis=1).astype(o_ref.dtype)


def _attn_a(qat, ka, vat):
    bsz, _, s = qat.shape
    lk = ka.shape[1]
    n_chunks = vat.shape[1]
    return pl.pallas_call(
        _attn_a_kernel,
        grid=(bsz, s // TQ_A),
        in_specs=[
            pl.BlockSpec((1, A_Q_COLS, TQ_A), lambda b, i: (b, 0, i)),
            pl.BlockSpec((1, lk, A_KV_COLS), lambda b, i: (b, 0, 0)),
            pl.BlockSpec((1, n_chunks, A_KV_COLS, TOK_TILE), lambda b, i: (b, 0, 0, 0)),
        ],
        out_specs=pl.BlockSpec((1, TQ_A, A_Q_COLS), lambda b, i: (b, i, 0)),
        out_shape=jax.ShapeDtypeStruct((bsz, s, A_Q_COLS), BF16),
        compiler_params=pltpu.CompilerParams(vmem_limit_bytes=VMEM_LIMIT),
        name="attn_a",
    )(qat, ka, vat)


def _attn_b_kernel(qt_ref, k_ref, vt_ref, lq1_ref, lk1_ref, lq2_ref, lk2_ref, sub_ref, o_ref):
    tq = qt_ref.shape[2]
    zeros = jnp.zeros((HEAD_DIM, tq), BF16)
    q1 = qt_ref[0, :HEAD_DIM, :]
    q2 = qt_ref[0, HEAD_DIM:, :]
    wq = jnp.concatenate([jnp.concatenate([q1, zeros], axis=1), jnp.concatenate([zeros, q2], axis=1)], axis=0)
    l, acc = _flash(k_ref, vt_ref, wq, _dot, B_V_DIM)
    r = acc / l
    lam = (jnp.exp(jnp.sum(lq1_ref[...] * lk1_ref[...], axis=1, keepdims=True))
           - jnp.exp(jnp.sum(lq2_ref[...] * lk2_ref[...], axis=1, keepdims=True)) + LAM_INIT)
    o = r[:, :tq] - lam * r[:, tq:]
    ms = jnp.mean(o * o, axis=0, keepdims=True)
    o = o * lax.rsqrt(ms + SUBLN_EPS) * sub_ref[...] * (1.0 - LAM_INIT)
    o_ref[0] = o.T.astype(o_ref.dtype)


def _attn_b(qbt, kb, vbt, lq1, lk1, lq2, lk2, subln):
    bsz, _, s = qbt.shape
    lk = kb.shape[1]
    n_chunks = vbt.shape[1]
    vec = pl.BlockSpec((1, HEAD_DIM), lambda b, h, i: (0, 0))
    return pl.pallas_call(
        _attn_b_kernel,
        grid=(bsz, B_HEADS, s // TQ_B),
        in_specs=[
            pl.BlockSpec((1, 2 * HEAD_DIM, TQ_B), lambda b, h, i: (b, h, i)),
            pl.BlockSpec((1, lk, 2 * HEAD_DIM), lambda b, h, i: (b, 0, h)),
            pl.BlockSpec((1, n_chunks, B_V_DIM, TOK_TILE), lambda b, h, i: (b, 0, h, 0)),
            vec, vec, vec, vec,
            pl.BlockSpec((B_V_DIM, 1), lambda b, h, i: (0, 0)),
        ],
        out_specs=pl.BlockSpec((1, TQ_B, B_V_DIM), lambda b, h, i: (b, i, h)),
        out_shape=jax.ShapeDtypeStruct((bsz, s, B_WIDTH), BF16),
        compiler_params=pltpu.CompilerParams(vmem_limit_bytes=VMEM_LIMIT),
        name="attn_b",
    )(qbt, kb, vbt, lq1, lk1, lq2, lk2, subln)


def _merge_kernel(tiles_per_sample, x_ref, ya_ref, yb_ref, mod_ref, gna_ref, gnm_ref, wg_ref, woa_ref, wob_ref,
                  wout_ref, rw_ref, rb_ref, ltri_ref, x1_ref, h2_ref, route_ref, cnt_ref, carry_ref):
    i = pl.program_id(0)
    b = i // tiles_per_sample
    mod = mod_ref[b]
    x = x_ref[...]
    d = x.shape[1]
    h = _rms_rows(x, gna_ref[...], EPS) * (1.0 + mod[1:2, :]) + mod[0:1, :]
    gates = _dot(h.astype(BF16), wg_ref[...])
    ga = jax.nn.sigmoid(gates[:, :d])
    gb = jax.nn.sigmoid(gates[:, d:])
    merged = ga * _dot(ya_ref[...], woa_ref[...]) + gb * _dot(yb_ref[...], wob_ref[...])
    x1 = x + mod[2:3, :] * _dot(merged.astype(BF16), wout_ref[...])
    x1_ref[...] = x1
    h2 = _rms_rows(x1, gnm_ref[...], EPS) * (1.0 + mod[4:5, :]) + mod[3:4, :]
    h2_ref[...] = h2

    logits = _dot3(h2, rw_ref[...]) + rb_ref[...]
    rows = logits.shape[0]
    eiota = lax.broadcasted_iota(I32, (rows, N_EXPERTS), 1)
    work = logits
    idxs, vals = [], []
    for _ in range(TOP_K):
        mx = jnp.max(work, axis=1, keepdims=True)
        idx = jnp.min(jnp.where(work == mx, eiota, N_EXPERTS), axis=1, keepdims=True)
        idxs.append(idx)
        vals.append(mx)
        work = jnp.where(eiota == idx, -jnp.inf, work)
    exps = [jnp.exp(v - vals[0]) for v in vals]
    denom = exps[0] + exps[1] + exps[2] + exps[3]
    weights = [e / denom for e in exps]

    @pl.when(i == 0)
    def _():
        carry_ref[...] = jnp.zeros_like(carry_ref)

    hits = [(eiota == idx) for idx in idxs]
    onehot = (hits[0] | hits[1] | hits[2] | hits[3]).astype(F32)
    before = _dot(ltri_ref[...], onehot.astype(BF16)) + carry_ref[...]
    ranks = [jnp.sum(jnp.where(hit, before, 0.0), axis=1, keepdims=True) for hit in hits]
    carry = carry_ref[...] + jnp.sum(onehot, axis=0, keepdims=True)
    carry_ref[...] = carry
    cnt_ref[...] = carry

    cols = [c.astype(F32) for c in idxs] + weights + ranks
    riota = lax.broadcasted_iota(I32, route_ref.shape, 1)
    route = jnp.zeros(route_ref.shape, F32)
    for k, col in enumerate(cols):
        route = jnp.where(riota == k, col, route)
    route_ref[...] = route


def _merge(x2, ya2, yb2, mod, gna, gnm, wg, woa, wob, wout, rw, rb, ltri, tiles_per_sample):
    t, d = x2.shape
    const2 = lambda i: (0, 0)
    row = lambda i: (i, 0)
    full = lambda a: pl.BlockSpec(a.shape, const2)
    return pl.pallas_call(
        functools.partial(_merge_kernel, tiles_per_sample),
        grid=(t // TOK_TILE,),
        in_specs=[
            pl.BlockSpec((TOK_TILE, d), row),
            pl.BlockSpec((TOK_TILE, ya2.shape[1]), row),
            pl.BlockSpec((TOK_TILE, yb2.shape[1]), row),
            pl.BlockSpec(mod.shape, lambda i: (0, 0, 0)),
            full(gna), full(gnm), full(wg), full(woa), full(wob), full(wout), full(rw), full(rb), full(ltri),
        ],
        out_specs=[
            pl.BlockSpec((TOK_TILE, d), row),
            pl.BlockSpec((TOK_TILE, d), row),
            pl.BlockSpec((TOK_TILE, 3 * TOP_K + 4), row),
            pl.BlockSpec((1, N_EXPERTS), const2),
        ],
        out_shape=[
            jax.ShapeDtypeStruct((t, d), F32),
            jax.ShapeDtypeStruct((t, d), F32),
            jax.ShapeDtypeStruct((t, 3 * TOP_K + 4), F32),
            jax.ShapeDtypeStruct((1, N_EXPERTS), F32),
        ],
        scratch_shapes=[pltpu.VMEM((1, N_EXPERTS), F32)],
        compiler_params=pltpu.CompilerParams(vmem_limit_bytes=VMEM_LIMIT),
        name="merge",
    )(x2, ya2, yb2, mod, gna, gnm, wg, woa, wob, wout, rw, rb, ltri)


def _to_row_tiles(dst_ref, val):
    n, d = val.shape
    sub = d // LANES
    for c in range(sub):
        dst_ref[pl.ds(c, n, stride=sub), :] = val[:, c * LANES:(c + 1) * LANES]


def _from_row_tiles(src_ref, first_row, n, d):
    sub = d // LANES
    return jnp.concatenate([src_ref[pl.ds(first_row * sub + c, n, stride=sub), :] for c in range(sub)], axis=1)


def _row_tile_copy(src, src_row, dst, dst_row, sub, sem):
    src_at = pl.ds(pl.multiple_of(src_row * sub, sub), sub)
    dst_at = pl.ds(pl.multiple_of(dst_row * sub, sub), sub)
    return pltpu.make_async_copy(src.at[src_at], dst.at[dst_at], sem)


def _dispatch_kernel(pos_ref, h_ref, xb_in, xb_out, tiles, sem):
    del xb_in
    i = pl.program_id(0)
    n = pl.num_programs(0)
    slot = i % 2
    n_tok, d = h_ref.shape
    sub = d // LANES

    def drain(sl):
        for _ in range(TOP_K):
            pltpu.make_async_copy(tiles.at[sl], xb_out.at[pl.ds(0, n_tok * sub)], sem.at[sl]).wait()

    @pl.when(i >= 2)
    def _():
        drain(slot)

    _to_row_tiles(tiles.at[slot], h_ref[...])
    base = i * n_tok

    def issue(t, carry):
        for k in range(TOP_K):
            _row_tile_copy(tiles.at[slot], t, xb_out, pos_ref[(base + t) * TOP_K + k], sub, sem.at[slot]).start()
        return carry

    lax.fori_loop(0, n_tok, issue, 0)

    @pl.when((i == n - 1) & (n >= 2))
    def _():
        drain(1 - slot)

    @pl.when(i == n - 1)
    def _():
        drain(slot)


def _dispatch(pos_flat, h2, xb_init):
    t, d = h2.shape
    any_spec = pl.BlockSpec(memory_space=pl.ANY)
    return pl.pallas_call(
        _dispatch_kernel,
        grid_spec=pltpu.PrefetchScalarGridSpec(
            num_scalar_prefetch=1,
            grid=(t // TOK_TILE,),
            in_specs=[pl.BlockSpec((TOK_TILE, d), lambda i, pos: (i, 0)), any_spec],
            out_specs=any_spec,
            scratch_shapes=[pltpu.VMEM((2, TOK_TILE * d // LANES, LANES), F32), pltpu.SemaphoreType.DMA((2,))],
        ),
        out_shape=jax.ShapeDtypeStruct(xb_init.shape, xb_init.dtype),
        input_output_aliases={2: 0},
        compiler_params=pltpu.CompilerParams(vmem_limit_bytes=VMEM_LIMIT),
        name="dispatch",
    )(pos_flat, h2, xb_init)


def _ffn_kernel(be_ref, nu_ref, x_ref, wg_ref, bg_ref, wu_ref, bu_ref, wd_ref, bd_ref, o_ref, wg_s, wu_s, wd_s):
    j = pl.program_id(0)
    prev = be_ref[jnp.maximum(j - 1, 0)]
    fresh = (j == 0) | (be_ref[j] != prev)

    @pl.when(fresh)
    def _():
        wg_s[...] = wg_ref[0].astype(BF16)
        wu_s[...] = wu_ref[0].astype(BF16)
        wd_s[...] = wd_ref[0].astype(BF16)

    @pl.when(j < nu_ref[0])
    def _():
        x = _from_row_tiles(x_ref, 0, FFN_ROWS, wg_s.shape[0]).astype(BF16)
        gate = jnp.minimum(_dot(x, wg_s[...]) + bg_ref[0], SWIGLU_LIMIT)
        up = jnp.clip(_dot(x, wu_s[...]) + bu_ref[0], -SWIGLU_LIMIT, SWIGLU_LIMIT)
        act = (up + 1.0) * (gate * jax.nn.sigmoid(SWIGLU_ALPHA * gate))
        _to_row_tiles(o_ref, _dot(act.astype(BF16), wd_s[...]) + bd_ref[0])

    @pl.when(j >= nu_ref[0])
    def _():
        o_ref[...] = jnp.zeros_like(o_ref)


def _ffn(block_e, n_used, xb, w_gate, b_gate, w_up, b_up, w_down, b_down):
    d, de = w_gate.shape[1:]
    blk = FFN_ROWS * d // LANES
    n_blocks = xb.shape[0] // blk
    xrow = lambda j, be, nu: (jnp.minimum(j, nu[0] - 1), 0)
    wsel = lambda j, be, nu: (be[j], 0, 0)
    return pl.pallas_call(
        _ffn_kernel,
        grid_spec=pltpu.PrefetchScalarGridSpec(
            num_scalar_prefetch=2,
            grid=(n_blocks,),
            in_specs=[
                pl.BlockSpec((blk, LANES), xrow),
                pl.BlockSpec((1, d, de), wsel),
                pl.BlockSpec((1, 1, de), wsel),
                pl.BlockSpec((1, d, de), wsel),
                pl.BlockSpec((1, 1, de), wsel),
                pl.BlockSpec((1, de, d), wsel),
                pl.BlockSpec((1, 1, d), wsel),
            ],
            out_specs=pl.BlockSpec((blk, LANES), lambda j, be, nu: (j, 0)),
            scratch_shapes=[pltpu.VMEM((d, de), BF16), pltpu.VMEM((d, de), BF16), pltpu.VMEM((de, d), BF16)],
        ),
        out_shape=jax.ShapeDtypeStruct(xb.shape, F32),
        compiler_params=pltpu.CompilerParams(vmem_limit_bytes=VMEM_LIMIT),
        name="ffn",
    )(block_e, n_used, xb, w_gate, b_gate, w_up, b_up, w_down, b_down)


def _combine_kernel(tiles_per_sample, pos_ref, x1_ref, route_ref, mod_ref, yb_hbm, o_ref, buf, sem):
    i = pl.program_id(0)
    n = pl.num_programs(0)
    rows = TOP_K * CMB_TOK
    d = x1_ref.shape[1]
    sub = d // LANES

    def issue(tile, slot):
        def body(r, carry):
            _row_tile_copy(yb_hbm, pos_ref[tile * rows + r], buf.at[slot], r, sub, sem.at[slot]).start()
            return carry
        lax.fori_loop(0, rows, body, 0)

    @pl.when(i == 0)
    def _():
        issue(0, 0)

    @pl.when(i + 1 < n)
    def _():
        issue(i + 1, (i + 1) % 2)

    slot = i % 2
    pltpu.make_async_copy(yb_hbm.at[pl.ds(0, rows * sub)], buf.at[slot], sem.at[slot]).wait()

    b = i // tiles_per_sample
    g_m = mod_ref[b][5:6, :]
    route = route_ref[...]
    acc = jnp.zeros(x1_ref.shape, F32)
    for k in range(TOP_K):
        acc = acc + route[:, TOP_K + k:TOP_K + k + 1] * _from_row_tiles(buf.at[slot], k * CMB_TOK, CMB_TOK, d)
    o_ref[...] = x1_ref[...] + g_m * acc


def _combine(pos_tiles, x1, route, mod, yb, tiles_per_sample):
    t, d = x1.shape
    row = lambda i, pos: (i, 0)
    return pl.pallas_call(
        functools.partial(_combine_kernel, tiles_per_sample),
        grid_spec=pltpu.PrefetchScalarGridSpec(
            num_scalar_prefetch=1,
            grid=(t // CMB_TOK,),
            in_specs=[
                pl.BlockSpec((CMB_TOK, d), row),
                pl.BlockSpec((CMB_TOK, route.shape[1]), row),
                pl.BlockSpec(mod.shape, lambda i, pos: (0, 0, 0)),
                pl.BlockSpec(memory_space=pl.ANY),
            ],
            out_specs=pl.BlockSpec((CMB_TOK, d), row),
            scratch_shapes=[pltpu.VMEM((2, TOP_K * CMB_TOK * d // LANES, LANES), F32),
                            pltpu.SemaphoreType.DMA((2,))],
        ),
        out_shape=jax.ShapeDtypeStruct((t, d), F32),
        compiler_params=pltpu.CompilerParams(vmem_limit_bytes=VMEM_LIMIT),
        name="combine",
    )(pos_tiles, x1, route, mod, yb)


def _rope_tables(s):
    rows = s // GRID_W
    row = jnp.repeat(jnp.arange(rows, dtype=F32), GRID_W)
    col = jnp.tile(jnp.arange(GRID_W, dtype=F32), rows)
    inv = 1.0 / (ROPE_THETA ** (jnp.arange(AXIS_PAIRS, dtype=F32) / AXIS_PAIRS))
    ang = jnp.concatenate([row[:, None] * inv, col[:, None] * inv], axis=-1)
    cos, sin = jnp.cos(ang), jnp.sin(ang)
    cos_i = jnp.repeat(cos, 2, axis=-1)
    sin_i = jnp.stack([-sin, sin], axis=-1).reshape(s, HEAD_DIM)
    reps = LANES // HEAD_DIM
    return jnp.tile(cos_i, (1, reps)), jnp.tile(sin_i, (1, reps))


def kernel(x, c, ctx, c_ctx, w_ada, b_ada, norm_attn, w_in, q_norm_a, k_norm_a, q_norm_b, k_norm_b, lambda_q1,
           lambda_k1, lambda_q2, lambda_k2, subln_b, w_oa, w_ob, w_out, norm_mlp, router_w, router_b, w_gate,
           b_gate, w_up, b_up, w_down, b_down):
    bsz, s, d = x.shape
    assert ctx.shape[1] == TOK_TILE and s % TOK_TILE == 0 and s % TQ_A == 0 and s % TQ_B == 0 and s % GRID_W == 0
    assert w_ada.shape[0] == 1, "single-layer block"
    t = bsz * s
    assert t % CMB_TOK == 0 and (t * TOP_K) % FFN_ROWS == 0

    mod_rows = 8
    assert bsz + 1 <= mod_rows
    cvec = jnp.concatenate([c, c_ctx[None, :], jnp.zeros((mod_rows - bsz - 1, d), F32)], axis=0)
    mod = _adaln(cvec, w_ada[0], b_ada[0][None, :]).reshape(mod_rows, N_MOD, d)

    cos, sin = _rope_tables(s)
    tile_gain = lambda g: jnp.tile(g, LANES // HEAD_DIM)[None, :]
    gidx = jnp.arange(MXU_DIM) // HEAD_DIM
    gmat = jnp.where(gidx[:, None] == gidx[None, :], 1.0 / HEAD_DIM, 0.0).astype(BF16)
    w_qkv = w_in[0][:, :QKV_COLS].astype(BF16)
    w_g = w_in[0][:, QKV_COLS:].astype(BF16)

    qat, ka, vat, qbt, kb, vbt = _project(
        x, ctx, mod, norm_attn, w_qkv, cos, sin, tile_gain(q_norm_a[0]), tile_gain(k_norm_a[0]),
        tile_gain(q_norm_b[0]), tile_gain(k_norm_b[0]), gmat)

    ya = _attn_a(qat, ka, vat)
    yb = _attn_b(qbt, kb, vbt, lambda_q1, lambda_k1, lambda_q2, lambda_k2, subln_b[0][:, None])

    tri = jnp.arange(TOK_TILE)
    ltri = (tri[None, :] < tri[:, None]).astype(BF16)
    x1, h2, route, counts = _merge(
        x.reshape(t, d), ya.reshape(t, A_Q_COLS), yb.reshape(t, B_WIDTH), mod, norm_attn, norm_mlp, w_g,
        w_oa[0].astype(BF16), w_ob[0].astype(BF16), w_out[0].astype(BF16), router_w[0], router_b, ltri,
        s // TOK_TILE)

    counts = counts[0].astype(I32)
    padded = (counts + FFN_ROWS - 1) // FFN_ROWS * FFN_ROWS
    pad_end = jnp.cumsum(padded)
    pad_start = pad_end - padded
    top_e = route[:, :TOP_K].astype(I32)
    rank = route[:, 2 * TOP_K:3 * TOP_K].astype(I32)
    pos = pad_start[top_e] + rank
    n_blocks = t * TOP_K // FFN_ROWS + N_EXPERTS
    block_start = jnp.arange(n_blocks, dtype=I32) * FFN_ROWS
    block_e = jnp.minimum(jnp.sum((pad_end[None, :] <= block_start[:, None]).astype(I32), axis=1), N_EXPERTS - 1)
    n_used = (pad_end[-1:] // FFN_ROWS).astype(I32)

    xb = _dispatch(pos.reshape(-1), h2, jnp.zeros((n_blocks * FFN_ROWS * d // LANES, LANES), F32))
    de = w_gate.shape[-1]
    yb_rows = _ffn(block_e, n_used, xb, w_gate[0], b_gate[0].reshape(N_EXPERTS, 1, de), w_up[0],
                   b_up[0].reshape(N_EXPERTS, 1, de), w_down[0], b_down[0].reshape(N_EXPERTS, 1, d))
    pos_tiles = pos.reshape(t // CMB_TOK, CMB_TOK, TOP_K).transpose(0, 2, 1).reshape(-1)
    out = _combine(pos_tiles, x1, route, mod, yb_rows, s // CMB_TOK)
    return out.reshape(bsz, s, d)
```

```python
import functools
import math

import jax
import jax.numpy as jnp
from jax import lax
from jax.experimental import pallas as pl
from jax.experimental.pallas import tpu as pltpu

F32 = jnp.float32
BF16 = jnp.bfloat16
I32 = jnp.int32

GRID_W = 64
HEAD_DIM = 64
A_HEADS = 8
A_KV_HEADS = 2
A_GROUP = A_HEADS // A_KV_HEADS
B_HEADS = 4
B_V_DIM = 2 * HEAD_DIM
N_EXPERTS = 32
TOP_K = 4
N_MOD = 6
ROPE_THETA = 10000.0
AXIS_PAIRS = HEAD_DIM // 4
SWIGLU_LIMIT = 7.0
SWIGLU_ALPHA = 1.702
EPS = 1e-6
SUBLN_EPS = 1e-5
LAM_INIT = 0.8 - 0.6 * math.exp(-0.3 * 0)

A_Q_COLS = A_HEADS * HEAD_DIM
A_KV_COLS = A_KV_HEADS * HEAD_DIM
B_QK_COLS = B_HEADS * 2 * HEAD_DIM
B_WIDTH = B_HEADS * B_V_DIM
QKV_COLS = A_Q_COLS + 2 * A_KV_COLS + 2 * B_QK_COLS + B_WIDTH

LANES = 128
MXU_DIM = 256
TOK_TILE = 256
TQ_A = 128
TQ_B = 512
ONES_ROWS = 16
A_VROWS = HEAD_DIM + ONES_ROWS
B_VROWS = B_V_DIM + ONES_ROWS
ADA_COLS = 1536
FFN_ROWS = 256
CMB_TOK = 128
VMEM_LIMIT = 56 * 1024 * 1024


def _dot(a, b):
    return jnp.dot(a, b, preferred_element_type=F32)


def _split_bf16(a):
    hi = a.astype(BF16)
    lo = (a - hi.astype(F32)).astype(BF16)
    return hi, lo


def _dot3(a, b):
    a_hi, a_lo = _split_bf16(a)
    b_hi, b_lo = _split_bf16(b)
    return _dot(a_hi, b_hi) + (_dot(a_hi, b_lo) + _dot(a_lo, b_hi))


def _rms_rows(x, gain, eps):
    ms = jnp.mean(x * x, axis=-1, keepdims=True)
    return x * lax.rsqrt(ms + eps) * gain


def _adaln_kernel(c_ref, w_ref, b_ref, o_ref):
    c = c_ref[...]
    a = c * jax.nn.sigmoid(c)
    o_ref[...] = _dot3(a, w_ref[...]) + b_ref[...]


def _adaln(cvec, w_ada, b_ada):
    rows, d = cvec.shape
    n = w_ada.shape[1]
    return pl.pallas_call(
        _adaln_kernel,
        grid=(n // ADA_COLS,),
        in_specs=[
            pl.BlockSpec((rows, d), lambda j: (0, 0)),
            pl.BlockSpec((d, ADA_COLS), lambda j: (0, j)),
            pl.BlockSpec((1, ADA_COLS), lambda j: (0, j)),
        ],
        out_specs=pl.BlockSpec((rows, ADA_COLS), lambda j: (0, j)),
        out_shape=jax.ShapeDtypeStruct((rows, n), F32),
        compiler_params=pltpu.CompilerParams(vmem_limit_bytes=VMEM_LIMIT),
        name="adaln",
    )(cvec, w_ada, b_ada)


def _group_rms(y, gmat, gain):
    w_total = y.shape[1]
    outs = []
    for c0 in range(0, w_total, MXU_DIM):
        w = min(MXU_DIM, w_total - c0)
        yc = y[:, c0:c0 + w]
        ms = _dot((yc * yc).astype(BF16), gmat[:w, :w])
        reps = w // LANES
        g = gain if reps == 1 else jnp.concatenate([gain] * reps, axis=1)
        outs.append(yc * lax.rsqrt(ms + EPS) * g)
    return outs[0] if len(outs) == 1 else jnp.concatenate(outs, axis=1)


def _rope(y, cos, sin):
    rows = y.shape[0]
    lane = lax.broadcasted_iota(I32, (rows, LANES), 1)
    even = (lane % 2) == 0
    outs = []
    for c in range(y.shape[1] // LANES):
        ch = y[:, c * LANES:(c + 1) * LANES]
        partner = jnp.where(even, pltpu.roll(ch, LANES - 1, 1), pltpu.roll(ch, 1, 1))
        outs.append(ch * cos + partner * sin)
    return outs[0] if len(outs) == 1 else jnp.concatenate(outs, axis=1)


def _proj_kernel(x_ref, ctx_ref, mod_ref, gn_ref, w_ref, cos_ref, sin_ref, qna_ref, kna_ref, qnb_ref, knb_ref,
                 gmat_ref, qat_ref, ka_ref, vat_ref, qbt_ref, kb_ref, vbt_ref):
    b = pl.program_id(0)
    j = pl.program_id(1)
    is_ctx = j == 0
    xt = jnp.where(is_ctx, ctx_ref[0], x_ref[0])
    mrow = jnp.where(is_ctx, pl.num_programs(0), b)
    mod = mod_ref[mrow]
    h = _rms_rows(xt, gn_ref[...], EPS) * (1.0 + mod[1:2, :]) + mod[0:1, :]
    y = _dot(h.astype(BF16), w_ref[...])
    cos = jnp.where(is_ctx, 1.0, cos_ref[...])
    sin = jnp.where(is_ctx, 0.0, sin_ref[...])
    gmat = gmat_ref[...]
    scale = HEAD_DIM ** -0.5 * math.log2(math.e)

    o = 0
    qa = _rope(_group_rms(y[:, o:o + A_Q_COLS], gmat, qna_ref[...]), cos, sin) * scale
    o += A_Q_COLS
    ka = _rope(_group_rms(y[:, o:o + A_KV_COLS], gmat, kna_ref[...]), cos, sin)
    o += A_KV_COLS
    va = y[:, o:o + A_KV_COLS]
    o += A_KV_COLS
    qb = _rope(_group_rms(y[:, o:o + B_QK_COLS], gmat, qnb_ref[...]), cos, sin) * scale
    o += B_QK_COLS
    kb = _rope(_group_rms(y[:, o:o + B_QK_COLS], gmat, knb_ref[...]), cos, sin)
    o += B_QK_COLS
    vb = y[:, o:o + B_WIDTH]

    @pl.when(j > 0)
    def _():
        qat_ref[0] = qa.T.astype(BF16)
        qbt_ref[0] = qb.T.astype(BF16)

    ka_ref[0] = ka.astype(BF16)
    kb_ref[0] = kb.astype(BF16)
    ones = jnp.ones((ONES_ROWS, xt.shape[0]), BF16)
    for dst_ref, vt, width in ((vat_ref, va.T.astype(BF16), HEAD_DIM), (vbt_ref, vb.T.astype(BF16), B_V_DIM)):
        stride = width + ONES_ROWS
        for hd in range(vt.shape[0] // width):
            dst_ref[0, 0, hd * stride:hd * stride + width, :] = vt[hd * width:(hd + 1) * width, :]
            dst_ref[0, 0, hd * stride + width:(hd + 1) * stride, :] = ones


def _project(x, ctx, mod, gn, w_qkv, cos, sin, qna, kna, qnb, knb, gmat):
    bsz, s, d = x.shape
    n_lat = s // TOK_TILE
    n_key = n_lat + 1
    lk = n_key * TOK_TILE
    lat = lambda b, j: (b, jnp.maximum(j - 1, 0), 0)
    const2 = lambda b, j: (0, 0)
    tab = lambda b, j: (jnp.maximum(j - 1, 0), 0)
    return pl.pallas_call(
        _proj_kernel,
        grid=(bsz, n_key),
        in_specs=[
            pl.BlockSpec((1, TOK_TILE, d), lat),
            pl.BlockSpec((1, TOK_TILE, d), lambda b, j: (b, 0, 0)),
            pl.BlockSpec(mod.shape, lambda b, j: (0, 0, 0)),
            pl.BlockSpec((1, d), const2),
            pl.BlockSpec((d, QKV_COLS), const2),
            pl.BlockSpec((TOK_TILE, LANES), tab),
            pl.BlockSpec((TOK_TILE, LANES), tab),
            pl.BlockSpec((1, LANES), const2),
            pl.BlockSpec((1, LANES), const2),
            pl.BlockSpec((1, LANES), const2),
            pl.BlockSpec((1, LANES), const2),
            pl.BlockSpec((MXU_DIM, MXU_DIM), const2),
        ],
        out_specs=[
            pl.BlockSpec((1, A_Q_COLS, TOK_TILE), lambda b, j: (b, 0, jnp.maximum(j - 1, 0))),
            pl.BlockSpec((1, TOK_TILE, A_KV_COLS), lambda b, j: (b, j, 0)),
            pl.BlockSpec((1, 1, A_KV_HEADS * A_VROWS, TOK_TILE), lambda b, j: (b, j, 0, 0)),
            pl.BlockSpec((1, B_QK_COLS, TOK_TILE), lambda b, j: (b, 0, jnp.maximum(j - 1, 0))),
            pl.BlockSpec((1, TOK_TILE, B_QK_COLS), lambda b, j: (b, j, 0)),
            pl.BlockSpec((1, 1, B_HEADS * B_VROWS, TOK_TILE), lambda b, j: (b, j, 0, 0)),
        ],
        out_shape=[
            jax.ShapeDtypeStruct((bsz, A_Q_COLS, s), BF16),
            jax.ShapeDtypeStruct((bsz, lk, A_KV_COLS), BF16),
            jax.ShapeDtypeStruct((bsz, n_key, A_KV_HEADS * A_VROWS, TOK_TILE), BF16),
            jax.ShapeDtypeStruct((bsz, B_QK_COLS, s), BF16),
            jax.ShapeDtypeStruct((bsz, lk, B_QK_COLS), BF16),
            jax.ShapeDtypeStruct((bsz, n_key, B_HEADS * B_VROWS, TOK_TILE), BF16),
        ],
        compiler_params=pltpu.CompilerParams(vmem_limit_bytes=VMEM_LIMIT),
        name="proj",
    )(x, ctx, mod, gn, w_qkv, cos, sin, qna, kna, qnb, knb, gmat)


def _flash(k_ref, vt_ref, wq, pv, dv, s_even, s_odd):
    n = wq.shape[1]
    n_tiles = n // MXU_DIM
    n_chunks = vt_ref.shape[1]
    cols = [slice(t * MXU_DIM, (t + 1) * MXU_DIM) for t in range(n_tiles)]
    wqs = [wq[:, cs] for cs in cols]

    def scores(c, t, dst):
        start = pl.multiple_of(c * TOK_TILE, TOK_TILE)
        dst[:, cols[t]] = _dot(k_ref[0, pl.ds(start, TOK_TILE), :], wqs[t])

    def chunk(c, cur, nxt, state):
        out = []
        for t in range(n_tiles):
            if nxt is not None:
                scores(c + 1, t, nxt)
            m, acc = state[t]
            s = cur[:, cols[t]]
            m_new = jnp.maximum(m, jnp.max(s, axis=0, keepdims=True))
            alpha = jnp.exp2(m - m_new)
            p = jnp.exp2(s - m_new).astype(BF16)
            out.append((m_new, alpha * acc + pv(t, vt_ref[0, c], p)))
        return tuple(out)

    for t in range(n_tiles):
        scores(0, t, s_even)

    def pair(i, state):
        c = 2 * i
        state = chunk(c, s_even, s_odd, state)
        return chunk(c + 1, s_odd, s_even, state)

    init = tuple((jnp.full((1, MXU_DIM), -jnp.inf, F32), jnp.zeros((dv + ONES_ROWS, MXU_DIM), F32))
                 for _ in range(n_tiles))
    n_pairs = (n_chunks - 1) // 2
    state = lax.fori_loop(0, n_pairs, pair, init)
    if n_chunks % 2 == 0:
        state = chunk(n_chunks - 2, s_even, s_odd, state)
        state = chunk(n_chunks - 1, s_odd, None, state)
    else:
        state = chunk(n_chunks - 1, s_even, None, state)
    return jnp.concatenate([acc for _, acc in state], axis=1)


def _attn_a_kernel(qt_ref, k_ref, vt_ref, o_ref, s_even, s_odd):
    tq = qt_ref.shape[2]
    half = A_GROUP * tq
    zeros = jnp.zeros((HEAD_DIM, half), BF16)
    rows = []
    for g in range(A_KV_HEADS):
        heads = range(g * A_GROUP, (g + 1) * A_GROUP)
        qg = jnp.concatenate([qt_ref[0, h * HEAD_DIM:(h + 1) * HEAD_DIM, :] for h in heads], axis=1)
        rows.append(jnp.concatenate([qg if gg == g else zeros for gg in range(A_KV_HEADS)], axis=1))
    wq = jnp.concatenate(rows, axis=0)

    def pv(t, vc, p):
        g = t * MXU_DIM // half
        return _dot(vc[g * A_VROWS:(g + 1) * A_VROWS], p)

    acc = _flash(k_ref, vt_ref, wq, pv, HEAD_DIM, s_even, s_odd)
    o = acc[:HEAD_DIM] / acc[HEAD_DIM:HEAD_DIM + 1]
    outs = [o[:, h * tq:(h + 1) * tq].T for h in range(A_HEADS)]
    o_ref[0] = jnp.concatenate(outs, axis=1).astype(o_ref.dtype)


def _attn_a(qat, ka, vat):
    bsz, _, s = qat.shape
    lk = ka.shape[1]
    n_chunks = vat.shape[1]
    return pl.pallas_call(
        _attn_a_kernel,
        grid=(bsz, s // TQ_A),
        in_specs=[
            pl.BlockSpec((1, A_Q_COLS, TQ_A), lambda b, i: (b, 0, i)),
            pl.BlockSpec((1, lk, A_KV_COLS), lambda b, i: (b, 0, 0)),
            pl.BlockSpec((1, n_chunks, A_KV_HEADS * A_VROWS, TOK_TILE), lambda b, i: (b, 0, 0, 0)),
        ],
        out_specs=pl.BlockSpec((1, TQ_A, A_Q_COLS), lambda b, i: (b, i, 0)),
        out_shape=jax.ShapeDtypeStruct((bsz, s, A_Q_COLS), BF16),
        scratch_shapes=[pltpu.VMEM((TOK_TILE, A_HEADS * TQ_A), F32)] * 2,
        compiler_params=pltpu.CompilerParams(vmem_limit_bytes=VMEM_LIMIT),
        name="attn_a",
    )(qat, ka, vat)


def _attn_b_kernel(qt_ref, k_ref, vt_ref, lq1_ref, lk1_ref, lq2_ref, lk2_ref, sub_ref, o_ref, s_even, s_odd):
    tq = qt_ref.shape[2]
    zeros = jnp.zeros((HEAD_DIM, tq), BF16)
    q1 = qt_ref[0, :HEAD_DIM, :]
    q2 = qt_ref[0, HEAD_DIM:, :]
    wq = jnp.concatenate([jnp.concatenate([q1, zeros], axis=1), jnp.concatenate([zeros, q2], axis=1)], axis=0)
    acc = _flash(k_ref, vt_ref, wq, lambda t, vc, p: _dot(vc, p), B_V_DIM, s_even, s_odd)
    r = acc[:B_V_DIM] / acc[B_V_DIM:B_V_DIM + 1]
    lam = (jnp.exp(jnp.sum(lq1_ref[...] * lk1_ref[...], axis=1, keepdims=True))
           - jnp.exp(jnp.sum(lq2_ref[...] * lk2_ref[...], axis=1, keepdims=True)) + LAM_INIT)
    o = r[:, :tq] - lam * r[:, tq:]
    ms = jnp.mean(o * o, axis=0, keepdims=True)
    o = o * lax.rsqrt(ms + SUBLN_EPS) * sub_ref[...] * (1.0 - LAM_INIT)
    o_ref[0] = o.T.astype(o_ref.dtype)


def _attn_b(qbt, kb, vbt, lq1, lk1, lq2, lk2, subln):
    bsz, _, s = qbt.shape
    lk = kb.shape[1]
    n_chunks = vbt.shape[1]
    vec = pl.BlockSpec((1, HEAD_DIM), lambda b, h, i: (0, 0))
    return pl.pallas_call(
        _attn_b_kernel,
        grid=(bsz, B_HEADS, s // TQ_B),
        in_specs=[
            pl.BlockSpec((1, 2 * HEAD_DIM, TQ_B), lambda b, h, i: (b, h, i)),
            pl.BlockSpec((1, lk, 2 * HEAD_DIM), lambda b, h, i: (b, 0, h)),
            pl.BlockSpec((1, n_chunks, B_VROWS, TOK_TILE), lambda b, h, i: (b, 0, h, 0)),
            vec, vec, vec, vec,
            pl.BlockSpec((B_V_DIM, 1), lambda b, h, i: (0, 0)),
        ],
        out_specs=pl.BlockSpec((1, TQ_B, B_V_DIM), lambda b, h, i: (b, i, h)),
        out_shape=jax.ShapeDtypeStruct((bsz, s, B_WIDTH), BF16),
        scratch_shapes=[pltpu.VMEM((TOK_TILE, 2 * TQ_B), F32)] * 2,
        compiler_params=pltpu.CompilerParams(vmem_limit_bytes=VMEM_LIMIT),
        name="attn_b",
    )(qbt, kb, vbt, lq1, lk1, lq2, lk2, subln)


def _merge_kernel(tiles_per_sample, x_ref, ya_ref, yb_ref, mod_ref, gna_ref, gnm_ref, wg_ref, woa_ref, wob_ref,
                  wout_ref, rw_ref, rb_ref, ltri_ref, x1_ref, h2_ref, route_ref, cnt_ref, carry_ref):
    i = pl.program_id(0)
    b = i // tiles_per_sample
    mod = mod_ref[b]
    x = x_ref[...]
    d = x.shape[1]
    h = _rms_rows(x, gna_ref[...], EPS) * (1.0 + mod[1:2, :]) + mod[0:1, :]
    gates = _dot(h.astype(BF16), wg_ref[...])
    ga = jax.nn.sigmoid(gates[:, :d])
    gb = jax.nn.sigmoid(gates[:, d:])
    merged = ga * _dot(ya_ref[...], woa_ref[...]) + gb * _dot(yb_ref[...], wob_ref[...])
    x1 = x + mod[2:3, :] * _dot(merged.astype(BF16), wout_ref[...])
    x1_ref[...] = x1
    h2 = _rms_rows(x1, gnm_ref[...], EPS) * (1.0 + mod[4:5, :]) + mod[3:4, :]
    h2_ref[...] = h2

    logits = _dot3(h2, rw_ref[...]) + rb_ref[...]
    rows = logits.shape[0]
    eiota = lax.broadcasted_iota(I32, (rows, N_EXPERTS), 1)
    work = logits
    idxs, vals = [], []
    for _ in range(TOP_K):
        mx = jnp.max(work, axis=1, keepdims=True)
        idx = jnp.min(jnp.where(work == mx, eiota, N_EXPERTS), axis=1, keepdims=True)
        idxs.append(idx)
        vals.append(mx)
        work = jnp.where(eiota == idx, -jnp.inf, work)
    exps = [jnp.exp(v - vals[0]) for v in vals]
    denom = exps[0] + exps[1] + exps[2] + exps[3]
    weights = [e / denom for e in exps]

    @pl.when(i == 0)
    def _():
        carry_ref[...] = jnp.zeros_like(carry_ref)

    hits = [(eiota == idx) for idx in idxs]
    onehot = (hits[0] | hits[1] | hits[2] | hits[3]).astype(F32)
    before = _dot(ltri_ref[...], onehot.astype(BF16)) + carry_ref[...]
    ranks = [jnp.sum(jnp.where(hit, before, 0.0), axis=1, keepdims=True) for hit in hits]
    carry = carry_ref[...] + jnp.sum(onehot, axis=0, keepdims=True)
    carry_ref[...] = carry
    cnt_ref[...] = carry

    cols = [c.astype(F32) for c in idxs] + weights + ranks
    riota = lax.broadcasted_iota(I32, route_ref.shape, 1)
    route = jnp.zeros(route_ref.shape, F32)
    for k, col in enumerate(cols):
        route = jnp.where(riota == k, col, route)
    route_ref[...] = route


def _merge(x2, ya2, yb2, mod, gna, gnm, wg, woa, wob, wout, rw, rb, ltri, tiles_per_sample):
    t, d = x2.shape
    const2 = lambda i: (0, 0)
    row = lambda i: (i, 0)
    full = lambda a: pl.BlockSpec(a.shape, const2)
    return pl.pallas_call(
        functools.partial(_merge_kernel, tiles_per_sample),
        grid=(t // TOK_TILE,),
        in_specs=[
            pl.BlockSpec((TOK_TILE, d), row),
            pl.BlockSpec((TOK_TILE, ya2.shape[1]), row),
            pl.BlockSpec((TOK_TILE, yb2.shape[1]), row),
            pl.BlockSpec(mod.shape, lambda i: (0, 0, 0)),
            full(gna), full(gnm), full(wg), full(woa), full(wob), full(wout), full(rw), full(rb), full(ltri),
        ],
        out_specs=[
            pl.BlockSpec((TOK_TILE, d), row),
            pl.BlockSpec((TOK_TILE, d), row),
            pl.BlockSpec((TOK_TILE, 3 * TOP_K + 4), row),
            pl.BlockSpec((1, N_EXPERTS), const2),
        ],
        out_shape=[
            jax.ShapeDtypeStruct((t, d), F32),
            jax.ShapeDtypeStruct((t, d), F32),
            jax.ShapeDtypeStruct((t, 3 * TOP_K + 4), F32),
            jax.ShapeDtypeStruct((1, N_EXPERTS), F32),
        ],
        scratch_shapes=[pltpu.VMEM((1, N_EXPERTS), F32)],
        compiler_params=pltpu.CompilerParams(vmem_limit_bytes=VMEM_LIMIT),
        name="merge",
    )(x2, ya2, yb2, mod, gna, gnm, wg, woa, wob, wout, rw, rb, ltri)


def _to_row_tiles(dst_ref, val):
    n, d = val.shape
    sub = d // LANES
    for c in range(sub):
        dst_ref[pl.ds(c, n, stride=sub), :] = val[:, c * LANES:(c + 1) * LANES]


def _from_row_tiles(src_ref, first_row, n, d):
    sub = d // LANES
    return jnp.concatenate([src_ref[pl.ds(first_row * sub + c, n, stride=sub), :] for c in range(sub)], axis=1)


def _row_tile_copy(src, src_row, dst, dst_row, sub, sem):
    src_at = pl.ds(pl.multiple_of(src_row * sub, sub), sub)
    dst_at = pl.ds(pl.multiple_of(dst_row * sub, sub), sub)
    return pltpu.make_async_copy(src.at[src_at], dst.at[dst_at], sem)


def _dispatch_kernel(pos_ref, h_ref, xb_in, xb_out, tiles, sem):
    del xb_in
    i = pl.program_id(0)
    n = pl.num_programs(0)
    slot = i % 2
    n_tok, d = h_ref.shape
    sub = d // LANES

    def drain(sl):
        for _ in range(TOP_K):
            pltpu.make_async_copy(tiles.at[sl], xb_out.at[pl.ds(0, n_tok * sub)], sem.at[sl]).wait()

    @pl.when(i >= 2)
    def _():
        drain(slot)

    _to_row_tiles(tiles.at[slot], h_ref[...])
    base = i * n_tok

    def issue(t, carry):
        for k in range(TOP_K):
            _row_tile_copy(tiles.at[slot], t, xb_out, pos_ref[(base + t) * TOP_K + k], sub, sem.at[slot]).start()
        return carry

    lax.fori_loop(0, n_tok, issue, 0)

    @pl.when((i == n - 1) & (n >= 2))
    def _():
        drain(1 - slot)

    @pl.when(i == n - 1)
    def _():
        drain(slot)


def _dispatch(pos_flat, h2, xb_init):
    t, d = h2.shape
    any_spec = pl.BlockSpec(memory_space=pl.ANY)
    return pl.pallas_call(
        _dispatch_kernel,
        grid_spec=pltpu.PrefetchScalarGridSpec(
            num_scalar_prefetch=1,
            grid=(t // TOK_TILE,),
            in_specs=[pl.BlockSpec((TOK_TILE, d), lambda i, pos: (i, 0)), any_spec],
            out_specs=any_spec,
            scratch_shapes=[pltpu.VMEM((2, TOK_TILE * d // LANES, LANES), F32), pltpu.SemaphoreType.DMA((2,))],
        ),
        out_shape=jax.ShapeDtypeStruct(xb_init.shape, xb_init.dtype),
        input_output_aliases={2: 0},
        compiler_params=pltpu.CompilerParams(vmem_limit_bytes=VMEM_LIMIT),
        name="dispatch",
    )(pos_flat, h2, xb_init)


def _ffn_kernel(be_ref, nu_ref, x_ref, wg_ref, bg_ref, wu_ref, bu_ref, wd_ref, bd_ref, o_ref, wg_s, wu_s, wd_s):
    j = pl.program_id(0)
    prev = be_ref[jnp.maximum(j - 1, 0)]
    fresh = (j == 0) | (be_ref[j] != prev)

    @pl.when(fresh)
    def _():
        wg_s[...] = wg_ref[0].astype(BF16)
        wu_s[...] = wu_ref[0].astype(BF16)
        wd_s[...] = wd_ref[0].astype(BF16)

    @pl.when(j < nu_ref[0])
    def _():
        x = _from_row_tiles(x_ref, 0, FFN_ROWS, wg_s.shape[0]).astype(BF16)
        gate = jnp.minimum(_dot(x, wg_s[...]) + bg_ref[0], SWIGLU_LIMIT)
        up = jnp.clip(_dot(x, wu_s[...]) + bu_ref[0], -SWIGLU_LIMIT, SWIGLU_LIMIT)
        act = (up + 1.0) * (gate * jax.nn.sigmoid(SWIGLU_ALPHA * gate))
        _to_row_tiles(o_ref, _dot(act.astype(BF16), wd_s[...]) + bd_ref[0])

    @pl.when(j >= nu_ref[0])
    def _():
        o_ref[...] = jnp.zeros_like(o_ref)


def _ffn(block_e, n_used, xb, w_gate, b_gate, w_up, b_up, w_down, b_down):
    d, de = w_gate.shape[1:]
    blk = FFN_ROWS * d // LANES
    n_blocks = xb.shape[0] // blk
    xrow = lambda j, be, nu: (jnp.minimum(j, nu[0] - 1), 0)
    wsel = lambda j, be, nu: (be[j], 0, 0)
    return pl.pallas_call(
        _ffn_kernel,
        grid_spec=pltpu.PrefetchScalarGridSpec(
            num_scalar_prefetch=2,
            grid=(n_blocks,),
            in_specs=[
                pl.BlockSpec((blk, LANES), xrow),
                pl.BlockSpec((1, d, de), wsel),
                pl.BlockSpec((1, 1, de), wsel),
                pl.BlockSpec((1, d, de), wsel),
                pl.BlockSpec((1, 1, de), wsel),
                pl.BlockSpec((1, de, d), wsel),
                pl.BlockSpec((1, 1, d), wsel),
            ],
            out_specs=pl.BlockSpec((blk, LANES), lambda j, be, nu: (j, 0)),
            scratch_shapes=[pltpu.VMEM((d, de), BF16), pltpu.VMEM((d, de), BF16), pltpu.VMEM((de, d), BF16)],
        ),
        out_shape=jax.ShapeDtypeStruct(xb.shape, F32),
        compiler_params=pltpu.CompilerParams(vmem_limit_bytes=VMEM_LIMIT),
        name="ffn",
    )(block_e, n_used, xb, w_gate, b_gate, w_up, b_up, w_down, b_down)


def _combine_kernel(tiles_per_sample, pos_ref, x1_ref, route_ref, mod_ref, yb_hbm, o_ref, buf, sem):
    i = pl.program_id(0)
    n = pl.num_programs(0)
    rows = TOP_K * CMB_TOK
    d = x1_ref.shape[1]
    sub = d // LANES

    def issue(tile, slot):
        def body(r, carry):
            _row_tile_copy(yb_hbm, pos_ref[tile * rows + r], buf.at[slot], r, sub, sem.at[slot]).start()
            return carry
        lax.fori_loop(0, rows, body, 0)

    @pl.when(i == 0)
    def _():
        issue(0, 0)

    @pl.when(i + 1 < n)
    def _():
        issue(i + 1, (i + 1) % 2)

    slot = i % 2
    pltpu.make_async_copy(yb_hbm.at[pl.ds(0, rows * sub)], buf.at[slot], sem.at[slot]).wait()

    b = i // tiles_per_sample
    g_m = mod_ref[b][5:6, :]
    route = route_ref[...]
    acc = jnp.zeros(x1_ref.shape, F32)
    for k in range(TOP_K):
        acc = acc + route[:, TOP_K + k:TOP_K + k + 1] * _from_row_tiles(buf.at[slot], k * CMB_TOK, CMB_TOK, d)
    o_ref[...] = x1_ref[...] + g_m * acc


def _combine(pos_tiles, x1, route, mod, yb, tiles_per_sample):
    t, d = x1.shape
    row = lambda i, pos: (i, 0)
    return pl.pallas_call(
        functools.partial(_combine_kernel, tiles_per_sample),
        grid_spec=pltpu.PrefetchScalarGridSpec(
            num_scalar_prefetch=1,
            grid=(t // CMB_TOK,),
            in_specs=[
                pl.BlockSpec((CMB_TOK, d), row),
                pl.BlockSpec((CMB_TOK, route.shape[1]), row),
                pl.BlockSpec(mod.shape, lambda i, pos: (0, 0, 0)),
                pl.BlockSpec(memory_space=pl.ANY),
            ],
            out_specs=pl.BlockSpec((CMB_TOK, d), row),
            scratch_shapes=[pltpu.VMEM((2, TOP_K * CMB_TOK * d // LANES, LANES), F32),
                            pltpu.SemaphoreType.DMA((2,))],
        ),
        out_shape=jax.ShapeDtypeStruct((t, d), F32),
        compiler_params=pltpu.CompilerParams(vmem_limit_bytes=VMEM_LIMIT),
        name="combine",
    )(pos_tiles, x1, route, mod, yb)


def _rope_tables(s):
    rows = s // GRID_W
    row = jnp.repeat(jnp.arange(rows, dtype=F32), GRID_W)
    col = jnp.tile(jnp.arange(GRID_W, dtype=F32), rows)
    inv = 1.0 / (ROPE_THETA ** (jnp.arange(AXIS_PAIRS, dtype=F32) / AXIS_PAIRS))
    ang = jnp.concatenate([row[:, None] * inv, col[:, None] * inv], axis=-1)
    cos, sin = jnp.cos(ang), jnp.sin(ang)
    cos_i = jnp.repeat(cos, 2, axis=-1)
    sin_i = jnp.stack([-sin, sin], axis=-1).reshape(s, HEAD_DIM)
    reps = LANES // HEAD_DIM
    return jnp.tile(cos_i, (1, reps)), jnp.tile(sin_i, (1, reps))


def kernel(x, c, ctx, c_ctx, w_ada, b_ada, norm_attn, w_in, q_norm_a, k_norm_a, q_norm_b, k_norm_b, lambda_q1,
           lambda_k1, lambda_q2, lambda_k2, subln_b, w_oa, w_ob, w_out, norm_mlp, router_w, router_b, w_gate,
           b_gate, w_up, b_up, w_down, b_down):
    bsz, s, d = x.shape
    assert ctx.shape[1] == TOK_TILE and s % TOK_TILE == 0 and s % TQ_A == 0 and s % TQ_B == 0 and s % GRID_W == 0
    assert w_ada.shape[0] == 1, "single-layer block"
    t = bsz * s
    assert t % CMB_TOK == 0 and (t * TOP_K) % FFN_ROWS == 0

    mod_rows = 8
    assert bsz + 1 <= mod_rows
    cvec = jnp.concatenate([c, c_ctx[None, :], jnp.zeros((mod_rows - bsz - 1, d), F32)], axis=0)
    mod = _adaln(cvec, w_ada[0], b_ada[0][None, :]).reshape(mod_rows, N_MOD, d)

    cos, sin = _rope_tables(s)
    tile_gain = lambda g: jnp.tile(g, LANES // HEAD_DIM)[None, :]
    gidx = jnp.arange(MXU_DIM) // HEAD_DIM
    gmat = jnp.where(gidx[:, None] == gidx[None, :], 1.0 / HEAD_DIM, 0.0).astype(BF16)
    w_qkv = w_in[0][:, :QKV_COLS].astype(BF16)
    w_g = w_in[0][:, QKV_COLS:].astype(BF16)

    qat, ka, vat, qbt, kb, vbt = _project(
        x, ctx, mod, norm_attn, w_qkv, cos, sin, tile_gain(q_norm_a[0]), tile_gain(k_norm_a[0]),
        tile_gain(q_norm_b[0]), tile_gain(k_norm_b[0]), gmat)

    ya = _attn_a(qat, ka, vat)
    yb = _attn_b(qbt, kb, vbt, lambda_q1, lambda_k1, lambda_q2, lambda_k2, subln_b[0][:, None])

    tri = jnp.arange(TOK_TILE)
    ltri = (tri[None, :] < tri[:, None]).astype(BF16)
    x1, h2, route, counts = _merge(
        x.reshape(t, d), ya.reshape(t, A_Q_COLS), yb.reshape(t, B_WIDTH), mod, norm_attn, norm_mlp, w_g,
        w_oa[0].astype(BF16), w_ob[0].astype(BF16), w_out[0].astype(BF16), router_w[0], router_b, ltri,
        s // TOK_TILE)

    counts = counts[0].astype(I32)
    padded = (counts + FFN_ROWS - 1) // FFN_ROWS * FFN_ROWS
    pad_end = jnp.cumsum(padded)
    pad_start = pad_end - padded
    top_e = route[:, :TOP_K].astype(I32)
    rank = route[:, 2 * TOP_K:3 * TOP_K].astype(I32)
    pos = pad_start[top_e] + rank
    n_blocks = t * TOP_K // FFN_ROWS + N_EXPERTS
    block_start = jnp.arange(n_blocks, dtype=I32) * FFN_ROWS
    block_e = jnp.minimum(jnp.sum((pad_end[None, :] <= block_start[:, None]).astype(I32), axis=1), N_EXPERTS - 1)
    n_used = (pad_end[-1:] // FFN_ROWS).astype(I32)

    xb = _dispatch(pos.reshape(-1), h2, jnp.zeros((n_blocks * FFN_ROWS * d // LANES, LANES), F32))
    de = w_gate.shape[-1]
    yb_rows = _ffn(block_e, n_used, xb, w_gate[0], b_gate[0].reshape(N_EXPERTS, 1, de), w_up[0],
                   b_up[0].reshape(N_EXPERTS, 1, de), w_down[0], b_down[0].reshape(N_EXPERTS, 1, d))
    pos_tiles = pos.reshape(t // CMB_TOK, CMB_TOK, TOP_K).transpose(0, 2, 1).reshape(-1)
    out = _combine(pos_tiles, x1, route, mod, yb_rows, s // CMB_TOK)
    return out.reshape(bsz, s, d)
```

```python
import functools
import math

import jax
import jax.numpy as jnp
from jax import lax
from jax.experimental import pallas as pl
from jax.experimental.pallas import tpu as pltpu

F32 = jnp.float32
BF16 = jnp.bfloat16
I32 = jnp.int32

GRID_W = 64
HEAD_DIM = 64
A_HEADS = 8
A_KV_HEADS = 2
A_GROUP = A_HEADS // A_KV_HEADS
B_HEADS = 4
B_V_DIM = 2 * HEAD_DIM
N_EXPERTS = 32
TOP_K = 4
N_MOD = 6
ROPE_THETA = 10000.0
AXIS_PAIRS = HEAD_DIM // 4
SWIGLU_LIMIT = 7.0
SWIGLU_ALPHA = 1.702
EPS = 1e-6
SUBLN_EPS = 1e-5
LAM_INIT = 0.8 - 0.6 * math.exp(-0.3 * 0)

A_Q_COLS = A_HEADS * HEAD_DIM
A_KV_COLS = A_KV_HEADS * HEAD_DIM
B_QK_COLS = B_HEADS * 2 * HEAD_DIM
B_WIDTH = B_HEADS * B_V_DIM
QKV_COLS = A_Q_COLS + 2 * A_KV_COLS + 2 * B_QK_COLS + B_WIDTH

LANES = 128
MXU_DIM = 256
TOK_TILE = 256
TQ_A = 128
TQ_B = 512
CHUNKS_PER_TRIP = 16
ONES_ROWS = 16
A_VROWS = HEAD_DIM + ONES_ROWS
B_VROWS = B_V_DIM + ONES_ROWS
ADA_COLS = 1536
FFN_ROWS = 256
CMB_TOK = 128
DMA_QUEUES = 2
VMEM_LIMIT = 56 * 1024 * 1024


def _dot(a, b):
    return jnp.dot(a, b, preferred_element_type=F32)


def _split_bf16(a):
    hi = a.astype(BF16)
    lo = (a - hi.astype(F32)).astype(BF16)
    return hi, lo


def _dot3(a, b):
    a_hi, a_lo = _split_bf16(a)
    b_hi, b_lo = _split_bf16(b)
    return _dot(a_hi, b_hi) + (_dot(a_hi, b_lo) + _dot(a_lo, b_hi))


def _rms_rows(x, gain, eps):
    ms = jnp.mean(x * x, axis=-1, keepdims=True)
    return x * lax.rsqrt(ms + eps) * gain


def _adaln_kernel(c_ref, w_ref, b_ref, o_ref):
    c = c_ref[...]
    a = c * jax.nn.sigmoid(c)
    o_ref[...] = _dot3(a, w_ref[...]) + b_ref[...]


def _adaln(cvec, w_ada, b_ada):
    rows, d = cvec.shape
    n = w_ada.shape[1]
    return pl.pallas_call(
        _adaln_kernel,
        grid=(n // ADA_COLS,),
        in_specs=[
            pl.BlockSpec((rows, d), lambda j: (0, 0)),
            pl.BlockSpec((d, ADA_COLS), lambda j: (0, j)),
            pl.BlockSpec((1, ADA_COLS), lambda j: (0, j)),
        ],
        out_specs=pl.BlockSpec((rows, ADA_COLS), lambda j: (0, j)),
        out_shape=jax.ShapeDtypeStruct((rows, n), F32),
        compiler_params=pltpu.CompilerParams(vmem_limit_bytes=VMEM_LIMIT),
        name="adaln",
    )(cvec, w_ada, b_ada)


def _group_rms(y, gmat, gain):
    w_total = y.shape[1]
    outs = []
    for c0 in range(0, w_total, MXU_DIM):
        w = min(MXU_DIM, w_total - c0)
        yc = y[:, c0:c0 + w]
        ms = _dot((yc * yc).astype(BF16), gmat[:w, :w])
        reps = w // LANES
        g = gain if reps == 1 else jnp.concatenate([gain] * reps, axis=1)
        outs.append(yc * lax.rsqrt(ms + EPS) * g)
    return outs[0] if len(outs) == 1 else jnp.concatenate(outs, axis=1)


def _rope(y, cos, sin):
    rows = y.shape[0]
    lane = lax.broadcasted_iota(I32, (rows, LANES), 1)
    even = (lane % 2) == 0
    outs = []
    for c in range(y.shape[1] // LANES):
        ch = y[:, c * LANES:(c + 1) * LANES]
        partner = jnp.where(even, pltpu.roll(ch, LANES - 1, 1), pltpu.roll(ch, 1, 1))
        outs.append(ch * cos + partner * sin)
    return outs[0] if len(outs) == 1 else jnp.concatenate(outs, axis=1)


def _proj_kernel(x_ref, ctx_ref, mod_ref, gn_ref, w_ref, cos_ref, sin_ref, qna_ref, kna_ref, qnb_ref, knb_ref,
                 gmat_ref, qat_ref, ka_ref, vat_ref, qbt_ref, kb_ref, vbt_ref):
    b = pl.program_id(0)
    j = pl.program_id(1)
    is_ctx = j == 0
    xt = jnp.where(is_ctx, ctx_ref[0], x_ref[0])
    mrow = jnp.where(is_ctx, pl.num_programs(0), b)
    mod = mod_ref[mrow]
    h = _rms_rows(xt, gn_ref[...], EPS) * (1.0 + mod[1:2, :]) + mod[0:1, :]
    y = _dot(h.astype(BF16), w_ref[...])
    cos = jnp.where(is_ctx, 1.0, cos_ref[...])
    sin = jnp.where(is_ctx, 0.0, sin_ref[...])
    gmat = gmat_ref[...]
    scale = HEAD_DIM ** -0.5 * math.log2(math.e)

    o = 0
    qa = _rope(_group_rms(y[:, o:o + A_Q_COLS], gmat, qna_ref[...]), cos, sin) * scale
    o += A_Q_COLS
    ka = _rope(_group_rms(y[:, o:o + A_KV_COLS], gmat, kna_ref[...]), cos, sin)
    o += A_KV_COLS
    va = y[:, o:o + A_KV_COLS]
    o += A_KV_COLS
    qb = _rope(_group_rms(y[:, o:o + B_QK_COLS], gmat, qnb_ref[...]), cos, sin) * scale
    o += B_QK_COLS
    kb = _rope(_group_rms(y[:, o:o + B_QK_COLS], gmat, knb_ref[...]), cos, sin)
    o += B_QK_COLS
    vb = y[:, o:o + B_WIDTH]

    @pl.when(j > 0)
    def _():
        qat_ref[0] = qa.T.astype(BF16)
        qbt_ref[0] = qb.T.astype(BF16)

    ka_ref[0] = ka.astype(BF16)
    kb_ref[0] = kb.astype(BF16)
    ones = jnp.ones((ONES_ROWS, xt.shape[0]), BF16)
    for dst_ref, vt, width in ((vat_ref, va.T.astype(BF16), HEAD_DIM), (vbt_ref, vb.T.astype(BF16), B_V_DIM)):
        stride = width + ONES_ROWS
        for hd in range(vt.shape[0] // width):
            dst_ref[0, 0, hd * stride:hd * stride + width, :] = vt[hd * width:(hd + 1) * width, :]
            dst_ref[0, 0, hd * stride + width:(hd + 1) * stride, :] = ones


def _project(x, ctx, mod, gn, w_qkv, cos, sin, qna, kna, qnb, knb, gmat):
    bsz, s, d = x.shape
    n_lat = s // TOK_TILE
    n_key = n_lat + 1
    lk = n_key * TOK_TILE
    lat = lambda b, j: (b, jnp.maximum(j - 1, 0), 0)
    const2 = lambda b, j: (0, 0)
    tab = lambda b, j: (jnp.maximum(j - 1, 0), 0)
    return pl.pallas_call(
        _proj_kernel,
        grid=(bsz, n_key),
        in_specs=[
            pl.BlockSpec((1, TOK_TILE, d), lat),
            pl.BlockSpec((1, TOK_TILE, d), lambda b, j: (b, 0, 0)),
            pl.BlockSpec(mod.shape, lambda b, j: (0, 0, 0)),
            pl.BlockSpec((1, d), const2),
            pl.BlockSpec((d, QKV_COLS), const2),
            pl.BlockSpec((TOK_TILE, LANES), tab),
            pl.BlockSpec((TOK_TILE, LANES), tab),
            pl.BlockSpec((1, LANES), const2),
            pl.BlockSpec((1, LANES), const2),
            pl.BlockSpec((1, LANES), const2),
            pl.BlockSpec((1, LANES), const2),
            pl.BlockSpec((MXU_DIM, MXU_DIM), const2),
        ],
        out_specs=[
            pl.BlockSpec((1, A_Q_COLS, TOK_TILE), lambda b, j: (b, 0, jnp.maximum(j - 1, 0))),
            pl.BlockSpec((1, TOK_TILE, A_KV_COLS), lambda b, j: (b, j, 0)),
            pl.BlockSpec((1, 1, A_KV_HEADS * A_VROWS, TOK_TILE), lambda b, j: (b, j, 0, 0)),
            pl.BlockSpec((1, B_QK_COLS, TOK_TILE), lambda b, j: (b, 0, jnp.maximum(j - 1, 0))),
            pl.BlockSpec((1, TOK_TILE, B_QK_COLS), lambda b, j: (b, j, 0)),
            pl.BlockSpec((1, 1, B_HEADS * B_VROWS, TOK_TILE), lambda b, j: (b, j, 0, 0)),
        ],
        out_shape=[
            jax.ShapeDtypeStruct((bsz, A_Q_COLS, s), BF16),
            jax.ShapeDtypeStruct((bsz, lk, A_KV_COLS), BF16),
            jax.ShapeDtypeStruct((bsz, n_key, A_KV_HEADS * A_VROWS, TOK_TILE), BF16),
            jax.ShapeDtypeStruct((bsz, B_QK_COLS, s), BF16),
            jax.ShapeDtypeStruct((bsz, lk, B_QK_COLS), BF16),
            jax.ShapeDtypeStruct((bsz, n_key, B_HEADS * B_VROWS, TOK_TILE), BF16),
        ],
        compiler_params=pltpu.CompilerParams(vmem_limit_bytes=VMEM_LIMIT),
        name="proj",
    )(x, ctx, mod, gn, w_qkv, cos, sin, qna, kna, qnb, knb, gmat)


def _flash(k_ref, vt_ref, wq, pv, dv, s_even, s_odd):
    n = wq.shape[1]
    n_tiles = n // MXU_DIM
    n_chunks = vt_ref.shape[1]
    cols = [slice(t * MXU_DIM, (t + 1) * MXU_DIM) for t in range(n_tiles)]
    wqs = [wq[:, cs] for cs in cols]

    def scores(c, t, dst):
        start = pl.multiple_of(c * TOK_TILE, TOK_TILE)
        dst[:, cols[t]] = _dot(k_ref[0, pl.ds(start, TOK_TILE), :], wqs[t])

    def chunk(c, cur, nxt, state):
        out = []
        for t in range(n_tiles):
            if nxt is not None:
                scores(c + 1, t, nxt)
            m, acc = state[t]
            s = cur[:, cols[t]]
            m_new = jnp.maximum(m, jnp.max(s, axis=0, keepdims=True))
            alpha = jnp.exp2(m - m_new)
            p = jnp.exp2(s - m_new).astype(BF16)
            out.append((m_new, alpha * acc + pv(t, vt_ref[0, c], p)))
        return tuple(out)

    for t in range(n_tiles):
        scores(0, t, s_even)

    bufs = (s_even, s_odd)

    def trip(i, state):
        for u in range(CHUNKS_PER_TRIP):
            state = chunk(i * CHUNKS_PER_TRIP + u, bufs[u % 2], bufs[(u + 1) % 2], state)
        return state

    init = tuple((jnp.full((1, MXU_DIM), -jnp.inf, F32), jnp.zeros((dv + ONES_ROWS, MXU_DIM), F32))
                 for _ in range(n_tiles))
    n_trips = (n_chunks - 1) // CHUNKS_PER_TRIP
    state = lax.fori_loop(0, n_trips, trip, init)
    for c in range(n_trips * CHUNKS_PER_TRIP, n_chunks):
        state = chunk(c, bufs[c % 2], bufs[(c + 1) % 2] if c + 1 < n_chunks else None, state)
    return jnp.concatenate([acc for _, acc in state], axis=1)


def _attn_a_kernel(qt_ref, k_ref, vt_ref, o_ref, s_even, s_odd):
    tq = qt_ref.shape[2]
    half = A_GROUP * tq
    zeros = jnp.zeros((HEAD_DIM, half), BF16)
    rows = []
    for g in range(A_KV_HEADS):
        heads = range(g * A_GROUP, (g + 1) * A_GROUP)
        qg = jnp.concatenate([qt_ref[0, h * HEAD_DIM:(h + 1) * HEAD_DIM, :] for h in heads], axis=1)
        rows.append(jnp.concatenate([qg if gg == g else zeros for gg in range(A_KV_HEADS)], axis=1))
    wq = jnp.concatenate(rows, axis=0)

    def pv(t, vc, p):
        g = t * MXU_DIM // half
        return _dot(vc[g * A_VROWS:(g + 1) * A_VROWS], p)

    acc = _flash(k_ref, vt_ref, wq, pv, HEAD_DIM, s_even, s_odd)
    o = acc[:HEAD_DIM] / acc[HEAD_DIM:HEAD_DIM + 1]
    outs = [o[:, h * tq:(h + 1) * tq].T for h in range(A_HEADS)]
    o_ref[0] = jnp.concatenate(outs, axis=1).astype(o_ref.dtype)


def _attn_a(qat, ka, vat):
    bsz, _, s = qat.shape
    lk = ka.shape[1]
    n_chunks = vat.shape[1]
    return pl.pallas_call(
        _attn_a_kernel,
        grid=(bsz, s // TQ_A),
        in_specs=[
            pl.BlockSpec((1, A_Q_COLS, TQ_A), lambda b, i: (b, 0, i)),
            pl.BlockSpec((1, lk, A_KV_COLS), lambda b, i: (b, 0, 0)),
            pl.BlockSpec((1, n_chunks, A_KV_HEADS * A_VROWS, TOK_TILE), lambda b, i: (b, 0, 0, 0)),
        ],
        out_specs=pl.BlockSpec((1, TQ_A, A_Q_COLS), lambda b, i: (b, i, 0)),
        out_shape=jax.ShapeDtypeStruct((bsz, s, A_Q_COLS), BF16),
        scratch_shapes=[pltpu.VMEM((TOK_TILE, A_HEADS * TQ_A), F32)] * 2,
        compiler_params=pltpu.CompilerParams(vmem_limit_bytes=VMEM_LIMIT),
        name="attn_a",
    )(qat, ka, vat)


def _attn_b_kernel(qt_ref, k_ref, vt_ref, lq1_ref, lk1_ref, lq2_ref, lk2_ref, sub_ref, o_ref, s_even, s_odd):
    tq = qt_ref.shape[2]
    zeros = jnp.zeros((HEAD_DIM, tq), BF16)
    q1 = qt_ref[0, :HEAD_DIM, :]
    q2 = qt_ref[0, HEAD_DIM:, :]
    wq = jnp.concatenate([jnp.concatenate([q1, zeros], axis=1), jnp.concatenate([zeros, q2], axis=1)], axis=0)
    acc = _flash(k_ref, vt_ref, wq, lambda t, vc, p: _dot(vc, p), B_V_DIM, s_even, s_odd)
    r = acc[:B_V_DIM] / acc[B_V_DIM:B_V_DIM + 1]
    lam = (jnp.exp(jnp.sum(lq1_ref[...] * lk1_ref[...], axis=1, keepdims=True))
           - jnp.exp(jnp.sum(lq2_ref[...] * lk2_ref[...], axis=1, keepdims=True)) + LAM_INIT)
    o = r[:, :tq] - lam * r[:, tq:]
    ms = jnp.mean(o * o, axis=0, keepdims=True)
    o = o * lax.rsqrt(ms + SUBLN_EPS) * sub_ref[...] * (1.0 - LAM_INIT)
    o_ref[0] = o.T.astype(o_ref.dtype)


def _attn_b(qbt, kb, vbt, lq1, lk1, lq2, lk2, subln):
    bsz, _, s = qbt.shape
    lk = kb.shape[1]
    n_chunks = vbt.shape[1]
    vec = pl.BlockSpec((1, HEAD_DIM), lambda b, h, i: (0, 0))
    return pl.pallas_call(
        _attn_b_kernel,
        grid=(bsz, B_HEADS, s // TQ_B),
        in_specs=[
            pl.BlockSpec((1, 2 * HEAD_DIM, TQ_B), lambda b, h, i: (b, h, i)),
            pl.BlockSpec((1, lk, 2 * HEAD_DIM), lambda b, h, i: (b, 0, h)),
            pl.BlockSpec((1, n_chunks, B_VROWS, TOK_TILE), lambda b, h, i: (b, 0, h, 0)),
            vec, vec, vec, vec,
            pl.BlockSpec((B_V_DIM, 1), lambda b, h, i: (0, 0)),
        ],
        out_specs=pl.BlockSpec((1, TQ_B, B_V_DIM), lambda b, h, i: (b, i, h)),
        out_shape=jax.ShapeDtypeStruct((bsz, s, B_WIDTH), BF16),
        scratch_shapes=[pltpu.VMEM((TOK_TILE, 2 * TQ_B), F32)] * 2,
        compiler_params=pltpu.CompilerParams(vmem_limit_bytes=VMEM_LIMIT),
        name="attn_b",
    )(qbt, kb, vbt, lq1, lk1, lq2, lk2, subln)


def _merge_kernel(tiles_per_sample, x_ref, ya_ref, yb_ref, mod_ref, gna_ref, gnm_ref, wg_ref, woa_ref, wob_ref,
                  wout_ref, rw_ref, rb_ref, ltri_ref, x1_ref, h2_ref, route_ref, cnt_ref, carry_ref):
    i = pl.program_id(0)
    b = i // tiles_per_sample
    mod = mod_ref[b]
    x = x_ref[...]
    d = x.shape[1]
    h = _rms_rows(x, gna_ref[...], EPS) * (1.0 + mod[1:2, :]) + mod[0:1, :]
    gates = _dot(h.astype(BF16), wg_ref[...])
    ga = jax.nn.sigmoid(gates[:, :d])
    gb = jax.nn.sigmoid(gates[:, d:])
    merged = ga * _dot(ya_ref[...], woa_ref[...]) + gb * _dot(yb_ref[...], wob_ref[...])
    x1 = x + mod[2:3, :] * _dot(merged.astype(BF16), wout_ref[...])
    x1_ref[...] = x1
    h2 = _rms_rows(x1, gnm_ref[...], EPS) * (1.0 + mod[4:5, :]) + mod[3:4, :]
    h2_ref[...] = h2

    logits = _dot3(h2, rw_ref[...]) + rb_ref[...]
    rows = logits.shape[0]
    eiota = lax.broadcasted_iota(I32, (rows, N_EXPERTS), 1)
    work = logits
    idxs, vals = [], []
    for _ in range(TOP_K):
        mx = jnp.max(work, axis=1, keepdims=True)
        idx = jnp.min(jnp.where(work == mx, eiota, N_EXPERTS), axis=1, keepdims=True)
        idxs.append(idx)
        vals.append(mx)
        work = jnp.where(eiota == idx, -jnp.inf, work)
    exps = [jnp.exp(v - vals[0]) for v in vals]
    denom = exps[0] + exps[1] + exps[2] + exps[3]
    weights = [e / denom for e in exps]

    @pl.when(i == 0)
    def _():
        carry_ref[...] = jnp.zeros_like(carry_ref)

    hits = [(eiota == idx) for idx in idxs]
    onehot = (hits[0] | hits[1] | hits[2] | hits[3]).astype(F32)
    before = _dot(ltri_ref[...], onehot.astype(BF16)) + carry_ref[...]
    ranks = [jnp.sum(jnp.where(hit, before, 0.0), axis=1, keepdims=True) for hit in hits]
    carry = carry_ref[...] + jnp.sum(onehot, axis=0, keepdims=True)
    carry_ref[...] = carry
    cnt_ref[...] = carry

    cols = [c.astype(F32) for c in idxs] + weights + ranks
    riota = lax.broadcasted_iota(I32, route_ref.shape, 1)
    route = jnp.zeros(route_ref.shape, F32)
    for k, col in enumerate(cols):
        route = jnp.where(riota == k, col, route)
    route_ref[...] = route


def _merge(x2, ya2, yb2, mod, gna, gnm, wg, woa, wob, wout, rw, rb, ltri, tiles_per_sample):
    t, d = x2.shape
    const2 = lambda i: (0, 0)
    row = lambda i: (i, 0)
    full = lambda a: pl.BlockSpec(a.shape, const2)
    return pl.pallas_call(
        functools.partial(_merge_kernel, tiles_per_sample),
        grid=(t // TOK_TILE,),
        in_specs=[
            pl.BlockSpec((TOK_TILE, d), row),
            pl.BlockSpec((TOK_TILE, ya2.shape[1]), row),
            pl.BlockSpec((TOK_TILE, yb2.shape[1]), row),
            pl.BlockSpec(mod.shape, lambda i: (0, 0, 0)),
            full(gna), full(gnm), full(wg), full(woa), full(wob), full(wout), full(rw), full(rb), full(ltri),
        ],
        out_specs=[
            pl.BlockSpec((TOK_TILE, d), row),
            pl.BlockSpec((TOK_TILE, d), row),
            pl.BlockSpec((TOK_TILE, 3 * TOP_K + 4), row),
            pl.BlockSpec((1, N_EXPERTS), const2),
        ],
        out_shape=[
            jax.ShapeDtypeStruct((t, d), F32),
            jax.ShapeDtypeStruct((t, d), F32),
            jax.ShapeDtypeStruct((t, 3 * TOP_K + 4), F32),
            jax.ShapeDtypeStruct((1, N_EXPERTS), F32),
        ],
        scratch_shapes=[pltpu.VMEM((1, N_EXPERTS), F32)],
        compiler_params=pltpu.CompilerParams(vmem_limit_bytes=VMEM_LIMIT),
        name="merge",
    )(x2, ya2, yb2, mod, gna, gnm, wg, woa, wob, wout, rw, rb, ltri)


def _to_row_tiles(dst_ref, val):
    n, d = val.shape
    sub = d // LANES
    for c in range(sub):
        dst_ref[pl.ds(c, n, stride=sub), :] = val[:, c * LANES:(c + 1) * LANES]


def _from_row_tiles(src_ref, first_row, n, d):
    sub = d // LANES
    return jnp.concatenate([src_ref[pl.ds(first_row * sub + c, n, stride=sub), :] for c in range(sub)], axis=1)


def _row_tile_copy(src, src_row, dst, dst_row, sub, sem):
    src_at = pl.ds(pl.multiple_of(src_row * sub, sub), sub)
    dst_at = pl.ds(pl.multiple_of(dst_row * sub, sub), sub)
    return pltpu.make_async_copy(src.at[src_at], dst.at[dst_at], sem)


def _dispatch_kernel(pos_ref, padfrom_ref, padcnt_ref, nu_ref, h_ref, xb_out, tiles, zeros, sem, zsem):
    i = pl.program_id(0)
    n = pl.num_programs(0)
    slot = i % 2
    n_tok, d = h_ref.shape
    sub = d // LANES
    blk = zeros.shape[0]
    n_blocks = xb_out.shape[0] // blk

    def zero_fill(wait):
        def go(copy):
            copy.wait() if wait else copy.start()

        def per_expert(e, carry):
            cnt = padcnt_ref[e]
            for bit in reversed(range((FFN_ROWS - 1).bit_length())):
                rows = 1 << bit

                @pl.when((cnt & rows) != 0)
                def _():
                    done = (cnt >> (bit + 1)) << (bit + 1)
                    at = pl.multiple_of((padfrom_ref[e] + done) * sub, sub)
                    go(pltpu.make_async_copy(zeros.at[pl.ds(0, rows * sub)], xb_out.at[pl.ds(at, rows * sub)], zsem))
            return carry

        lax.fori_loop(0, N_EXPERTS, per_expert, 0)

        def per_block(j, carry):
            go(pltpu.make_async_copy(zeros, xb_out.at[pl.ds(pl.multiple_of(j * blk, blk), blk)], zsem))
            return carry

        lax.fori_loop(nu_ref[0], n_blocks, per_block, 0)

    @pl.when(i == 0)
    def _():
        zeros[...] = jnp.zeros_like(zeros)
        zero_fill(wait=False)
        zero_fill(wait=True)

    def drain(sl):
        for _ in range(TOP_K):
            pltpu.make_async_copy(tiles.at[sl], xb_out.at[pl.ds(0, n_tok * sub)], sem.at[sl]).wait()

    @pl.when(i >= 2)
    def _():
        drain(slot)

    _to_row_tiles(tiles.at[slot], h_ref[...])
    base = i * n_tok

    def issue(t, carry):
        for k in range(TOP_K):
            _row_tile_copy(tiles.at[slot], t, xb_out, pos_ref[(base + t) * TOP_K + k], sub, sem.at[slot]).start()
        return carry

    lax.fori_loop(0, n_tok, issue, 0)

    @pl.when((i == n - 1) & (n >= 2))
    def _():
        drain(1 - slot)

    @pl.when(i == n - 1)
    def _():
        drain(slot)


def _dispatch(pos_flat, pad_from, pad_cnt, n_used, h2, n_blocks):
    t, d = h2.shape
    sub = d // LANES
    return pl.pallas_call(
        _dispatch_kernel,
        grid_spec=pltpu.PrefetchScalarGridSpec(
            num_scalar_prefetch=4,
            grid=(t // TOK_TILE,),
            in_specs=[pl.BlockSpec((TOK_TILE, d), lambda i, *_: (i, 0))],
            out_specs=pl.BlockSpec(memory_space=pl.ANY),
            scratch_shapes=[pltpu.VMEM((2, TOK_TILE * sub, LANES), F32), pltpu.VMEM((FFN_ROWS * sub, LANES), F32),
                            pltpu.SemaphoreType.DMA((2,)), pltpu.SemaphoreType.DMA(())],
        ),
        out_shape=jax.ShapeDtypeStruct((n_blocks * FFN_ROWS * sub, LANES), F32),
        compiler_params=pltpu.CompilerParams(vmem_limit_bytes=VMEM_LIMIT),
        name="dispatch",
    )(pos_flat, pad_from, pad_cnt, n_used, h2)


def _ffn_kernel(be_ref, nu_ref, x_ref, wg_ref, bg_ref, wu_ref, bu_ref, wd_ref, bd_ref, o_ref, wg_s, wu_s, wd_s):
    j = pl.program_id(0)
    prev = be_ref[jnp.maximum(j - 1, 0)]
    fresh = (j == 0) | (be_ref[j] != prev)

    @pl.when(fresh)
    def _():
        wg_s[...] = wg_ref[0].astype(BF16)
        wu_s[...] = wu_ref[0].astype(BF16)
        wd_s[...] = wd_ref[0].astype(BF16)

    @pl.when(j < nu_ref[0])
    def _():
        x = _from_row_tiles(x_ref, 0, FFN_ROWS, wg_s.shape[0]).astype(BF16)
        gate = jnp.minimum(_dot(x, wg_s[...]) + bg_ref[0], SWIGLU_LIMIT)
        up = jnp.clip(_dot(x, wu_s[...]) + bu_ref[0], -SWIGLU_LIMIT, SWIGLU_LIMIT)
        act = (up + 1.0) * (gate * jax.nn.sigmoid(SWIGLU_ALPHA * gate))
        _to_row_tiles(o_ref, _dot(act.astype(BF16), wd_s[...]) + bd_ref[0])

    @pl.when(j >= nu_ref[0])
    def _():
        o_ref[...] = jnp.zeros_like(o_ref)


def _ffn(block_e, n_used, xb, w_gate, b_gate, w_up, b_up, w_down, b_down):
    d, de = w_gate.shape[1:]
    blk = FFN_ROWS * d // LANES
    n_blocks = xb.shape[0] // blk
    xrow = lambda j, be, nu: (jnp.minimum(j, nu[0] - 1), 0)
    wsel = lambda j, be, nu: (be[j], 0, 0)
    return pl.pallas_call(
        _ffn_kernel,
        grid_spec=pltpu.PrefetchScalarGridSpec(
            num_scalar_prefetch=2,
            grid=(n_blocks,),
            in_specs=[
                pl.BlockSpec((blk, LANES), xrow),
                pl.BlockSpec((1, d, de), wsel),
                pl.BlockSpec((1, 1, de), wsel),
                pl.BlockSpec((1, d, de), wsel),
                pl.BlockSpec((1, 1, de), wsel),
                pl.BlockSpec((1, de, d), wsel),
                pl.BlockSpec((1, 1, d), wsel),
            ],
            out_specs=pl.BlockSpec((blk, LANES), lambda j, be, nu: (j, 0)),
            scratch_shapes=[pltpu.VMEM((d, de), BF16), pltpu.VMEM((d, de), BF16), pltpu.VMEM((de, d), BF16)],
        ),
        out_shape=jax.ShapeDtypeStruct(xb.shape, F32),
        compiler_params=pltpu.CompilerParams(vmem_limit_bytes=VMEM_LIMIT),
        name="ffn",
    )(block_e, n_used, xb, w_gate, b_gate, w_up, b_up, w_down, b_down)


def _combine_kernel(tiles_per_sample, pos_ref, x1_ref, route_ref, mod_ref, yb_hbm, o_ref, buf, sem):
    i = pl.program_id(0)
    n = pl.num_programs(0)
    rows = TOP_K * CMB_TOK
    d = x1_ref.shape[1]
    sub = d // LANES

    def issue(tile, slot):
        def body(r2, carry):
            for u in range(DMA_QUEUES):
                r = r2 * DMA_QUEUES + u
                _row_tile_copy(yb_hbm, pos_ref[tile * rows + r], buf.at[slot], r, sub, sem.at[slot]).start(priority=u)
            return carry
        lax.fori_loop(0, rows // DMA_QUEUES, body, 0)

    @pl.when(i == 0)
    def _():
        issue(0, 0)

    @pl.when(i + 1 < n)
    def _():
        issue(i + 1, (i + 1) % 2)

    slot = i % 2
    pltpu.make_async_copy(yb_hbm.at[pl.ds(0, rows * sub)], buf.at[slot], sem.at[slot]).wait()

    b = i // tiles_per_sample
    g_m = mod_ref[b][5:6, :]
    route = route_ref[...]
    acc = jnp.zeros(x1_ref.shape, F32)
    for k in range(TOP_K):
        acc = acc + route[:, TOP_K + k:TOP_K + k + 1] * _from_row_tiles(buf.at[slot], k * CMB_TOK, CMB_TOK, d)
    o_ref[...] = x1_ref[...] + g_m * acc


def _combine(pos_tiles, x1, route, mod, yb, tiles_per_sample):
    t, d = x1.shape
    row = lambda i, pos: (i, 0)
    return pl.pallas_call(
        functools.partial(_combine_kernel, tiles_per_sample),
        grid_spec=pltpu.PrefetchScalarGridSpec(
            num_scalar_prefetch=1,
            grid=(t // CMB_TOK,),
            in_specs=[
                pl.BlockSpec((CMB_TOK, d), row),
                pl.BlockSpec((CMB_TOK, route.shape[1]), row),
                pl.BlockSpec(mod.shape, lambda i, pos: (0, 0, 0)),
                pl.BlockSpec(memory_space=pl.ANY),
            ],
            out_specs=pl.BlockSpec((CMB_TOK, d), row),
            scratch_shapes=[pltpu.VMEM((2, TOP_K * CMB_TOK * d // LANES, LANES), F32),
                            pltpu.SemaphoreType.DMA((2,))],
        ),
        out_shape=jax.ShapeDtypeStruct((t, d), F32),
        compiler_params=pltpu.CompilerParams(vmem_limit_bytes=VMEM_LIMIT),
        name="combine",
    )(pos_tiles, x1, route, mod, yb)


def _rope_tables(s):
    rows = s // GRID_W
    row = jnp.repeat(jnp.arange(rows, dtype=F32), GRID_W)
    col = jnp.tile(jnp.arange(GRID_W, dtype=F32), rows)
    inv = 1.0 / (ROPE_THETA ** (jnp.arange(AXIS_PAIRS, dtype=F32) / AXIS_PAIRS))
    ang = jnp.concatenate([row[:, None] * inv, col[:, None] * inv], axis=-1)
    cos, sin = jnp.cos(ang), jnp.sin(ang)
    cos_i = jnp.repeat(cos, 2, axis=-1)
    sin_i = jnp.stack([-sin, sin], axis=-1).reshape(s, HEAD_DIM)
    reps = LANES // HEAD_DIM
    return jnp.tile(cos_i, (1, reps)), jnp.tile(sin_i, (1, reps))


def kernel(x, c, ctx, c_ctx, w_ada, b_ada, norm_attn, w_in, q_norm_a, k_norm_a, q_norm_b, k_norm_b, lambda_q1,
           lambda_k1, lambda_q2, lambda_k2, subln_b, w_oa, w_ob, w_out, norm_mlp, router_w, router_b, w_gate,
           b_gate, w_up, b_up, w_down, b_down):
    bsz, s, d = x.shape
    assert ctx.shape[1] == TOK_TILE and s % TOK_TILE == 0 and s % TQ_A == 0 and s % TQ_B == 0 and s % GRID_W == 0
    assert w_ada.shape[0] == 1, "single-layer block"
    t = bsz * s
    assert t % CMB_TOK == 0 and (t * TOP_K) % FFN_ROWS == 0

    mod_rows = 8
    assert bsz + 1 <= mod_rows
    cvec = jnp.concatenate([c, c_ctx[None, :], jnp.zeros((mod_rows - bsz - 1, d), F32)], axis=0)
    mod = _adaln(cvec, w_ada[0], b_ada[0][None, :]).reshape(mod_rows, N_MOD, d)

    cos, sin = _rope_tables(s)
    tile_gain = lambda g: jnp.tile(g, LANES // HEAD_DIM)[None, :]
    gidx = jnp.arange(MXU_DIM) // HEAD_DIM
    gmat = jnp.where(gidx[:, None] == gidx[None, :], 1.0 / HEAD_DIM, 0.0).astype(BF16)
    w_qkv = w_in[0][:, :QKV_COLS].astype(BF16)
    w_g = w_in[0][:, QKV_COLS:].astype(BF16)

    qat, ka, vat, qbt, kb, vbt = _project(
        x, ctx, mod, norm_attn, w_qkv, cos, sin, tile_gain(q_norm_a[0]), tile_gain(k_norm_a[0]),
        tile_gain(q_norm_b[0]), tile_gain(k_norm_b[0]), gmat)

    ya = _attn_a(qat, ka, vat)
    yb = _attn_b(qbt, kb, vbt, lambda_q1, lambda_k1, lambda_q2, lambda_k2, subln_b[0][:, None])

    tri = jnp.arange(TOK_TILE)
    ltri = (tri[None, :] < tri[:, None]).astype(BF16)
    x1, h2, route, counts = _merge(
        x.reshape(t, d), ya.reshape(t, A_Q_COLS), yb.reshape(t, B_WIDTH), mod, norm_attn, norm_mlp, w_g,
        w_oa[0].astype(BF16), w_ob[0].astype(BF16), w_out[0].astype(BF16), router_w[0], router_b, ltri,
        s // TOK_TILE)

    counts = counts[0].astype(I32)
    padded = (counts + FFN_ROWS - 1) // FFN_ROWS * FFN_ROWS
    pad_end = jnp.cumsum(padded)
    pad_start = pad_end - padded
    top_e = route[:, :TOP_K].astype(I32)
    rank = route[:, 2 * TOP_K:3 * TOP_K].astype(I32)
    pos = pad_start[top_e] + rank
    n_blocks = t * TOP_K // FFN_ROWS + N_EXPERTS
    block_start = jnp.arange(n_blocks, dtype=I32) * FFN_ROWS
    block_e = jnp.minimum(jnp.sum((pad_end[None, :] <= block_start[:, None]).astype(I32), axis=1), N_EXPERTS - 1)
    n_used = (pad_end[-1:] // FFN_ROWS).astype(I32)

    xb = _dispatch(pos.reshape(-1), pad_start + counts, padded - counts, n_used, h2, n_blocks)
    de = w_gate.shape[-1]
    yb_rows = _ffn(block_e, n_used, xb, w_gate[0], b_gate[0].reshape(N_EXPERTS, 1, de), w_up[0],
                   b_up[0].reshape(N_EXPERTS, 1, de), w_down[0], b_down[0].reshape(N_EXPERTS, 1, d))
    pos_tiles = pos.reshape(t // CMB_TOK, CMB_TOK, TOP_K).transpose(0, 2, 1).reshape(-1)
    out = _combine(pos_tiles, x1, route, mod, yb_rows, s // CMB_TOK)
    return out.reshape(bsz, s, d)
```

```python
import functools
import math

import jax
import jax.numpy as jnp
from jax import lax
from jax.experimental import pallas as pl
from jax.experimental.pallas import tpu as pltpu

F32 = jnp.float32
BF16 = jnp.bfloat16
I32 = jnp.int32

GRID_W = 64
HEAD_DIM = 64
A_HEADS = 8
A_KV_HEADS = 2
A_GROUP = A_HEADS // A_KV_HEADS
B_HEADS = 4
B_V_DIM = 2 * HEAD_DIM
N_EXPERTS = 32
TOP_K = 4
N_MOD = 6
ROPE_THETA = 10000.0
AXIS_PAIRS = HEAD_DIM // 4
SWIGLU_LIMIT = 7.0
SWIGLU_ALPHA = 1.702
EPS = 1e-6
SUBLN_EPS = 1e-5
LAM_INIT = 0.8 - 0.6 * math.exp(-0.3 * 0)

A_Q_COLS = A_HEADS * HEAD_DIM
A_KV_COLS = A_KV_HEADS * HEAD_DIM
B_QK_COLS = B_HEADS * 2 * HEAD_DIM
B_WIDTH = B_HEADS * B_V_DIM
QKV_COLS = A_Q_COLS + 2 * A_KV_COLS + 2 * B_QK_COLS + B_WIDTH

LANES = 128
MXU_DIM = 256
TOK_TILE = 256
TQ_A = 128
TQ_B = 512
CHUNKS_PER_TRIP = 16
ONES_ROWS = 16
A_VROWS = HEAD_DIM + ONES_ROWS
B_VROWS = B_V_DIM + ONES_ROWS
ADA_COLS = 1536
FFN_ROWS = 256
CMB_TOK = 128
ROUTE_ROWS = 16
DMA_QUEUES = 2
VMEM_LIMIT = 56 * 1024 * 1024


def _dot(a, b):
    return jnp.dot(a, b, preferred_element_type=F32)


def _split_bf16(a):
    hi = a.astype(BF16)
    lo = (a - hi.astype(F32)).astype(BF16)
    return hi, lo


def _dot3(a, b):
    a_hi, a_lo = _split_bf16(a)
    b_hi, b_lo = _split_bf16(b)
    return _dot(a_hi, b_hi) + (_dot(a_hi, b_lo) + _dot(a_lo, b_hi))


def _dot3_nt(a, b):
    dims = (((1,), (1,)), ((), ()))
    dot_nt = lambda u, v: lax.dot_general(u, v, dims, preferred_element_type=F32)
    a_hi, a_lo = _split_bf16(a)
    b_hi, b_lo = _split_bf16(b)
    return dot_nt(a_hi, b_hi) + (dot_nt(a_hi, b_lo) + dot_nt(a_lo, b_hi))


def _rms_rows(x, gain, eps):
    ms = jnp.mean(x * x, axis=-1, keepdims=True)
    return x * lax.rsqrt(ms + eps) * gain


def _adaln_kernel(c_ref, w_ref, b_ref, o_ref):
    c = c_ref[...]
    a = c * jax.nn.sigmoid(c)
    o_ref[...] = _dot3(a, w_ref[...]) + b_ref[...]


def _adaln(cvec, w_ada, b_ada):
    rows, d = cvec.shape
    n = w_ada.shape[1]
    return pl.pallas_call(
        _adaln_kernel,
        grid=(n // ADA_COLS,),
        in_specs=[
            pl.BlockSpec((rows, d), lambda j: (0, 0)),
            pl.BlockSpec((d, ADA_COLS), lambda j: (0, j)),
            pl.BlockSpec((1, ADA_COLS), lambda j: (0, j)),
        ],
        out_specs=pl.BlockSpec((rows, ADA_COLS), lambda j: (0, j)),
        out_shape=jax.ShapeDtypeStruct((rows, n), F32),
        compiler_params=pltpu.CompilerParams(vmem_limit_bytes=VMEM_LIMIT),
        name="adaln",
    )(cvec, w_ada, b_ada)


def _group_rms(y, gmat, gain):
    w_total = y.shape[1]
    outs = []
    for c0 in range(0, w_total, MXU_DIM):
        w = min(MXU_DIM, w_total - c0)
        yc = y[:, c0:c0 + w]
        ms = _dot((yc * yc).astype(BF16), gmat[:w, :w])
        reps = w // LANES
        g = gain if reps == 1 else jnp.concatenate([gain] * reps, axis=1)
        outs.append(yc * lax.rsqrt(ms + EPS) * g)
    return outs[0] if len(outs) == 1 else jnp.concatenate(outs, axis=1)


def _rope(y, cos, sin):
    rows = y.shape[0]
    lane = lax.broadcasted_iota(I32, (rows, LANES), 1)
    even = (lane % 2) == 0
    outs = []
    for c in range(y.shape[1] // LANES):
        ch = y[:, c * LANES:(c + 1) * LANES]
        partner = jnp.where(even, pltpu.roll(ch, LANES - 1, 1), pltpu.roll(ch, 1, 1))
        outs.append(ch * cos + partner * sin)
    return outs[0] if len(outs) == 1 else jnp.concatenate(outs, axis=1)


def _proj_kernel(x_ref, ctx_ref, mod_ref, gn_ref, w_ref, cos_ref, sin_ref, qna_ref, kna_ref, qnb_ref, knb_ref,
                 gmat_ref, qat_ref, ka_ref, vat_ref, qbt_ref, kb_ref, vbt_ref):
    b = pl.program_id(0)
    j = pl.program_id(1)
    is_ctx = j == 0
    xt = jnp.where(is_ctx, ctx_ref[0], x_ref[0])
    mrow = jnp.where(is_ctx, pl.num_programs(0), b)
    mod = mod_ref[mrow]
    h = _rms_rows(xt, gn_ref[...], EPS) * (1.0 + mod[1:2, :]) + mod[0:1, :]
    y = _dot(h.astype(BF16), w_ref[...])
    cos = jnp.where(is_ctx, 1.0, cos_ref[...])
    sin = jnp.where(is_ctx, 0.0, sin_ref[...])
    gmat = gmat_ref[...]
    scale = HEAD_DIM ** -0.5 * math.log2(math.e)

    o = 0
    qa = _rope(_group_rms(y[:, o:o + A_Q_COLS], gmat, qna_ref[...]), cos, sin) * scale
    o += A_Q_COLS
    ka = _rope(_group_rms(y[:, o:o + A_KV_COLS], gmat, kna_ref[...]), cos, sin)
    o += A_KV_COLS
    va = y[:, o:o + A_KV_COLS]
    o += A_KV_COLS
    qb = _rope(_group_rms(y[:, o:o + B_QK_COLS], gmat, qnb_ref[...]), cos, sin) * scale
    o += B_QK_COLS
    kb = _rope(_group_rms(y[:, o:o + B_QK_COLS], gmat, knb_ref[...]), cos, sin)
    o += B_QK_COLS
    vb = y[:, o:o + B_WIDTH]

    @pl.when(j > 0)
    def _():
        qat_ref[0] = qa.T.astype(BF16)
        qbt_ref[0] = qb.T.astype(BF16)

    ka_ref[0] = ka.astype(BF16)
    kb_ref[0] = kb.astype(BF16)
    ones = jnp.ones((ONES_ROWS, xt.shape[0]), BF16)
    for dst_ref, vt, width in ((vat_ref, va.T.astype(BF16), HEAD_DIM), (vbt_ref, vb.T.astype(BF16), B_V_DIM)):
        stride = width + ONES_ROWS
        for hd in range(vt.shape[0] // width):
            dst_ref[0, 0, hd * stride:hd * stride + width, :] = vt[hd * width:(hd + 1) * width, :]
            dst_ref[0, 0, hd * stride + width:(hd + 1) * stride, :] = ones


def _project(x, ctx, mod, gn, w_qkv, cos, sin, qna, kna, qnb, knb, gmat):
    bsz, s, d = x.shape
    n_lat = s // TOK_TILE
    n_key = n_lat + 1
    lk = n_key * TOK_TILE
    lat = lambda b, j: (b, jnp.maximum(j - 1, 0), 0)
    const2 = lambda b, j: (0, 0)
    tab = lambda b, j: (jnp.maximum(j - 1, 0), 0)
    return pl.pallas_call(
        _proj_kernel,
        grid=(bsz, n_key),
        in_specs=[
            pl.BlockSpec((1, TOK_TILE, d), lat),
            pl.BlockSpec((1, TOK_TILE, d), lambda b, j: (b, 0, 0)),
            pl.BlockSpec(mod.shape, lambda b, j: (0, 0, 0)),
            pl.BlockSpec((1, d), const2),
            pl.BlockSpec((d, QKV_COLS), const2),
            pl.BlockSpec((TOK_TILE, LANES), tab),
            pl.BlockSpec((TOK_TILE, LANES), tab),
            pl.BlockSpec((1, LANES), const2),
            pl.BlockSpec((1, LANES), const2),
            pl.BlockSpec((1, LANES), const2),
            pl.BlockSpec((1, LANES), const2),
            pl.BlockSpec((MXU_DIM, MXU_DIM), const2),
        ],
        out_specs=[
            pl.BlockSpec((1, A_Q_COLS, TOK_TILE), lambda b, j: (b, 0, jnp.maximum(j - 1, 0))),
            pl.BlockSpec((1, TOK_TILE, A_KV_COLS), lambda b, j: (b, j, 0)),
            pl.BlockSpec((1, 1, A_KV_HEADS * A_VROWS, TOK_TILE), lambda b, j: (b, j, 0, 0)),
            pl.BlockSpec((1, B_QK_COLS, TOK_TILE), lambda b, j: (b, 0, jnp.maximum(j - 1, 0))),
            pl.BlockSpec((1, TOK_TILE, B_QK_COLS), lambda b, j: (b, j, 0)),
            pl.BlockSpec((1, 1, B_HEADS * B_VROWS, TOK_TILE), lambda b, j: (b, j, 0, 0)),
        ],
        out_shape=[
            jax.ShapeDtypeStruct((bsz, A_Q_COLS, s), BF16),
            jax.ShapeDtypeStruct((bsz, lk, A_KV_COLS), BF16),
            jax.ShapeDtypeStruct((bsz, n_key, A_KV_HEADS * A_VROWS, TOK_TILE), BF16),
            jax.ShapeDtypeStruct((bsz, B_QK_COLS, s), BF16),
            jax.ShapeDtypeStruct((bsz, lk, B_QK_COLS), BF16),
            jax.ShapeDtypeStruct((bsz, n_key, B_HEADS * B_VROWS, TOK_TILE), BF16),
        ],
        compiler_params=pltpu.CompilerParams(vmem_limit_bytes=VMEM_LIMIT),
        name="proj",
    )(x, ctx, mod, gn, w_qkv, cos, sin, qna, kna, qnb, knb, gmat)


def _flash(k_ref, vt_ref, wq, pv, dv, s_even, s_odd):
    n = wq.shape[1]
    n_tiles = n // MXU_DIM
    n_chunks = vt_ref.shape[1]
    cols = [slice(t * MXU_DIM, (t + 1) * MXU_DIM) for t in range(n_tiles)]
    wqs = [wq[:, cs] for cs in cols]

    def scores(c, t, dst):
        start = pl.multiple_of(c * TOK_TILE, TOK_TILE)
        dst[:, cols[t]] = _dot(k_ref[0, pl.ds(start, TOK_TILE), :], wqs[t])

    def chunk(c, cur, nxt, state):
        out = []
        for t in range(n_tiles):
            if nxt is not None:
                scores(c + 1, t, nxt)
            m, acc = state[t]
            s = cur[:, cols[t]]
            m_new = jnp.maximum(m, jnp.max(s, axis=0, keepdims=True))
            alpha = jnp.exp2(m - m_new)
            p = jnp.exp2(s - m_new).astype(BF16)
            out.append((m_new, alpha * acc + pv(t, vt_ref[0, c], p)))
        return tuple(out)

    for t in range(n_tiles):
        scores(0, t, s_even)

    bufs = (s_even, s_odd)

    def trip(i, state):
        for u in range(CHUNKS_PER_TRIP):
            state = chunk(i * CHUNKS_PER_TRIP + u, bufs[u % 2], bufs[(u + 1) % 2], state)
        return state

    init = tuple((jnp.full((1, MXU_DIM), -jnp.inf, F32), jnp.zeros((dv + ONES_ROWS, MXU_DIM), F32))
                 for _ in range(n_tiles))
    n_trips = (n_chunks - 1) // CHUNKS_PER_TRIP
    state = lax.fori_loop(0, n_trips, trip, init)
    for c in range(n_trips * CHUNKS_PER_TRIP, n_chunks):
        state = chunk(c, bufs[c % 2], bufs[(c + 1) % 2] if c + 1 < n_chunks else None, state)
    return jnp.concatenate([acc for _, acc in state], axis=1)


def _attn_a_kernel(qt_ref, k_ref, vt_ref, o_ref, s_even, s_odd):
    tq = qt_ref.shape[2]
    half = A_GROUP * tq
    zeros = jnp.zeros((HEAD_DIM, half), BF16)
    rows = []
    for g in range(A_KV_HEADS):
        heads = range(g * A_GROUP, (g + 1) * A_GROUP)
        qg = jnp.concatenate([qt_ref[0, h * HEAD_DIM:(h + 1) * HEAD_DIM, :] for h in heads], axis=1)
        rows.append(jnp.concatenate([qg if gg == g else zeros for gg in range(A_KV_HEADS)], axis=1))
    wq = jnp.concatenate(rows, axis=0)

    def pv(t, vc, p):
        g = t * MXU_DIM // half
        return _dot(vc[g * A_VROWS:(g + 1) * A_VROWS], p)

    acc = _flash(k_ref, vt_ref, wq, pv, HEAD_DIM, s_even, s_odd)
    o = acc[:HEAD_DIM] / acc[HEAD_DIM:HEAD_DIM + 1]
    outs = [o[:, h * tq:(h + 1) * tq].T for h in range(A_HEADS)]
    o_ref[0] = jnp.concatenate(outs, axis=1).astype(o_ref.dtype)


def _attn_a(qat, ka, vat):
    bsz, _, s = qat.shape
    lk = ka.shape[1]
    n_chunks = vat.shape[1]
    return pl.pallas_call(
        _attn_a_kernel,
        grid=(bsz, s // TQ_A),
        in_specs=[
            pl.BlockSpec((1, A_Q_COLS, TQ_A), lambda b, i: (b, 0, i)),
            pl.BlockSpec((1, lk, A_KV_COLS), lambda b, i: (b, 0, 0)),
            pl.BlockSpec((1, n_chunks, A_KV_HEADS * A_VROWS, TOK_TILE), lambda b, i: (b, 0, 0, 0)),
        ],
        out_specs=pl.BlockSpec((1, TQ_A, A_Q_COLS), lambda b, i: (b, i, 0)),
        out_shape=jax.ShapeDtypeStruct((bsz, s, A_Q_COLS), BF16),
        scratch_shapes=[pltpu.VMEM((TOK_TILE, A_HEADS * TQ_A), F32)] * 2,
        compiler_params=pltpu.CompilerParams(vmem_limit_bytes=VMEM_LIMIT),
        name="attn_a",
    )(qat, ka, vat)


def _attn_b_kernel(qt_ref, k_ref, vt_ref, lq1_ref, lk1_ref, lq2_ref, lk2_ref, sub_ref, o_ref, s_even, s_odd):
    tq = qt_ref.shape[2]
    zeros = jnp.zeros((HEAD_DIM, tq), BF16)
    q1 = qt_ref[0, :HEAD_DIM, :]
    q2 = qt_ref[0, HEAD_DIM:, :]
    wq = jnp.concatenate([jnp.concatenate([q1, zeros], axis=1), jnp.concatenate([zeros, q2], axis=1)], axis=0)
    acc = _flash(k_ref, vt_ref, wq, lambda t, vc, p: _dot(vc, p), B_V_DIM, s_even, s_odd)
    r = acc[:B_V_DIM] / acc[B_V_DIM:B_V_DIM + 1]
    lam = (jnp.exp(jnp.sum(lq1_ref[...] * lk1_ref[...], axis=1, keepdims=True))
           - jnp.exp(jnp.sum(lq2_ref[...] * lk2_ref[...], axis=1, keepdims=True)) + LAM_INIT)
    o = r[:, :tq] - lam * r[:, tq:]
    ms = jnp.mean(o * o, axis=0, keepdims=True)
    o = o * lax.rsqrt(ms + SUBLN_EPS) * sub_ref[...] * (1.0 - LAM_INIT)
    o_ref[0] = o.T.astype(o_ref.dtype)


def _attn_b(qbt, kb, vbt, lq1, lk1, lq2, lk2, subln):
    bsz, _, s = qbt.shape
    lk = kb.shape[1]
    n_chunks = vbt.shape[1]
    vec = pl.BlockSpec((1, HEAD_DIM), lambda b, h, i: (0, 0))
    return pl.pallas_call(
        _attn_b_kernel,
        grid=(bsz, B_HEADS, s // TQ_B),
        in_specs=[
            pl.BlockSpec((1, 2 * HEAD_DIM, TQ_B), lambda b, h, i: (b, h, i)),
            pl.BlockSpec((1, lk, 2 * HEAD_DIM), lambda b, h, i: (b, 0, h)),
            pl.BlockSpec((1, n_chunks, B_VROWS, TOK_TILE), lambda b, h, i: (b, 0, h, 0)),
            vec, vec, vec, vec,
            pl.BlockSpec((B_V_DIM, 1), lambda b, h, i: (0, 0)),
        ],
        out_specs=pl.BlockSpec((1, TQ_B, B_V_DIM), lambda b, h, i: (b, i, h)),
        out_shape=jax.ShapeDtypeStruct((bsz, s, B_WIDTH), BF16),
        scratch_shapes=[pltpu.VMEM((TOK_TILE, 2 * TQ_B), F32)] * 2,
        compiler_params=pltpu.CompilerParams(vmem_limit_bytes=VMEM_LIMIT),
        name="attn_b",
    )(qbt, kb, vbt, lq1, lk1, lq2, lk2, subln)


def _merge_kernel(tiles_per_sample, x_ref, ya_ref, yb_ref, mod_ref, gna_ref, gnm_ref, wg_ref, woa_ref, wob_ref,
                  wout_ref, rw_ref, rb_ref, utri_ref, x1_ref, h2_ref, route_ref, cnt_ref, carry_ref):
    i = pl.program_id(0)
    b = i // tiles_per_sample
    mod = mod_ref[b]
    x = x_ref[...]
    d = x.shape[1]
    h = _rms_rows(x, gna_ref[...], EPS) * (1.0 + mod[1:2, :]) + mod[0:1, :]
    gates = _dot(h.astype(BF16), wg_ref[...])
    ga = jax.nn.sigmoid(gates[:, :d])
    gb = jax.nn.sigmoid(gates[:, d:])
    merged = ga * _dot(ya_ref[...], woa_ref[...]) + gb * _dot(yb_ref[...], wob_ref[...])
    x1 = x + mod[2:3, :] * _dot(merged.astype(BF16), wout_ref[...])
    x1_ref[...] = x1
    h2 = _rms_rows(x1, gnm_ref[...], EPS) * (1.0 + mod[4:5, :]) + mod[3:4, :]
    h2_ref[...] = h2

    logits = _dot3_nt(rw_ref[...], h2) + rb_ref[...]
    rows = logits.shape[1]
    eiota = lax.broadcasted_iota(I32, (N_EXPERTS, rows), 0)
    work = logits
    idxs, vals = [], []
    for _ in range(TOP_K):
        mx = jnp.max(work, axis=0, keepdims=True)
        idx = jnp.min(jnp.where(work == mx, eiota, N_EXPERTS), axis=0, keepdims=True)
        idxs.append(idx)
        vals.append(mx)
        work = jnp.where(eiota == idx, -jnp.inf, work)
    exps = [jnp.exp(v - vals[0]) for v in vals]
    denom = exps[0] + exps[1] + exps[2] + exps[3]
    weights = [e / denom for e in exps]

    @pl.when(i == 0)
    def _():
        carry_ref[...] = jnp.zeros_like(carry_ref)

    hits = [(eiota == idx) for idx in idxs]
    onehot = (hits[0] | hits[1] | hits[2] | hits[3]).astype(F32)
    before = _dot(onehot.astype(BF16), utri_ref[...]) + carry_ref[...]
    ranks = [jnp.sum(jnp.where(hit, before, 0.0), axis=0, keepdims=True) for hit in hits]
    carry = carry_ref[...] + jnp.sum(onehot, axis=1, keepdims=True)
    carry_ref[...] = carry
    cnt_ref[...] = carry

    lines = [v.astype(F32) for v in idxs] + weights + ranks
    riota = lax.broadcasted_iota(I32, route_ref.shape, 0)
    route = jnp.zeros(route_ref.shape, F32)
    for k, line in enumerate(lines):
        route = jnp.where(riota == k, line, route)
    route_ref[...] = route


def _merge(x2, ya2, yb2, mod, gna, gnm, wg, woa, wob, wout, rw, rb, utri, tiles_per_sample):
    t, d = x2.shape
    const2 = lambda i: (0, 0)
    row = lambda i: (i, 0)
    full = lambda a: pl.BlockSpec(a.shape, const2)
    return pl.pallas_call(
        functools.partial(_merge_kernel, tiles_per_sample),
        grid=(t // TOK_TILE,),
        in_specs=[
            pl.BlockSpec((TOK_TILE, d), row),
            pl.BlockSpec((TOK_TILE, ya2.shape[1]), row),
            pl.BlockSpec((TOK_TILE, yb2.shape[1]), row),
            pl.BlockSpec(mod.shape, lambda i: (0, 0, 0)),
            full(gna), full(gnm), full(wg), full(woa), full(wob), full(wout), full(rw), full(rb), full(utri),
        ],
        out_specs=[
            pl.BlockSpec((TOK_TILE, d), row),
            pl.BlockSpec((TOK_TILE, d), row),
            pl.BlockSpec((ROUTE_ROWS, TOK_TILE), lambda i: (0, i)),
            pl.BlockSpec((N_EXPERTS, 1), const2),
        ],
        out_shape=[
            jax.ShapeDtypeStruct((t, d), F32),
            jax.ShapeDtypeStruct((t, d), F32),
            jax.ShapeDtypeStruct((ROUTE_ROWS, t), F32),
            jax.ShapeDtypeStruct((N_EXPERTS, 1), F32),
        ],
        scratch_shapes=[pltpu.VMEM((N_EXPERTS, 1), F32)],
        compiler_params=pltpu.CompilerParams(vmem_limit_bytes=VMEM_LIMIT),
        name="merge",
    )(x2, ya2, yb2, mod, gna, gnm, wg, woa, wob, wout, rw, rb, utri)


def _to_row_tiles(dst_ref, val):
    n, d = val.shape
    sub = d // LANES
    for c in range(sub):
        dst_ref[pl.ds(c, n, stride=sub), :] = val[:, c * LANES:(c + 1) * LANES]


def _from_row_tiles(src_ref, first_row, n, d):
    sub = d // LANES
    return jnp.concatenate([src_ref[pl.ds(first_row * sub + c, n, stride=sub), :] for c in range(sub)], axis=1)


def _row_tile_copy(src, src_row, dst, dst_row, sub, sem):
    src_at = pl.ds(pl.multiple_of(src_row * sub, sub), sub)
    dst_at = pl.ds(pl.multiple_of(dst_row * sub, sub), sub)
    return pltpu.make_async_copy(src.at[src_at], dst.at[dst_at], sem)


def _dispatch_kernel(pos_ref, padfrom_ref, padcnt_ref, nu_ref, h_ref, xb_out, tiles, zeros, sem, zsem):
    i = pl.program_id(0)
    n = pl.num_programs(0)
    slot = i % 2
    n_tok, d = h_ref.shape
    sub = d // LANES
    blk = zeros.shape[0]
    n_blocks = xb_out.shape[0] // blk

    def zero_fill(wait):
        def go(copy):
            copy.wait() if wait else copy.start()

        def per_expert(e, carry):
            cnt = padcnt_ref[e]
            for bit in reversed(range((FFN_ROWS - 1).bit_length())):
                rows = 1 << bit

                @pl.when((cnt & rows) != 0)
                def _():
                    done = (cnt >> (bit + 1)) << (bit + 1)
                    at = pl.multiple_of((padfrom_ref[e] + done) * sub, sub)
                    go(pltpu.make_async_copy(zeros.at[pl.ds(0, rows * sub)], xb_out.at[pl.ds(at, rows * sub)], zsem))
            return carry

        lax.fori_loop(0, N_EXPERTS, per_expert, 0)

        def per_block(j, carry):
            go(pltpu.make_async_copy(zeros, xb_out.at[pl.ds(pl.multiple_of(j * blk, blk), blk)], zsem))
            return carry

        lax.fori_loop(nu_ref[0], n_blocks, per_block, 0)

    @pl.when(i == 0)
    def _():
        zeros[...] = jnp.zeros_like(zeros)
        zero_fill(wait=False)
        zero_fill(wait=True)

    def drain(sl):
        for _ in range(TOP_K):
            pltpu.make_async_copy(tiles.at[sl], xb_out.at[pl.ds(0, n_tok * sub)], sem.at[sl]).wait()

    @pl.when(i >= 2)
    def _():
        drain(slot)

    _to_row_tiles(tiles.at[slot], h_ref[...])
    base = i * n_tok

    def issue(t, carry):
        for k in range(TOP_K):
            _row_tile_copy(tiles.at[slot], t, xb_out, pos_ref[(base + t) * TOP_K + k], sub,
                           sem.at[slot]).start(priority=k % DMA_QUEUES)
        return carry

    lax.fori_loop(0, n_tok, issue, 0)

    @pl.when((i == n - 1) & (n >= 2))
    def _():
        drain(1 - slot)

    @pl.when(i == n - 1)
    def _():
        drain(slot)


def _dispatch(pos_flat, pad_from, pad_cnt, n_used, h2, n_blocks):
    t, d = h2.shape
    sub = d // LANES
    return pl.pallas_call(
        _dispatch_kernel,
        grid_spec=pltpu.PrefetchScalarGridSpec(
            num_scalar_prefetch=4,
            grid=(t // TOK_TILE,),
            in_specs=[pl.BlockSpec((TOK_TILE, d), lambda i, *_: (i, 0))],
            out_specs=pl.BlockSpec(memory_space=pl.ANY),
            scratch_shapes=[pltpu.VMEM((2, TOK_TILE * sub, LANES), F32), pltpu.VMEM((FFN_ROWS * sub, LANES), F32),
                            pltpu.SemaphoreType.DMA((2,)), pltpu.SemaphoreType.DMA(())],
        ),
        out_shape=jax.ShapeDtypeStruct((n_blocks * FFN_ROWS * sub, LANES), F32),
        compiler_params=pltpu.CompilerParams(vmem_limit_bytes=VMEM_LIMIT),
        name="dispatch",
    )(pos_flat, pad_from, pad_cnt, n_used, h2)


def _ffn_kernel(be_ref, nu_ref, nxt_ref, x_ref, wg_hbm, bg_ref, wu_hbm, bu_ref, wd_hbm, bd_ref, o_ref,
                stage_g, stage_u, stage_d, wg_s, wu_s, wd_s, wsem):
    j = pl.program_id(0)
    prev = be_ref[jnp.maximum(j - 1, 0)]
    fresh = ((j == 0) | (be_ref[j] != prev)) & (j < nu_ref[0])
    streams = ((wg_hbm, stage_g, wg_s), (wu_hbm, stage_u, wu_s), (wd_hbm, stage_d, wd_s))

    def weight_copies(expert):
        return [pltpu.make_async_copy(hbm.at[expert], stage, wsem.at[n]) for n, (hbm, stage, _) in enumerate(streams)]

    @pl.when(j == 0)
    def _():
        for copy in weight_copies(be_ref[0]):
            copy.start()

    @pl.when(fresh)
    def _():
        for copy, (_, stage, dst) in zip(weight_copies(be_ref[j]), streams):
            copy.wait()
            dst[...] = stage[...].astype(BF16)

        @pl.when(nxt_ref[j] >= 0)
        def _():
            for copy in weight_copies(nxt_ref[j]):
                copy.start()

    @pl.when(j < nu_ref[0])
    def _():
        x = _from_row_tiles(x_ref, 0, FFN_ROWS, wg_s.shape[0]).astype(BF16)
        gate = jnp.minimum(_dot(x, wg_s[...]) + bg_ref[0], SWIGLU_LIMIT)
        up = jnp.clip(_dot(x, wu_s[...]) + bu_ref[0], -SWIGLU_LIMIT, SWIGLU_LIMIT)
        act = (up + 1.0) * (gate * jax.nn.sigmoid(SWIGLU_ALPHA * gate))
        _to_row_tiles(o_ref, _dot(act.astype(BF16), wd_s[...]) + bd_ref[0])

    @pl.when(j >= nu_ref[0])
    def _():
        o_ref[...] = jnp.zeros_like(o_ref)


def _ffn(block_e, n_used, next_e, xb, w_gate, b_gate, w_up, b_up, w_down, b_down):
    d, de = w_gate.shape[1:]
    blk = FFN_ROWS * d // LANES
    n_blocks = xb.shape[0] // blk
    xrow = lambda j, be, nu, nxt: (jnp.minimum(j, nu[0] - 1), 0)
    bsel = lambda j, be, nu, nxt: (be[j], 0, 0)
    any_spec = pl.BlockSpec(memory_space=pl.ANY)
    return pl.pallas_call(
        _ffn_kernel,
        grid_spec=pltpu.PrefetchScalarGridSpec(
            num_scalar_prefetch=3,
            grid=(n_blocks,),
            in_specs=[
                pl.BlockSpec((blk, LANES), xrow),
                any_spec,
                pl.BlockSpec((1, 1, de), bsel),
                any_spec,
                pl.BlockSpec((1, 1, de), bsel),
                any_spec,
                pl.BlockSpec((1, 1, d), bsel),
            ],
            out_specs=pl.BlockSpec((blk, LANES), lambda j, be, nu, nxt: (j, 0)),
            scratch_shapes=[pltpu.VMEM((d, de), F32), pltpu.VMEM((d, de), F32), pltpu.VMEM((de, d), F32),
                            pltpu.VMEM((d, de), BF16), pltpu.VMEM((d, de), BF16), pltpu.VMEM((de, d), BF16),
                            pltpu.SemaphoreType.DMA((3,))],
        ),
        out_shape=jax.ShapeDtypeStruct(xb.shape, F32),
        compiler_params=pltpu.CompilerParams(vmem_limit_bytes=VMEM_LIMIT),
        name="ffn",
    )(block_e, n_used, next_e, xb, w_gate, b_gate, w_up, b_up, w_down, b_down)


def _combine_kernel(tiles_per_sample, pos_ref, x1_ref, route_ref, mod_ref, yb_hbm, o_ref, buf, sem):
    i = pl.program_id(0)
    n = pl.num_programs(0)
    rows = TOP_K * CMB_TOK
    d = x1_ref.shape[1]
    sub = d // LANES

    def issue(tile, slot):
        def body(r2, carry):
            for u in range(DMA_QUEUES):
                r = r2 * DMA_QUEUES + u
                _row_tile_copy(yb_hbm, pos_ref[tile * rows + r], buf.at[slot], r, sub, sem.at[slot]).start(priority=u)
            return carry
        lax.fori_loop(0, rows // DMA_QUEUES, body, 0)

    @pl.when(i == 0)
    def _():
        issue(0, 0)

    @pl.when(i + 1 < n)
    def _():
        issue(i + 1, (i + 1) % 2)

    slot = i % 2
    pltpu.make_async_copy(yb_hbm.at[pl.ds(0, rows * sub)], buf.at[slot], sem.at[slot]).wait()

    b = i // tiles_per_sample
    g_m = mod_ref[b][5:6, :]
    route = route_ref[...]
    acc = jnp.zeros(x1_ref.shape, F32)
    for k in range(TOP_K):
        acc = acc + route[:, TOP_K + k:TOP_K + k + 1] * _from_row_tiles(buf.at[slot], k * CMB_TOK, CMB_TOK, d)
    o_ref[...] = x1_ref[...] + g_m * acc


def _combine(pos_tiles, x1, route, mod, yb, tiles_per_sample):
    t, d = x1.shape
    row = lambda i, pos: (i, 0)
    return pl.pallas_call(
        functools.partial(_combine_kernel, tiles_per_sample),
        grid_spec=pltpu.PrefetchScalarGridSpec(
            num_scalar_prefetch=1,
            grid=(t // CMB_TOK,),
            in_specs=[
                pl.BlockSpec((CMB_TOK, d), row),
                pl.BlockSpec((CMB_TOK, route.shape[1]), row),
                pl.BlockSpec(mod.shape, lambda i, pos: (0, 0, 0)),
                pl.BlockSpec(memory_space=pl.ANY),
            ],
            out_specs=pl.BlockSpec((CMB_TOK, d), row),
            scratch_shapes=[pltpu.VMEM((2, TOP_K * CMB_TOK * d // LANES, LANES), F32),
                            pltpu.SemaphoreType.DMA((2,))],
        ),
        out_shape=jax.ShapeDtypeStruct((t, d), F32),
        compiler_params=pltpu.CompilerParams(vmem_limit_bytes=VMEM_LIMIT),
        name="combine",
    )(pos_tiles, x1, route, mod, yb)


def _rope_tables(s):
    rows = s // GRID_W
    row = jnp.repeat(jnp.arange(rows, dtype=F32), GRID_W)
    col = jnp.tile(jnp.arange(GRID_W, dtype=F32), rows)
    inv = 1.0 / (ROPE_THETA ** (jnp.arange(AXIS_PAIRS, dtype=F32) / AXIS_PAIRS))
    ang = jnp.concatenate([row[:, None] * inv, col[:, None] * inv], axis=-1)
    cos, sin = jnp.cos(ang), jnp.sin(ang)
    cos_i = jnp.repeat(cos, 2, axis=-1)
    sin_i = jnp.stack([-sin, sin], axis=-1).reshape(s, HEAD_DIM)
    reps = LANES // HEAD_DIM
    return jnp.tile(cos_i, (1, reps)), jnp.tile(sin_i, (1, reps))


def kernel(x, c, ctx, c_ctx, w_ada, b_ada, norm_attn, w_in, q_norm_a, k_norm_a, q_norm_b, k_norm_b, lambda_q1,
           lambda_k1, lambda_q2, lambda_k2, subln_b, w_oa, w_ob, w_out, norm_mlp, router_w, router_b, w_gate,
           b_gate, w_up, b_up, w_down, b_down):
    bsz, s, d = x.shape
    assert ctx.shape[1] == TOK_TILE and s % TOK_TILE == 0 and s % TQ_A == 0 and s % TQ_B == 0 and s % GRID_W == 0
    assert w_ada.shape[0] == 1, "single-layer block"
    t = bsz * s
    assert t % CMB_TOK == 0 and (t * TOP_K) % FFN_ROWS == 0

    mod_rows = 8
    assert bsz + 1 <= mod_rows
    cvec = jnp.concatenate([c, c_ctx[None, :], jnp.zeros((mod_rows - bsz - 1, d), F32)], axis=0)
    mod = _adaln(cvec, w_ada[0], b_ada[0][None, :]).reshape(mod_rows, N_MOD, d)

    cos, sin = _rope_tables(s)
    tile_gain = lambda g: jnp.tile(g, LANES // HEAD_DIM)[None, :]
    gidx = jnp.arange(MXU_DIM) // HEAD_DIM
    gmat = jnp.where(gidx[:, None] == gidx[None, :], 1.0 / HEAD_DIM, 0.0).astype(BF16)
    w_qkv = w_in[0][:, :QKV_COLS].astype(BF16)
    w_g = w_in[0][:, QKV_COLS:].astype(BF16)

    qat, ka, vat, qbt, kb, vbt = _project(
        x, ctx, mod, norm_attn, w_qkv, cos, sin, tile_gain(q_norm_a[0]), tile_gain(k_norm_a[0]),
        tile_gain(q_norm_b[0]), tile_gain(k_norm_b[0]), gmat)

    ya = _attn_a(qat, ka, vat)
    yb = _attn_b(qbt, kb, vbt, lambda_q1, lambda_k1, lambda_q2, lambda_k2, subln_b[0][:, None])

    tri = jnp.arange(TOK_TILE)
    utri = (tri[:, None] < tri[None, :]).astype(BF16)
    x1, h2, route_t, counts = _merge(
        x.reshape(t, d), ya.reshape(t, A_Q_COLS), yb.reshape(t, B_WIDTH), mod, norm_attn, norm_mlp, w_g,
        w_oa[0].astype(BF16), w_ob[0].astype(BF16), w_out[0].astype(BF16), router_w[0].T, router_b[0][:, None], utri,
        s // TOK_TILE)

    counts = counts[:, 0].astype(I32)
    padded = (counts + FFN_ROWS - 1) // FFN_ROWS * FFN_ROWS
    pad_end = jnp.cumsum(padded)
    pad_start = pad_end - padded
    top_e = route_t[:TOP_K].astype(I32)
    rank = route_t[2 * TOP_K:3 * TOP_K].astype(I32)
    pos_t = pad_start[top_e] + rank
    n_blocks = t * TOP_K // FFN_ROWS + N_EXPERTS
    block_start = jnp.arange(n_blocks, dtype=I32) * FFN_ROWS
    block_e = jnp.minimum(jnp.sum((pad_end[None, :] <= block_start[:, None]).astype(I32), axis=1), N_EXPERTS - 1)
    n_used = (pad_end[-1:] // FFN_ROWS).astype(I32)
    first_at_or_after = lax.cummin(jnp.where(padded > 0, jnp.arange(N_EXPERTS, dtype=I32), N_EXPERTS), reverse=True)
    next_nonempty = jnp.concatenate([first_at_or_after[1:], jnp.full((1,), N_EXPERTS, I32)])
    next_e = jnp.where(next_nonempty < N_EXPERTS, next_nonempty, -1)[block_e]

    xb = _dispatch(pos_t.T.reshape(-1), pad_start + counts, padded - counts, n_used, h2, n_blocks)
    de = w_gate.shape[-1]
    yb_rows = _ffn(block_e, n_used, next_e, xb, w_gate[0], b_gate[0].reshape(N_EXPERTS, 1, de), w_up[0],
                   b_up[0].reshape(N_EXPERTS, 1, de), w_down[0], b_down[0].reshape(N_EXPERTS, 1, d))
    pos_tiles = pos_t.reshape(TOP_K, t // CMB_TOK, CMB_TOK).transpose(1, 0, 2).reshape(-1)
    out = _combine(pos_tiles, x1, route_t.T, mod, yb_rows, s // CMB_TOK)
    return out.reshape(bsz, s, d)
```

```python
import functools
import math

import jax
import jax.numpy as jnp
from jax import lax
from jax.experimental import pallas as pl
from jax.experimental.pallas import tpu as pltpu

F32 = jnp.float32
BF16 = jnp.bfloat16
I32 = jnp.int32

GRID_W = 64
HEAD_DIM = 64
A_HEADS = 8
A_KV_HEADS = 2
A_GROUP = A_HEADS // A_KV_HEADS
B_HEADS = 4
B_V_DIM = 2 * HEAD_DIM
N_EXPERTS = 32
TOP_K = 4
N_MOD = 6
ROPE_THETA = 10000.0
AXIS_PAIRS = HEAD_DIM // 4
SWIGLU_LIMIT = 7.0
SWIGLU_ALPHA = 1.702
EPS = 1e-6
SUBLN_EPS = 1e-5
LAM_INIT = 0.8 - 0.6 * math.exp(-0.3 * 0)

A_Q_COLS = A_HEADS * HEAD_DIM
A_KV_COLS = A_KV_HEADS * HEAD_DIM
B_QK_COLS = B_HEADS * 2 * HEAD_DIM
B_WIDTH = B_HEADS * B_V_DIM
QKV_COLS = A_Q_COLS + 2 * A_KV_COLS + 2 * B_QK_COLS + B_WIDTH

LANES = 128
MXU_DIM = 256
TOK_TILE = 256
TQ_A = 256
TQ_B = 512
SCORE_TILES_PER_TRIP = 64
ONES_ROWS = 16
A_VROWS = HEAD_DIM + ONES_ROWS
B_VROWS = B_V_DIM + ONES_ROWS
ADA_COLS = 1536
FFN_ROWS = 256
CMB_TOK = 128
ROUTE_ROWS = 16
DMA_QUEUES = 2
VMEM_LIMIT = 56 * 1024 * 1024


def _dot(a, b):
    return jnp.dot(a, b, preferred_element_type=F32)


def _split_bf16(a):
    hi = a.astype(BF16)
    lo = (a - hi.astype(F32)).astype(BF16)
    return hi, lo


def _dot3(a, b):
    a_hi, a_lo = _split_bf16(a)
    b_hi, b_lo = _split_bf16(b)
    return _dot(a_hi, b_hi) + (_dot(a_hi, b_lo) + _dot(a_lo, b_hi))


def _dot3_nt(a, b):
    dims = (((1,), (1,)), ((), ()))
    dot_nt = lambda u, v: lax.dot_general(u, v, dims, preferred_element_type=F32)
    a_hi, a_lo = _split_bf16(a)
    b_hi, b_lo = _split_bf16(b)
    return dot_nt(a_hi, b_hi) + (dot_nt(a_hi, b_lo) + dot_nt(a_lo, b_hi))


def _rms_rows(x, gain, eps):
    ms = jnp.mean(x * x, axis=-1, keepdims=True)
    return x * lax.rsqrt(ms + eps) * gain


def _adaln_kernel(c_ref, w_ref, b_ref, o_ref):
    c = c_ref[...]
    a = c * jax.nn.sigmoid(c)
    o_ref[...] = _dot3(a, w_ref[...]) + b_ref[...]


def _adaln(cvec, w_ada, b_ada):
    rows, d = cvec.shape
    n = w_ada.shape[1]
    return pl.pallas_call(
        _adaln_kernel,
        grid=(n // ADA_COLS,),
        in_specs=[
            pl.BlockSpec((rows, d), lambda j: (0, 0)),
            pl.BlockSpec((d, ADA_COLS), lambda j: (0, j)),
            pl.BlockSpec((1, ADA_COLS), lambda j: (0, j)),
        ],
        out_specs=pl.BlockSpec((rows, ADA_COLS), lambda j: (0, j)),
        out_shape=jax.ShapeDtypeStruct((rows, n), F32),
        compiler_params=pltpu.CompilerParams(vmem_limit_bytes=VMEM_LIMIT),
        name="adaln",
    )(cvec, w_ada, b_ada)


def _group_rms(y, gmat, gain):
    w_total = y.shape[1]
    outs = []
    for c0 in range(0, w_total, MXU_DIM):
        w = min(MXU_DIM, w_total - c0)
        yc = y[:, c0:c0 + w]
        ms = _dot((yc * yc).astype(BF16), gmat[:w, :w])
        reps = w // LANES
        g = gain if reps == 1 else jnp.concatenate([gain] * reps, axis=1)
        outs.append(yc * lax.rsqrt(ms + EPS) * g)
    return outs[0] if len(outs) == 1 else jnp.concatenate(outs, axis=1)


def _rope(y, cos, sin):
    rows = y.shape[0]
    lane = lax.broadcasted_iota(I32, (rows, LANES), 1)
    even = (lane % 2) == 0
    outs = []
    for c in range(y.shape[1] // LANES):
        ch = y[:, c * LANES:(c + 1) * LANES]
        partner = jnp.where(even, pltpu.roll(ch, LANES - 1, 1), pltpu.roll(ch, 1, 1))
        outs.append(ch * cos + partner * sin)
    return outs[0] if len(outs) == 1 else jnp.concatenate(outs, axis=1)


def _proj_kernel(x_ref, ctx_ref, mod_ref, gn_ref, w_ref, cos_ref, sin_ref, qna_ref, kna_ref, qnb_ref, knb_ref,
                 gmat_ref, qat_ref, ka_ref, vat_ref, qbt_ref, kb_ref, vbt_ref):
    b = pl.program_id(0)
    j = pl.program_id(1)
    is_ctx = j == 0
    xt = jnp.where(is_ctx, ctx_ref[0], x_ref[0])
    mrow = jnp.where(is_ctx, pl.num_programs(0), b)
    mod = mod_ref[mrow]
    h = _rms_rows(xt, gn_ref[...], EPS) * (1.0 + mod[1:2, :]) + mod[0:1, :]
    y = _dot(h.astype(BF16), w_ref[...])
    cos = jnp.where(is_ctx, 1.0, cos_ref[...])
    sin = jnp.where(is_ctx, 0.0, sin_ref[...])
    gmat = gmat_ref[...]
    scale = HEAD_DIM ** -0.5 * math.log2(math.e)

    o = 0
    qa = _rope(_group_rms(y[:, o:o + A_Q_COLS], gmat, qna_ref[...]), cos, sin) * scale
    o += A_Q_COLS
    ka = _rope(_group_rms(y[:, o:o + A_KV_COLS], gmat, kna_ref[...]), cos, sin)
    o += A_KV_COLS
    va = y[:, o:o + A_KV_COLS]
    o += A_KV_COLS
    qb = _rope(_group_rms(y[:, o:o + B_QK_COLS], gmat, qnb_ref[...]), cos, sin) * scale
    o += B_QK_COLS
    kb = _rope(_group_rms(y[:, o:o + B_QK_COLS], gmat, knb_ref[...]), cos, sin)
    o += B_QK_COLS
    vb = y[:, o:o + B_WIDTH]

    @pl.when(j > 0)
    def _():
        qat_ref[0] = qa.T.astype(BF16)
        qbt_ref[0] = qb.T.astype(BF16)

    ka_ref[0] = ka.astype(BF16)
    kb_ref[0] = kb.astype(BF16)
    ones = jnp.ones((ONES_ROWS, xt.shape[0]), BF16)
    for dst_ref, vt, width in ((vat_ref, va.T.astype(BF16), HEAD_DIM), (vbt_ref, vb.T.astype(BF16), B_V_DIM)):
        stride = width + ONES_ROWS
        for hd in range(vt.shape[0] // width):
            dst_ref[0, 0, hd * stride:hd * stride + width, :] = vt[hd * width:(hd + 1) * width, :]
            dst_ref[0, 0, hd * stride + width:(hd + 1) * stride, :] = ones


def _project(x, ctx, mod, gn, w_qkv, cos, sin, qna, kna, qnb, knb, gmat):
    bsz, s, d = x.shape
    n_lat = s // TOK_TILE
    n_key = n_lat + 1
    lk = n_key * TOK_TILE
    lat = lambda b, j: (b, jnp.maximum(j - 1, 0), 0)
    const2 = lambda b, j: (0, 0)
    tab = lambda b, j: (jnp.maximum(j - 1, 0), 0)
    return pl.pallas_call(
        _proj_kernel,
        grid=(bsz, n_key),
        in_specs=[
            pl.BlockSpec((1, TOK_TILE, d), lat),
            pl.BlockSpec((1, TOK_TILE, d), lambda b, j: (b, 0, 0)),
            pl.BlockSpec(mod.shape, lambda b, j: (0, 0, 0)),
            pl.BlockSpec((1, d), const2),
            pl.BlockSpec((d, QKV_COLS), const2),
            pl.BlockSpec((TOK_TILE, LANES), tab),
            pl.BlockSpec((TOK_TILE, LANES), tab),
            pl.BlockSpec((1, LANES), const2),
            pl.BlockSpec((1, LANES), const2),
            pl.BlockSpec((1, LANES), const2),
            pl.BlockSpec((1, LANES), const2),
            pl.BlockSpec((MXU_DIM, MXU_DIM), const2),
        ],
        out_specs=[
            pl.BlockSpec((1, A_Q_COLS, TOK_TILE), lambda b, j: (b, 0, jnp.maximum(j - 1, 0))),
            pl.BlockSpec((1, TOK_TILE, A_KV_COLS), lambda b, j: (b, j, 0)),
            pl.BlockSpec((1, 1, A_KV_HEADS * A_VROWS, TOK_TILE), lambda b, j: (b, j, 0, 0)),
            pl.BlockSpec((1, B_QK_COLS, TOK_TILE), lambda b, j: (b, 0, jnp.maximum(j - 1, 0))),
            pl.BlockSpec((1, TOK_TILE, B_QK_COLS), lambda b, j: (b, j, 0)),
            pl.BlockSpec((1, 1, B_HEADS * B_VROWS, TOK_TILE), lambda b, j: (b, j, 0, 0)),
        ],
        out_shape=[
            jax.ShapeDtypeStruct((bsz, A_Q_COLS, s), BF16),
            jax.ShapeDtypeStruct((bsz, lk, A_KV_COLS), BF16),
            jax.ShapeDtypeStruct((bsz, n_key, A_KV_HEADS * A_VROWS, TOK_TILE), BF16),
            jax.ShapeDtypeStruct((bsz, B_QK_COLS, s), BF16),
            jax.ShapeDtypeStruct((bsz, lk, B_QK_COLS), BF16),
            jax.ShapeDtypeStruct((bsz, n_key, B_HEADS * B_VROWS, TOK_TILE), BF16),
        ],
        compiler_params=pltpu.CompilerParams(vmem_limit_bytes=VMEM_LIMIT),
        name="proj",
    )(x, ctx, mod, gn, w_qkv, cos, sin, qna, kna, qnb, knb, gmat)


def _flash(k_ref, vt_ref, wq, pv, dv, s_even, s_odd):
    n = wq.shape[1]
    n_tiles = n // MXU_DIM
    n_chunks = vt_ref.shape[1]
    cols = [slice(t * MXU_DIM, (t + 1) * MXU_DIM) for t in range(n_tiles)]
    wqs = [wq[:, cs] for cs in cols]

    def scores(c, t, dst):
        start = pl.multiple_of(c * TOK_TILE, TOK_TILE)
        dst[:, cols[t]] = _dot(k_ref[0, pl.ds(start, TOK_TILE), :], wqs[t])

    def chunk(c, cur, nxt, state):
        out = []
        for t in range(n_tiles):
            if nxt is not None:
                scores(c + 1, t, nxt)
            m, acc = state[t]
            s = cur[:, cols[t]]
            m_new = jnp.maximum(m, jnp.max(s, axis=0, keepdims=True))
            alpha = jnp.exp2(m - m_new)
            p = jnp.exp2(s - m_new).astype(BF16)
            out.append((m_new, alpha * acc + pv(t, vt_ref[0, c], p)))
        return tuple(out)

    for t in range(n_tiles):
        scores(0, t, s_even)

    bufs = (s_even, s_odd)

    per_trip = SCORE_TILES_PER_TRIP // n_tiles
    assert per_trip % 2 == 0, "the two score buffers alternate, so a trip must hold an even number of chunks"

    def trip(i, state):
        for u in range(per_trip):
            state = chunk(i * per_trip + u, bufs[u % 2], bufs[(u + 1) % 2], state)
        return state

    init = tuple((jnp.full((1, MXU_DIM), -jnp.inf, F32), jnp.zeros((dv + ONES_ROWS, MXU_DIM), F32))
                 for _ in range(n_tiles))
    n_trips = (n_chunks - 1) // per_trip
    state = lax.fori_loop(0, n_trips, trip, init)
    for c in range(n_trips * per_trip, n_chunks):
        state = chunk(c, bufs[c % 2], bufs[(c + 1) % 2] if c + 1 < n_chunks else None, state)
    return jnp.concatenate([acc for _, acc in state], axis=1)


def _attn_a_kernel(qt_ref, k_ref, vt_ref, o_ref, s_even, s_odd):
    tq = qt_ref.shape[2]
    half = A_GROUP * tq
    zeros = jnp.zeros((HEAD_DIM, half), BF16)
    rows = []
    for g in range(A_KV_HEADS):
        heads = range(g * A_GROUP, (g + 1) * A_GROUP)
        qg = jnp.concatenate([qt_ref[0, h * HEAD_DIM:(h + 1) * HEAD_DIM, :] for h in heads], axis=1)
        rows.append(jnp.concatenate([qg if gg == g else zeros for gg in range(A_KV_HEADS)], axis=1))
    wq = jnp.concatenate(rows, axis=0)

    def pv(t, vc, p):
        g = t * MXU_DIM // half
        return _dot(vc[g * A_VROWS:(g + 1) * A_VROWS], p)

    acc = _flash(k_ref, vt_ref, wq, pv, HEAD_DIM, s_even, s_odd)
    o = acc[:HEAD_DIM] / acc[HEAD_DIM:HEAD_DIM + 1]
    outs = [o[:, h * tq:(h + 1) * tq].T for h in range(A_HEADS)]
    o_ref[0] = jnp.concatenate(outs, axis=1).astype(o_ref.dtype)


def _attn_a(qat, ka, vat):
    bsz, _, s = qat.shape
    lk = ka.shape[1]
    n_chunks = vat.shape[1]
    return pl.pallas_call(
        _attn_a_kernel,
        grid=(bsz, s // TQ_A),
        in_specs=[
            pl.BlockSpec((1, A_Q_COLS, TQ_A), lambda b, i: (b, 0, i)),
            pl.BlockSpec((1, lk, A_KV_COLS), lambda b, i: (b, 0, 0)),
            pl.BlockSpec((1, n_chunks, A_KV_HEADS * A_VROWS, TOK_TILE), lambda b, i: (b, 0, 0, 0)),
        ],
        out_specs=pl.BlockSpec((1, TQ_A, A_Q_COLS), lambda b, i: (b, i, 0)),
        out_shape=jax.ShapeDtypeStruct((bsz, s, A_Q_COLS), BF16),
        scratch_shapes=[pltpu.VMEM((TOK_TILE, A_HEADS * TQ_A), F32)] * 2,
        compiler_params=pltpu.CompilerParams(vmem_limit_bytes=VMEM_LIMIT),
        name="attn_a",
    )(qat, ka, vat)


def _attn_b_kernel(qt_ref, k_ref, vt_ref, lq1_ref, lk1_ref, lq2_ref, lk2_ref, sub_ref, o_ref, s_even, s_odd):
    tq = qt_ref.shape[2]
    zeros = jnp.zeros((HEAD_DIM, tq), BF16)
    q1 = qt_ref[0, :HEAD_DIM, :]
    q2 = qt_ref[0, HEAD_DIM:, :]
    wq = jnp.concatenate([jnp.concatenate([q1, zeros], axis=1), jnp.concatenate([zeros, q2], axis=1)], axis=0)
    acc = _flash(k_ref, vt_ref, wq, lambda t, vc, p: _dot(vc, p), B_V_DIM, s_even, s_odd)
    r = acc[:B_V_DIM] / acc[B_V_DIM:B_V_DIM + 1]
    lam = (jnp.exp(jnp.sum(lq1_ref[...] * lk1_ref[...], axis=1, keepdims=True))
           - jnp.exp(jnp.sum(lq2_ref[...] * lk2_ref[...], axis=1, keepdims=True)) + LAM_INIT)
    o = r[:, :tq] - lam * r[:, tq:]
    ms = jnp.mean(o * o, axis=0, keepdims=True)
    o = o * lax.rsqrt(ms + SUBLN_EPS) * sub_ref[...] * (1.0 - LAM_INIT)
    o_ref[0] = o.T.astype(o_ref.dtype)


def _attn_b(qbt, kb, vbt, lq1, lk1, lq2, lk2, subln):
    bsz, _, s = qbt.shape
    lk = kb.shape[1]
    n_chunks = vbt.shape[1]
    vec = pl.BlockSpec((1, HEAD_DIM), lambda b, h, i: (0, 0))
    return pl.pallas_call(
        _attn_b_kernel,
        grid=(bsz, B_HEADS, s // TQ_B),
        in_specs=[
            pl.BlockSpec((1, 2 * HEAD_DIM, TQ_B), lambda b, h, i: (b, h, i)),
            pl.BlockSpec((1, lk, 2 * HEAD_DIM), lambda b, h, i: (b, 0, h)),
            pl.BlockSpec((1, n_chunks, B_VROWS, TOK_TILE), lambda b, h, i: (b, 0, h, 0)),
            vec, vec, vec, vec,
            pl.BlockSpec((B_V_DIM, 1), lambda b, h, i: (0, 0)),
        ],
        out_specs=pl.BlockSpec((1, TQ_B, B_V_DIM), lambda b, h, i: (b, i, h)),
        out_shape=jax.ShapeDtypeStruct((bsz, s, B_WIDTH), BF16),
        scratch_shapes=[pltpu.VMEM((TOK_TILE, 2 * TQ_B), F32)] * 2,
        compiler_params=pltpu.CompilerParams(vmem_limit_bytes=VMEM_LIMIT),
        name="attn_b",
    )(qbt, kb, vbt, lq1, lk1, lq2, lk2, subln)


def _merge_kernel(tiles_per_sample, x_ref, ya_ref, yb_ref, mod_ref, gna_ref, gnm_ref, wg_ref, woa_ref, wob_ref,
                  wout_ref, rw_ref, rb_ref, utri_ref, x1_ref, h2_ref, route_ref, cnt_ref, carry_ref):
    i = pl.program_id(0)
    b = i // tiles_per_sample
    mod = mod_ref[b]
    x = x_ref[...]
    d = x.shape[1]
    h = _rms_rows(x, gna_ref[...], EPS) * (1.0 + mod[1:2, :]) + mod[0:1, :]
    gates = _dot(h.astype(BF16), wg_ref[...])
    ga = jax.nn.sigmoid(gates[:, :d])
    gb = jax.nn.sigmoid(gates[:, d:])
    merged = ga * _dot(ya_ref[...], woa_ref[...]) + gb * _dot(yb_ref[...], wob_ref[...])
    x1 = x + mod[2:3, :] * _dot(merged.astype(BF16), wout_ref[...])
    x1_ref[...] = x1
    h2 = _rms_rows(x1, gnm_ref[...], EPS) * (1.0 + mod[4:5, :]) + mod[3:4, :]
    h2_ref[...] = h2

    logits = _dot3_nt(rw_ref[...], h2) + rb_ref[...]
    rows = logits.shape[1]
    eiota = lax.broadcasted_iota(I32, (N_EXPERTS, rows), 0)
    work = logits
    idxs, vals = [], []
    for _ in range(TOP_K):
        mx = jnp.max(work, axis=0, keepdims=True)
        idx = jnp.min(jnp.where(work == mx, eiota, N_EXPERTS), axis=0, keepdims=True)
        idxs.append(idx)
        vals.append(mx)
        work = jnp.where(eiota == idx, -jnp.inf, work)
    exps = [jnp.exp(v - vals[0]) for v in vals]
    denom = exps[0] + exps[1] + exps[2] + exps[3]
    weights = [e / denom for e in exps]

    @pl.when(i == 0)
    def _():
        carry_ref[...] = jnp.zeros_like(carry_ref)

    hits = [(eiota == idx) for idx in idxs]
    onehot = (hits[0] | hits[1] | hits[2] | hits[3]).astype(F32)
    before = _dot(onehot.astype(BF16), utri_ref[...]) + carry_ref[...]
    ranks = [jnp.sum(jnp.where(hit, before, 0.0), axis=0, keepdims=True) for hit in hits]
    carry = carry_ref[...] + jnp.sum(onehot, axis=1, keepdims=True)
    carry_ref[...] = carry
    cnt_ref[...] = carry

    lines = [v.astype(F32) for v in idxs] + weights + ranks
    riota = lax.broadcasted_iota(I32, route_ref.shape, 0)
    route = jnp.zeros(route_ref.shape, F32)
    for k, line in enumerate(lines):
        route = jnp.where(riota == k, line, route)
    route_ref[...] = route


def _merge(x2, ya2, yb2, mod, gna, gnm, wg, woa, wob, wout, rw, rb, utri, tiles_per_sample):
    t, d = x2.shape
    const2 = lambda i: (0, 0)
    row = lambda i: (i, 0)
    full = lambda a: pl.BlockSpec(a.shape, const2)
    return pl.pallas_call(
        functools.partial(_merge_kernel, tiles_per_sample),
        grid=(t // TOK_TILE,),
        in_specs=[
            pl.BlockSpec((TOK_TILE, d), row),
            pl.BlockSpec((TOK_TILE, ya2.shape[1]), row),
            pl.BlockSpec((TOK_TILE, yb2.shape[1]), row),
            pl.BlockSpec(mod.shape, lambda i: (0, 0, 0)),
            full(gna), full(gnm), full(wg), full(woa), full(wob), full(wout), full(rw), full(rb), full(utri),
        ],
        out_specs=[
            pl.BlockSpec((TOK_TILE, d), row),
            pl.BlockSpec((TOK_TILE, d), row),
            pl.BlockSpec((ROUTE_ROWS, TOK_TILE), lambda i: (0, i)),
            pl.BlockSpec((N_EXPERTS, 1), const2),
        ],
        out_shape=[
            jax.ShapeDtypeStruct((t, d), F32),
            jax.ShapeDtypeStruct((t, d), F32),
            jax.ShapeDtypeStruct((ROUTE_ROWS, t), F32),
            jax.ShapeDtypeStruct((N_EXPERTS, 1), F32),
        ],
        scratch_shapes=[pltpu.VMEM((N_EXPERTS, 1), F32)],
        compiler_params=pltpu.CompilerParams(vmem_limit_bytes=VMEM_LIMIT),
        name="merge",
    )(x2, ya2, yb2, mod, gna, gnm, wg, woa, wob, wout, rw, rb, utri)


def _to_row_tiles(dst_ref, val):
    n, d = val.shape
    sub = d // LANES
    for c in range(sub):
        dst_ref[pl.ds(c, n, stride=sub), :] = val[:, c * LANES:(c + 1) * LANES]


def _from_row_tiles(src_ref, first_row, n, d):
    sub = d // LANES
    return jnp.concatenate([src_ref[pl.ds(first_row * sub + c, n, stride=sub), :] for c in range(sub)], axis=1)


def _row_tile_copy(src, src_row, dst, dst_row, sub, sem):
    src_at = pl.ds(pl.multiple_of(src_row * sub, sub), sub)
    dst_at = pl.ds(pl.multiple_of(dst_row * sub, sub), sub)
    return pltpu.make_async_copy(src.at[src_at], dst.at[dst_at], sem)


def _dispatch_kernel(pos_ref, padfrom_ref, padcnt_ref, nu_ref, h_ref, xb_out, tiles, zeros, sem, zsem):
    i = pl.program_id(0)
    n = pl.num_programs(0)
    slot = i % 2
    n_tok, d = h_ref.shape
    sub = d // LANES
    blk = zeros.shape[0]
    n_blocks = xb_out.shape[0] // blk

    def zero_fill(wait):
        def go(copy):
            copy.wait() if wait else copy.start()

        def per_expert(e, carry):
            cnt = padcnt_ref[e]
            for bit in reversed(range((FFN_ROWS - 1).bit_length())):
                rows = 1 << bit

                @pl.when((cnt & rows) != 0)
                def _():
                    done = (cnt >> (bit + 1)) << (bit + 1)
                    at = pl.multiple_of((padfrom_ref[e] + done) * sub, sub)
                    go(pltpu.make_async_copy(zeros.at[pl.ds(0, rows * sub)], xb_out.at[pl.ds(at, rows * sub)], zsem))
            return carry

        lax.fori_loop(0, N_EXPERTS, per_expert, 0)

        def per_block(j, carry):
            go(pltpu.make_async_copy(zeros, xb_out.at[pl.ds(pl.multiple_of(j * blk, blk), blk)], zsem))
            return carry

        lax.fori_loop(nu_ref[0], n_blocks, per_block, 0)

    @pl.when(i == 0)
    def _():
        zeros[...] = jnp.zeros_like(zeros)
        zero_fill(wait=False)
        zero_fill(wait=True)

    def drain(sl):
        for _ in range(TOP_K):
            pltpu.make_async_copy(tiles.at[sl], xb_out.at[pl.ds(0, n_tok * sub)], sem.at[sl]).wait()

    @pl.when(i >= 2)
    def _():
        drain(slot)

    _to_row_tiles(tiles.at[slot], h_ref[...])
    base = i * n_tok

    def issue(t, carry):
        for k in range(TOP_K):
            _row_tile_copy(tiles.at[slot], t, xb_out, pos_ref[(base + t) * TOP_K + k], sub,
                           sem.at[slot]).start(priority=k % DMA_QUEUES)
        return carry

    lax.fori_loop(0, n_tok, issue, 0)

    @pl.when((i == n - 1) & (n >= 2))
    def _():
        drain(1 - slot)

    @pl.when(i == n - 1)
    def _():
        drain(slot)


def _dispatch(pos_flat, pad_from, pad_cnt, n_used, h2, n_blocks):
    t, d = h2.shape
    sub = d // LANES
    return pl.pallas_call(
        _dispatch_kernel,
        grid_spec=pltpu.PrefetchScalarGridSpec(
            num_scalar_prefetch=4,
            grid=(t // TOK_TILE,),
            in_specs=[pl.BlockSpec((TOK_TILE, d), lambda i, *_: (i, 0))],
            out_specs=pl.BlockSpec(memory_space=pl.ANY),
            scratch_shapes=[pltpu.VMEM((2, TOK_TILE * sub, LANES), F32), pltpu.VMEM((FFN_ROWS * sub, LANES), F32),
                            pltpu.SemaphoreType.DMA((2,)), pltpu.SemaphoreType.DMA(())],
        ),
        out_shape=jax.ShapeDtypeStruct((n_blocks * FFN_ROWS * sub, LANES), F32),
        compiler_params=pltpu.CompilerParams(vmem_limit_bytes=VMEM_LIMIT),
        name="dispatch",
    )(pos_flat, pad_from, pad_cnt, n_used, h2)


def _ffn_kernel(be_ref, nu_ref, nxt_ref, x_ref, wg_hbm, bg_ref, wu_hbm, bu_ref, wd_hbm, bd_ref, o_ref,
                stage_g, stage_u, stage_d, wg_s, wu_s, wd_s, wsem):
    j = pl.program_id(0)
    prev = be_ref[jnp.maximum(j - 1, 0)]
    fresh = ((j == 0) | (be_ref[j] != prev)) & (j < nu_ref[0])
    streams = ((wg_hbm, stage_g, wg_s), (wu_hbm, stage_u, wu_s), (wd_hbm, stage_d, wd_s))

    def weight_copies(expert):
        return [pltpu.make_async_copy(hbm.at[expert], stage, wsem.at[n]) for n, (hbm, stage, _) in enumerate(streams)]

    @pl.when(j == 0)
    def _():
        for copy in weight_copies(be_ref[0]):
            copy.start()

    @pl.when(fresh)
    def _():
        for copy, (_, stage, dst) in zip(weight_copies(be_ref[j]), streams):
            copy.wait()
            dst[...] = stage[...].astype(BF16)

        @pl.when(nxt_ref[j] >= 0)
        def _():
            for copy in weight_copies(nxt_ref[j]):
                copy.start()

    @pl.when(j < nu_ref[0])
    def _():
        x = _from_row_tiles(x_ref, 0, FFN_ROWS, wg_s.shape[0]).astype(BF16)
        gate = jnp.minimum(_dot(x, wg_s[...]) + bg_ref[0], SWIGLU_LIMIT)
        up = jnp.clip(_dot(x, wu_s[...]) + bu_ref[0], -SWIGLU_LIMIT, SWIGLU_LIMIT)
        act = (up + 1.0) * (gate * jax.nn.sigmoid(SWIGLU_ALPHA * gate))
        _to_row_tiles(o_ref, _dot(act.astype(BF16), wd_s[...]) + bd_ref[0])

    @pl.when(j >= nu_ref[0])
    def _():
        o_ref[...] = jnp.zeros_like(o_ref)


def _ffn(block_e, n_used, next_e, xb, w_gate, b_gate, w_up, b_up, w_down, b_down):
    d, de = w_gate.shape[1:]
    blk = FFN_ROWS * d // LANES
    n_blocks = xb.shape[0] // blk
    xrow = lambda j, be, nu, nxt: (jnp.minimum(j, nu[0] - 1), 0)
    bsel = lambda j, be, nu, nxt: (be[j], 0, 0)
    any_spec = pl.BlockSpec(memory_space=pl.ANY)
    return pl.pallas_call(
        _ffn_kernel,
        grid_spec=pltpu.PrefetchScalarGridSpec(
            num_scalar_prefetch=3,
            grid=(n_blocks,),
            in_specs=[
                pl.BlockSpec((blk, LANES), xrow),
                any_spec,
                pl.BlockSpec((1, 1, de), bsel),
                any_spec,
                pl.BlockSpec((1, 1, de), bsel),
                any_spec,
                pl.BlockSpec((1, 1, d), bsel),
            ],
            out_specs=pl.BlockSpec((blk, LANES), lambda j, be, nu, nxt: (j, 0)),
            scratch_shapes=[pltpu.VMEM((d, de), F32), pltpu.VMEM((d, de), F32), pltpu.VMEM((de, d), F32),
                            pltpu.VMEM((d, de), BF16), pltpu.VMEM((d, de), BF16), pltpu.VMEM((de, d), BF16),
                            pltpu.SemaphoreType.DMA((3,))],
        ),
        out_shape=jax.ShapeDtypeStruct(xb.shape, F32),
        compiler_params=pltpu.CompilerParams(vmem_limit_bytes=VMEM_LIMIT),
        name="ffn",
    )(block_e, n_used, next_e, xb, w_gate, b_gate, w_up, b_up, w_down, b_down)


def _combine_kernel(tiles_per_sample, pos_ref, x1_ref, route_ref, mod_ref, yb_hbm, o_ref, buf, sem):
    i = pl.program_id(0)
    n = pl.num_programs(0)
    rows = TOP_K * CMB_TOK
    d = x1_ref.shape[1]
    sub = d // LANES

    def issue(tile, slot):
        def body(r2, carry):
            for u in range(DMA_QUEUES):
                r = r2 * DMA_QUEUES + u
                _row_tile_copy(yb_hbm, pos_ref[tile * rows + r], buf.at[slot], r, sub, sem.at[slot]).start(priority=u)
            return carry
        lax.fori_loop(0, rows // DMA_QUEUES, body, 0)

    @pl.when(i == 0)
    def _():
        issue(0, 0)

    @pl.when(i + 1 < n)
    def _():
        issue(i + 1, (i + 1) % 2)

    slot = i % 2
    pltpu.make_async_copy(yb_hbm.at[pl.ds(0, rows * sub)], buf.at[slot], sem.at[slot]).wait()

    b = i // tiles_per_sample
    g_m = mod_ref[b][5:6, :]
    route = route_ref[...]
    acc = jnp.zeros(x1_ref.shape, F32)
    for k in range(TOP_K):
        acc = acc + route[:, TOP_K + k:TOP_K + k + 1] * _from_row_tiles(buf.at[slot], k * CMB_TOK, CMB_TOK, d)
    o_ref[...] = x1_ref[...] + g_m * acc


def _combine(pos_tiles, x1, route, mod, yb, tiles_per_sample):
    t, d = x1.shape
    row = lambda i, pos: (i, 0)
    return pl.pallas_call(
        functools.partial(_combine_kernel, tiles_per_sample),
        grid_spec=pltpu.PrefetchScalarGridSpec(
            num_scalar_prefetch=1,
            grid=(t // CMB_TOK,),
            in_specs=[
                pl.BlockSpec((CMB_TOK, d), row),
                pl.BlockSpec((CMB_TOK, route.shape[1]), row),
                pl.BlockSpec(mod.shape, lambda i, pos: (0, 0, 0)),
                pl.BlockSpec(memory_space=pl.ANY),
            ],
            out_specs=pl.BlockSpec((CMB_TOK, d), row),
            scratch_shapes=[pltpu.VMEM((2, TOP_K * CMB_TOK * d // LANES, LANES), F32),
                            pltpu.SemaphoreType.DMA((2,))],
        ),
        out_shape=jax.ShapeDtypeStruct((t, d), F32),
        compiler_params=pltpu.CompilerParams(vmem_limit_bytes=VMEM_LIMIT),
        name="combine",
    )(pos_tiles, x1, route, mod, yb)


def _rope_tables(s):
    rows = s // GRID_W
    row = jnp.repeat(jnp.arange(rows, dtype=F32), GRID_W)
    col = jnp.tile(jnp.arange(GRID_W, dtype=F32), rows)
    inv = 1.0 / (ROPE_THETA ** (jnp.arange(AXIS_PAIRS, dtype=F32) / AXIS_PAIRS))
    ang = jnp.concatenate([row[:, None] * inv, col[:, None] * inv], axis=-1)
    cos, sin = jnp.cos(ang), jnp.sin(ang)
    cos_i = jnp.repeat(cos, 2, axis=-1)
    sin_i = jnp.stack([-sin, sin], axis=-1).reshape(s, HEAD_DIM)
    reps = LANES // HEAD_DIM
    return jnp.tile(cos_i, (1, reps)), jnp.tile(sin_i, (1, reps))


def kernel(x, c, ctx, c_ctx, w_ada, b_ada, norm_attn, w_in, q_norm_a, k_norm_a, q_norm_b, k_norm_b, lambda_q1,
           lambda_k1, lambda_q2, lambda_k2, subln_b, w_oa, w_ob, w_out, norm_mlp, router_w, router_b, w_gate,
           b_gate, w_up, b_up, w_down, b_down):
    bsz, s, d = x.shape
    assert ctx.shape[1] == TOK_TILE and s % TOK_TILE == 0 and s % TQ_A == 0 and s % TQ_B == 0 and s % GRID_W == 0
    assert w_ada.shape[0] == 1, "single-layer block"
    t = bsz * s
    assert t % CMB_TOK == 0 and (t * TOP_K) % FFN_ROWS == 0

    mod_rows = 8
    assert bsz + 1 <= mod_rows
    cvec = jnp.concatenate([c, c_ctx[None, :], jnp.zeros((mod_rows - bsz - 1, d), F32)], axis=0)
    mod = _adaln(cvec, w_ada[0], b_ada[0][None, :]).reshape(mod_rows, N_MOD, d)

    cos, sin = _rope_tables(s)
    tile_gain = lambda g: jnp.tile(g, LANES // HEAD_DIM)[None, :]
    gidx = jnp.arange(MXU_DIM) // HEAD_DIM
    gmat = jnp.where(gidx[:, None] == gidx[None, :], 1.0 / HEAD_DIM, 0.0).astype(BF16)
    w_qkv = w_in[0][:, :QKV_COLS].astype(BF16)
    w_g = w_in[0][:, QKV_COLS:].astype(BF16)

    qat, ka, vat, qbt, kb, vbt = _project(
        x, ctx, mod, norm_attn, w_qkv, cos, sin, tile_gain(q_norm_a[0]), tile_gain(k_norm_a[0]),
        tile_gain(q_norm_b[0]), tile_gain(k_norm_b[0]), gmat)

    ya = _attn_a(qat, ka, vat)
    yb = _attn_b(qbt, kb, vbt, lambda_q1, lambda_k1, lambda_q2, lambda_k2, subln_b[0][:, None])

    tri = jnp.arange(TOK_TILE)
    utri = (tri[:, None] < tri[None, :]).astype(BF16)
    x1, h2, route_t, counts = _merge(
        x.reshape(t, d), ya.reshape(t, A_Q_COLS), yb.reshape(t, B_WIDTH), mod, norm_attn, norm_mlp, w_g,
        w_oa[0].astype(BF16), w_ob[0].astype(BF16), w_out[0].astype(BF16), router_w[0].T, router_b[0][:, None], utri,
        s // TOK_TILE)

    counts = counts[:, 0].astype(I32)
    padded = (counts + FFN_ROWS - 1) // FFN_ROWS * FFN_ROWS
    pad_end = jnp.cumsum(padded)
    pad_start = pad_end - padded
    top_e = route_t[:TOP_K].astype(I32)
    rank = route_t[2 * TOP_K:3 * TOP_K].astype(I32)
    experts = jnp.arange(N_EXPERTS, dtype=I32)
    start_of = jnp.sum(jnp.where(top_e[..., None] == experts, pad_start, 0), axis=-1)
    pos_t = start_of + rank
    n_blocks = t * TOP_K // FFN_ROWS + N_EXPERTS
    block_start = jnp.arange(n_blocks, dtype=I32) * FFN_ROWS
    block_e = jnp.minimum(jnp.sum((pad_end[None, :] <= block_start[:, None]).astype(I32), axis=1), N_EXPERTS - 1)
    n_used = (pad_end[-1:] // FFN_ROWS).astype(I32)
    first_at_or_after = lax.cummin(jnp.where(padded > 0, jnp.arange(N_EXPERTS, dtype=I32), N_EXPERTS), reverse=True)
    next_nonempty = jnp.concatenate([first_at_or_after[1:], jnp.full((1,), N_EXPERTS, I32)])
    next_of = jnp.where(next_nonempty < N_EXPERTS, next_nonempty, -1)
    next_e = jnp.sum(jnp.where(block_e[:, None] == experts, next_of, 0), axis=-1)

    xb = _dispatch(pos_t.T.reshape(-1), pad_start + counts, padded - counts, n_used, h2, n_blocks)
    de = w_gate.shape[-1]
    yb_rows = _ffn(block_e, n_used, next_e, xb, w_gate[0], b_gate[0].reshape(N_EXPERTS, 1, de), w_up[0],
                   b_up[0].reshape(N_EXPERTS, 1, de), w_down[0], b_down[0].reshape(N_EXPERTS, 1, d))
    pos_tiles = pos_t.reshape(TOP_K, t // CMB_TOK, CMB_TOK).transpose(1, 0, 2).reshape(-1)
    out = _combine(pos_tiles, x1, route_t.T, mod, yb_rows, s // CMB_TOK)
    return out.reshape(bsz, s, d)
```

```python
import functools
import math

import jax
import jax.numpy as jnp
from jax import lax
from jax.experimental import pallas as pl
from jax.experimental.pallas import tpu as pltpu

F32 = jnp.float32
BF16 = jnp.bfloat16
I32 = jnp.int32

GRID_W = 64
HEAD_DIM = 64
A_HEADS = 8
A_KV_HEADS = 2
A_GROUP = A_HEADS // A_KV_HEADS
B_HEADS = 4
B_V_DIM = 2 * HEAD_DIM
N_EXPERTS = 32
TOP_K = 4
N_MOD = 6
ROPE_THETA = 10000.0
AXIS_PAIRS = HEAD_DIM // 4
SWIGLU_LIMIT = 7.0
SWIGLU_ALPHA = 1.702
EPS = 1e-6
SUBLN_EPS = 1e-5
LAM_INIT = 0.8 - 0.6 * math.exp(-0.3 * 0)

A_Q_COLS = A_HEADS * HEAD_DIM
A_KV_COLS = A_KV_HEADS * HEAD_DIM
B_QK_COLS = B_HEADS * 2 * HEAD_DIM
B_WIDTH = B_HEADS * B_V_DIM
QKV_COLS = A_Q_COLS + 2 * A_KV_COLS + 2 * B_QK_COLS + B_WIDTH

LANES = 128
MXU_DIM = 256
TOK_TILE = 256
MERGE_TOK = 4 * TOK_TILE
TQ_A = 256
TQ_B = 512
SCORE_TILES_PER_TRIP = 64
ONES_ROWS = 16
A_VROWS = HEAD_DIM + ONES_ROWS
B_VROWS = B_V_DIM + ONES_ROWS
ADA_COLS = 1536
FFN_ROWS = 256
CMB_TOK = 128
ROUTE_ROWS = 16
DMA_QUEUES = 2
VMEM_LIMIT = 56 * 1024 * 1024


def _dot(a, b):
    return jnp.dot(a, b, preferred_element_type=F32)


def _split_bf16(a):
    hi = a.astype(BF16)
    lo = (a - hi.astype(F32)).astype(BF16)
    return hi, lo


def _dot3(a, b):
    a_hi, a_lo = _split_bf16(a)
    b_hi, b_lo = _split_bf16(b)
    return _dot(a_hi, b_hi) + (_dot(a_hi, b_lo) + _dot(a_lo, b_hi))


def _dot3_nt(a, b):
    dims = (((1,), (1,)), ((), ()))
    dot_nt = lambda u, v: lax.dot_general(u, v, dims, preferred_element_type=F32)
    a_hi, a_lo = _split_bf16(a)
    b_hi, b_lo = _split_bf16(b)
    return dot_nt(a_hi, b_hi) + (dot_nt(a_hi, b_lo) + dot_nt(a_lo, b_hi))


def _rms_rows(x, gain, eps):
    ms = jnp.mean(x * x, axis=-1, keepdims=True)
    return x * lax.rsqrt(ms + eps) * gain


def _adaln_kernel(c_ref, w_ref, b_ref, o_ref):
    c = c_ref[...]
    a = c * jax.nn.sigmoid(c)
    o_ref[...] = _dot3(a, w_ref[...]) + b_ref[...]


def _adaln(cvec, w_ada, b_ada):
    rows, d = cvec.shape
    n = w_ada.shape[1]
    return pl.pallas_call(
        _adaln_kernel,
        grid=(n // ADA_COLS,),
        in_specs=[
            pl.BlockSpec((rows, d), lambda j: (0, 0)),
            pl.BlockSpec((d, ADA_COLS), lambda j: (0, j)),
            pl.BlockSpec((1, ADA_COLS), lambda j: (0, j)),
        ],
        out_specs=pl.BlockSpec((rows, ADA_COLS), lambda j: (0, j)),
        out_shape=jax.ShapeDtypeStruct((rows, n), F32),
        compiler_params=pltpu.CompilerParams(vmem_limit_bytes=VMEM_LIMIT),
        name="adaln",
    )(cvec, w_ada, b_ada)


def _group_rms(y, gmat, gain):
    w_total = y.shape[1]
    outs = []
    for c0 in range(0, w_total, MXU_DIM):
        w = min(MXU_DIM, w_total - c0)
        yc = y[:, c0:c0 + w]
        ms = _dot((yc * yc).astype(BF16), gmat[:w, :w])
        reps = w // LANES
        g = gain if reps == 1 else jnp.concatenate([gain] * reps, axis=1)
        outs.append(yc * lax.rsqrt(ms + EPS) * g)
    return outs[0] if len(outs) == 1 else jnp.concatenate(outs, axis=1)


def _rope(y, cos, sin):
    rows = y.shape[0]
    lane = lax.broadcasted_iota(I32, (rows, LANES), 1)
    even = (lane % 2) == 0
    outs = []
    for c in range(y.shape[1] // LANES):
        ch = y[:, c * LANES:(c + 1) * LANES]
        partner = jnp.where(even, pltpu.roll(ch, LANES - 1, 1), pltpu.roll(ch, 1, 1))
        outs.append(ch * cos + partner * sin)
    return outs[0] if len(outs) == 1 else jnp.concatenate(outs, axis=1)


def _proj_kernel(x_ref, ctx_ref, mod_ref, gn_ref, w_ref, cos_ref, sin_ref, qna_ref, kna_ref, qnb_ref, knb_ref,
                 gmat_ref, qat_ref, ka_ref, vat_ref, qbt_ref, kb_ref, vbt_ref):
    b = pl.program_id(0)
    j = pl.program_id(1)
    is_ctx = j == 0
    xt = jnp.where(is_ctx, ctx_ref[0], x_ref[0])
    mrow = jnp.where(is_ctx, pl.num_programs(0), b)
    mod = mod_ref[mrow]
    h = _rms_rows(xt, gn_ref[...], EPS) * (1.0 + mod[1:2, :]) + mod[0:1, :]
    y = _dot(h.astype(BF16), w_ref[...])
    cos = jnp.where(is_ctx, 1.0, cos_ref[...])
    sin = jnp.where(is_ctx, 0.0, sin_ref[...])
    gmat = gmat_ref[...]
    scale = HEAD_DIM ** -0.5 * math.log2(math.e)

    o = 0
    qa = _rope(_group_rms(y[:, o:o + A_Q_COLS], gmat, qna_ref[...]), cos, sin) * scale
    o += A_Q_COLS
    ka = _rope(_group_rms(y[:, o:o + A_KV_COLS], gmat, kna_ref[...]), cos, sin)
    o += A_KV_COLS
    va = y[:, o:o + A_KV_COLS]
    o += A_KV_COLS
    qb = _rope(_group_rms(y[:, o:o + B_QK_COLS], gmat, qnb_ref[...]), cos, sin) * scale
    o += B_QK_COLS
    kb = _rope(_group_rms(y[:, o:o + B_QK_COLS], gmat, knb_ref[...]), cos, sin)
    o += B_QK_COLS
    vb = y[:, o:o + B_WIDTH]

    @pl.when(j > 0)
    def _():
        qat_ref[0] = qa.T.astype(BF16)
        qbt_ref[0] = qb.T.astype(BF16)

    ka_ref[0] = ka.astype(BF16)
    kb_ref[0] = kb.astype(BF16)
    ones = jnp.ones((ONES_ROWS, xt.shape[0]), BF16)
    for dst_ref, vt, width in ((vat_ref, va.T.astype(BF16), HEAD_DIM), (vbt_ref, vb.T.astype(BF16), B_V_DIM)):
        stride = width + ONES_ROWS
        for hd in range(vt.shape[0] // width):
            dst_ref[0, 0, hd * stride:hd * stride + width, :] = vt[hd * width:(hd + 1) * width, :]
            dst_ref[0, 0, hd * stride + width:(hd + 1) * stride, :] = ones


def _project(x, ctx, mod, gn, w_qkv, cos, sin, qna, kna, qnb, knb, gmat):
    bsz, s, d = x.shape
    n_lat = s // TOK_TILE
    n_key = n_lat + 1
    lk = n_key * TOK_TILE
    lat = lambda b, j: (b, jnp.maximum(j - 1, 0), 0)
    const2 = lambda b, j: (0, 0)
    tab = lambda b, j: (jnp.maximum(j - 1, 0), 0)
    return pl.pallas_call(
        _proj_kernel,
        grid=(bsz, n_key),
        in_specs=[
            pl.BlockSpec((1, TOK_TILE, d), lat),
            pl.BlockSpec((1, TOK_TILE, d), lambda b, j: (b, 0, 0)),
            pl.BlockSpec(mod.shape, lambda b, j: (0, 0, 0)),
            pl.BlockSpec((1, d), const2),
            pl.BlockSpec((d, QKV_COLS), const2),
            pl.BlockSpec((TOK_TILE, LANES), tab),
            pl.BlockSpec((TOK_TILE, LANES), tab),
            pl.BlockSpec((1, LANES), const2),
            pl.BlockSpec((1, LANES), const2),
            pl.BlockSpec((1, LANES), const2),
            pl.BlockSpec((1, LANES), const2),
            pl.BlockSpec((MXU_DIM, MXU_DIM), const2),
        ],
        out_specs=[
            pl.BlockSpec((1, A_Q_COLS, TOK_TILE), lambda b, j: (b, 0, jnp.maximum(j - 1, 0))),
            pl.BlockSpec((1, TOK_TILE, A_KV_COLS), lambda b, j: (b, j, 0)),
            pl.BlockSpec((1, 1, A_KV_HEADS * A_VROWS, TOK_TILE), lambda b, j: (b, j, 0, 0)),
            pl.BlockSpec((1, B_QK_COLS, TOK_TILE), lambda b, j: (b, 0, jnp.maximum(j - 1, 0))),
            pl.BlockSpec((1, TOK_TILE, B_QK_COLS), lambda b, j: (b, j, 0)),
            pl.BlockSpec((1, 1, B_HEADS * B_VROWS, TOK_TILE), lambda b, j: (b, j, 0, 0)),
        ],
        out_shape=[
            jax.ShapeDtypeStruct((bsz, A_Q_COLS, s), BF16),
            jax.ShapeDtypeStruct((bsz, lk, A_KV_COLS), BF16),
            jax.ShapeDtypeStruct((bsz, n_key, A_KV_HEADS * A_VROWS, TOK_TILE), BF16),
            jax.ShapeDtypeStruct((bsz, B_QK_COLS, s), BF16),
            jax.ShapeDtypeStruct((bsz, lk, B_QK_COLS), BF16),
            jax.ShapeDtypeStruct((bsz, n_key, B_HEADS * B_VROWS, TOK_TILE), BF16),
        ],
        compiler_params=pltpu.CompilerParams(vmem_limit_bytes=VMEM_LIMIT),
        name="proj",
    )(x, ctx, mod, gn, w_qkv, cos, sin, qna, kna, qnb, knb, gmat)


def _flash(k_ref, vt_ref, wq, pv, dv, s_even, s_odd):
    n = wq.shape[1]
    n_tiles = n // MXU_DIM
    n_chunks = vt_ref.shape[1]
    cols = [slice(t * MXU_DIM, (t + 1) * MXU_DIM) for t in range(n_tiles)]
    wqs = [wq[:, cs] for cs in cols]

    def scores(c, t, dst):
        start = pl.multiple_of(c * TOK_TILE, TOK_TILE)
        dst[:, cols[t]] = _dot(k_ref[0, pl.ds(start, TOK_TILE), :], wqs[t])

    def chunk(c, cur, nxt, state):
        out = []
        for t in range(n_tiles):
            if nxt is not None:
                scores(c + 1, t, nxt)
            m, acc = state[t]
            s = cur[:, cols[t]]
            m_new = jnp.maximum(m, jnp.max(s, axis=0, keepdims=True))
            alpha = jnp.exp2(m - m_new)
            p = jnp.exp2(s - m_new).astype(BF16)
            out.append((m_new, alpha * acc + pv(t, vt_ref[0, c], p)))
        return tuple(out)

    for t in range(n_tiles):
        scores(0, t, s_even)

    bufs = (s_even, s_odd)

    per_trip = SCORE_TILES_PER_TRIP // n_tiles
    assert per_trip % 2 == 0, "the two score buffers alternate, so a trip must hold an even number of chunks"

    def trip(i, state):
        for u in range(per_trip):
            state = chunk(i * per_trip + u, bufs[u % 2], bufs[(u + 1) % 2], state)
        return state

    init = tuple((jnp.full((1, MXU_DIM), -jnp.inf, F32), jnp.zeros((dv + ONES_ROWS, MXU_DIM), F32))
                 for _ in range(n_tiles))
    n_trips = (n_chunks - 1) // per_trip
    state = lax.fori_loop(0, n_trips, trip, init)
    for c in range(n_trips * per_trip, n_chunks):
        state = chunk(c, bufs[c % 2], bufs[(c + 1) % 2] if c + 1 < n_chunks else None, state)
    return jnp.concatenate([acc for _, acc in state], axis=1)


def _attn_a_kernel(qt_ref, k_ref, vt_ref, o_ref, s_even, s_odd):
    tq = qt_ref.shape[2]
    half = A_GROUP * tq
    zeros = jnp.zeros((HEAD_DIM, half), BF16)
    rows = []
    for g in range(A_KV_HEADS):
        heads = range(g * A_GROUP, (g + 1) * A_GROUP)
        qg = jnp.concatenate([qt_ref[0, h * HEAD_DIM:(h + 1) * HEAD_DIM, :] for h in heads], axis=1)
        rows.append(jnp.concatenate([qg if gg == g else zeros for gg in range(A_KV_HEADS)], axis=1))
    wq = jnp.concatenate(rows, axis=0)

    def pv(t, vc, p):
        g = t * MXU_DIM // half
        return _dot(vc[g * A_VROWS:(g + 1) * A_VROWS], p)

    acc = _flash(k_ref, vt_ref, wq, pv, HEAD_DIM, s_even, s_odd)
    o = acc[:HEAD_DIM] / acc[HEAD_DIM:HEAD_DIM + 1]
    outs = [o[:, h * tq:(h + 1) * tq].T for h in range(A_HEADS)]
    o_ref[0] = jnp.concatenate(outs, axis=1).astype(o_ref.dtype)


def _attn_a(qat, ka, vat):
    bsz, _, s = qat.shape
    lk = ka.shape[1]
    n_chunks = vat.shape[1]
    return pl.pallas_call(
        _attn_a_kernel,
        grid=(bsz, s // TQ_A),
        in_specs=[
            pl.BlockSpec((1, A_Q_COLS, TQ_A), lambda b, i: (b, 0, i)),
            pl.BlockSpec((1, lk, A_KV_COLS), lambda b, i: (b, 0, 0)),
            pl.BlockSpec((1, n_chunks, A_KV_HEADS * A_VROWS, TOK_TILE), lambda b, i: (b, 0, 0, 0)),
        ],
        out_specs=pl.BlockSpec((1, TQ_A, A_Q_COLS), lambda b, i: (b, i, 0)),
        out_shape=jax.ShapeDtypeStruct((bsz, s, A_Q_COLS), BF16),
        scratch_shapes=[pltpu.VMEM((TOK_TILE, A_HEADS * TQ_A), F32)] * 2,
        compiler_params=pltpu.CompilerParams(vmem_limit_bytes=VMEM_LIMIT),
        name="attn_a",
    )(qat, ka, vat)


def _attn_b_kernel(qt_ref, k_ref, vt_ref, lq1_ref, lk1_ref, lq2_ref, lk2_ref, sub_ref, o_ref, s_even, s_odd):
    tq = qt_ref.shape[2]
    zeros = jnp.zeros((HEAD_DIM, tq), BF16)
    q1 = qt_ref[0, :HEAD_DIM, :]
    q2 = qt_ref[0, HEAD_DIM:, :]
    wq = jnp.concatenate([jnp.concatenate([q1, zeros], axis=1), jnp.concatenate([zeros, q2], axis=1)], axis=0)
    acc = _flash(k_ref, vt_ref, wq, lambda t, vc, p: _dot(vc, p), B_V_DIM, s_even, s_odd)
    r = acc[:B_V_DIM] / acc[B_V_DIM:B_V_DIM + 1]
    lam = (jnp.exp(jnp.sum(lq1_ref[...] * lk1_ref[...], axis=1, keepdims=True))
           - jnp.exp(jnp.sum(lq2_ref[...] * lk2_ref[...], axis=1, keepdims=True)) + LAM_INIT)
    o = r[:, :tq] - lam * r[:, tq:]
    ms = jnp.mean(o * o, axis=0, keepdims=True)
    o = o * lax.rsqrt(ms + SUBLN_EPS) * sub_ref[...] * (1.0 - LAM_INIT)
    o_ref[0] = o.T.astype(o_ref.dtype)


def _attn_b(qbt, kb, vbt, lq1, lk1, lq2, lk2, subln):
    bsz, _, s = qbt.shape
    lk = kb.shape[1]
    n_chunks = vbt.shape[1]
    vec = pl.BlockSpec((1, HEAD_DIM), lambda b, h, i: (0, 0))
    return pl.pallas_call(
        _attn_b_kernel,
        grid=(bsz, B_HEADS, s // TQ_B),
        in_specs=[
            pl.BlockSpec((1, 2 * HEAD_DIM, TQ_B), lambda b, h, i: (b, h, i)),
            pl.BlockSpec((1, lk, 2 * HEAD_DIM), lambda b, h, i: (b, 0, h)),
            pl.BlockSpec((1, n_chunks, B_VROWS, TOK_TILE), lambda b, h, i: (b, 0, h, 0)),
            vec, vec, vec, vec,
            pl.BlockSpec((B_V_DIM, 1), lambda b, h, i: (0, 0)),
        ],
        out_specs=pl.BlockSpec((1, TQ_B, B_V_DIM), lambda b, h, i: (b, i, h)),
        out_shape=jax.ShapeDtypeStruct((bsz, s, B_WIDTH), BF16),
        scratch_shapes=[pltpu.VMEM((TOK_TILE, 2 * TQ_B), F32)] * 2,
        compiler_params=pltpu.CompilerParams(vmem_limit_bytes=VMEM_LIMIT),
        name="attn_b",
    )(qbt, kb, vbt, lq1, lk1, lq2, lk2, subln)


def _merge_kernel(tiles_per_sample, x_ref, ya_ref, yb_ref, mod_ref, gna_ref, gnm_ref, wg_ref, woa_ref, wob_ref,
                  wout_ref, rw_ref, rb_ref, utri_ref, x1_ref, h2_ref, route_ref, cnt_ref, carry_ref):
    i = pl.program_id(0)

    @pl.when(i == 0)
    def _():
        carry_ref[...] = jnp.zeros_like(carry_ref)

    mod = mod_ref[i // tiles_per_sample]
    d = x_ref.shape[1]
    subs = [slice(u * TOK_TILE, (u + 1) * TOK_TILE) for u in range(x_ref.shape[0] // TOK_TILE)]
    xs = [x_ref[r, :] for r in subs]
    hs = [(_rms_rows(x, gna_ref[...], EPS) * (1.0 + mod[1:2, :]) + mod[0:1, :]).astype(BF16) for x in xs]
    gates = [_dot(h, wg_ref[...]) for h in hs]
    branch = [(_dot(ya_ref[r, :], woa_ref[...]), _dot(yb_ref[r, :], wob_ref[...])) for r in subs]
    merged = [(jax.nn.sigmoid(g[:, :d]) * pa + jax.nn.sigmoid(g[:, d:]) * pb).astype(BF16)
              for g, (pa, pb) in zip(gates, branch)]
    x1s = [x + mod[2:3, :] * _dot(m, wout_ref[...]) for x, m in zip(xs, merged)]
    h2s = [_rms_rows(x1, gnm_ref[...], EPS) * (1.0 + mod[4:5, :]) + mod[3:4, :] for x1 in x1s]
    for r, x1, h2 in zip(subs, x1s, h2s):
        x1_ref[r, :] = x1
        h2_ref[r, :] = h2

    all_logits = [_dot3_nt(rw_ref[...], h2) + rb_ref[...] for h2 in h2s]
    eiota = lax.broadcasted_iota(I32, (N_EXPERTS, TOK_TILE), 0)
    riota = lax.broadcasted_iota(I32, (ROUTE_ROWS, TOK_TILE), 0)
    carry = carry_ref[...]
    for r, logits in zip(subs, all_logits):
        work = logits
        idxs, vals = [], []
        for _ in range(TOP_K):
            mx = jnp.max(work, axis=0, keepdims=True)
            idx = jnp.min(jnp.where(work == mx, eiota, N_EXPERTS), axis=0, keepdims=True)
            idxs.append(idx)
            vals.append(mx)
            work = jnp.where(eiota == idx, -jnp.inf, work)
        exps = [jnp.exp(v - vals[0]) for v in vals]
        denom = exps[0] + exps[1] + exps[2] + exps[3]
        weights = [e / denom for e in exps]

        hits = [(eiota == idx) for idx in idxs]
        onehot = (hits[0] | hits[1] | hits[2] | hits[3]).astype(F32)
        before = _dot(onehot.astype(BF16), utri_ref[...]) + carry
        ranks = [jnp.sum(jnp.where(hit, before, 0.0), axis=0, keepdims=True) for hit in hits]
        carry = carry + jnp.sum(onehot, axis=1, keepdims=True)

        lines = [v.astype(F32) for v in idxs] + weights + ranks
        route = jnp.zeros((ROUTE_ROWS, TOK_TILE), F32)
        for k, line in enumerate(lines):
            route = jnp.where(riota == k, line, route)
        route_ref[:, r] = route
    carry_ref[...] = carry
    cnt_ref[...] = carry


def _merge(x2, ya2, yb2, mod, gna, gnm, wg, woa, wob, wout, rw, rb, utri, tiles_per_sample):
    t, d = x2.shape
    const2 = lambda i: (0, 0)
    row = lambda i: (i, 0)
    full = lambda a: pl.BlockSpec(a.shape, const2)
    return pl.pallas_call(
        functools.partial(_merge_kernel, tiles_per_sample),
        grid=(t // MERGE_TOK,),
        in_specs=[
            pl.BlockSpec((MERGE_TOK, d), row),
            pl.BlockSpec((MERGE_TOK, ya2.shape[1]), row),
            pl.BlockSpec((MERGE_TOK, yb2.shape[1]), row),
            pl.BlockSpec(mod.shape, lambda i: (0, 0, 0)),
            full(gna), full(gnm), full(wg), full(woa), full(wob), full(wout), full(rw), full(rb), full(utri),
        ],
        out_specs=[
            pl.BlockSpec((MERGE_TOK, d), row),
            pl.BlockSpec((MERGE_TOK, d), row),
            pl.BlockSpec((ROUTE_ROWS, MERGE_TOK), lambda i: (0, i)),
            pl.BlockSpec((N_EXPERTS, 1), const2),
        ],
        out_shape=[
            jax.ShapeDtypeStruct((t, d), F32),
            jax.ShapeDtypeStruct((t, d), F32),
            jax.ShapeDtypeStruct((ROUTE_ROWS, t), F32),
            jax.ShapeDtypeStruct((N_EXPERTS, 1), F32),
        ],
        scratch_shapes=[pltpu.VMEM((N_EXPERTS, 1), F32)],
        compiler_params=pltpu.CompilerParams(vmem_limit_bytes=VMEM_LIMIT),
        name="merge",
    )(x2, ya2, yb2, mod, gna, gnm, wg, woa, wob, wout, rw, rb, utri)


def _to_row_tiles(dst_ref, val):
    n, d = val.shape
    sub = d // LANES
    for c in range(sub):
        dst_ref[pl.ds(c, n, stride=sub), :] = val[:, c * LANES:(c + 1) * LANES]


def _from_row_tiles(src_ref, first_row, n, d):
    sub = d // LANES
    return jnp.concatenate([src_ref[pl.ds(first_row * sub + c, n, stride=sub), :] for c in range(sub)], axis=1)


def _row_tile_copy(src, src_row, dst, dst_row, sub, sem):
    src_at = pl.ds(pl.multiple_of(src_row * sub, sub), sub)
    dst_at = pl.ds(pl.multiple_of(dst_row * sub, sub), sub)
    return pltpu.make_async_copy(src.at[src_at], dst.at[dst_at], sem)


def _dispatch_kernel(pos_ref, padfrom_ref, padcnt_ref, nu_ref, h_ref, xb_out, tiles, zeros, sem, zsem):
    i = pl.program_id(0)
    n = pl.num_programs(0)
    slot = i % 2
    n_tok, d = h_ref.shape
    sub = d // LANES
    blk = zeros.shape[0]
    n_blocks = xb_out.shape[0] // blk

    def zero_fill(wait):
        def go(copy):
            copy.wait() if wait else copy.start()

        def per_expert(e, carry):
            cnt = padcnt_ref[e]
            for bit in reversed(range((FFN_ROWS - 1).bit_length())):
                rows = 1 << bit

                @pl.when((cnt & rows) != 0)
                def _():
                    done = (cnt >> (bit + 1)) << (bit + 1)
                    at = pl.multiple_of((padfrom_ref[e] + done) * sub, sub)
                    go(pltpu.make_async_copy(zeros.at[pl.ds(0, rows * sub)], xb_out.at[pl.ds(at, rows * sub)], zsem))
            return carry

        lax.fori_loop(0, N_EXPERTS, per_expert, 0)

        def per_block(j, carry):
            go(pltpu.make_async_copy(zeros, xb_out.at[pl.ds(pl.multiple_of(j * blk, blk), blk)], zsem))
            return carry

        lax.fori_loop(nu_ref[0], n_blocks, per_block, 0)

    @pl.when(i == 0)
    def _():
        zeros[...] = jnp.zeros_like(zeros)
        zero_fill(wait=False)
        zero_fill(wait=True)

    def drain(sl):
        for _ in range(TOP_K):
            pltpu.make_async_copy(tiles.at[sl], xb_out.at[pl.ds(0, n_tok * sub)], sem.at[sl]).wait()

    @pl.when(i >= 2)
    def _():
        drain(slot)

    _to_row_tiles(tiles.at[slot], h_ref[...])
    base = i * n_tok

    def issue(t, carry):
        for k in range(TOP_K):
            _row_tile_copy(tiles.at[slot], t, xb_out, pos_ref[(base + t) * TOP_K + k], sub,
                           sem.at[slot]).start(priority=k % DMA_QUEUES)
        return carry

    lax.fori_loop(0, n_tok, issue, 0)

    @pl.when((i == n - 1) & (n >= 2))
    def _():
        drain(1 - slot)

    @pl.when(i == n - 1)
    def _():
        drain(slot)


def _dispatch(pos_flat, pad_from, pad_cnt, n_used, h2, n_blocks):
    t, d = h2.shape
    sub = d // LANES
    return pl.pallas_call(
        _dispatch_kernel,
        grid_spec=pltpu.PrefetchScalarGridSpec(
            num_scalar_prefetch=4,
            grid=(t // TOK_TILE,),
            in_specs=[pl.BlockSpec((TOK_TILE, d), lambda i, *_: (i, 0))],
            out_specs=pl.BlockSpec(memory_space=pl.ANY),
            scratch_shapes=[pltpu.VMEM((2, TOK_TILE * sub, LANES), F32), pltpu.VMEM((FFN_ROWS * sub, LANES), F32),
                            pltpu.SemaphoreType.DMA((2,)), pltpu.SemaphoreType.DMA(())],
        ),
        out_shape=jax.ShapeDtypeStruct((n_blocks * FFN_ROWS * sub, LANES), F32),
        compiler_params=pltpu.CompilerParams(vmem_limit_bytes=VMEM_LIMIT),
        name="dispatch",
    )(pos_flat, pad_from, pad_cnt, n_used, h2)


def _ffn_kernel(be_ref, nu_ref, nxt_ref, x_ref, wg_hbm, bg_ref, wu_hbm, bu_ref, wd_hbm, bd_ref, o_ref,
                stage_g, stage_u, stage_d, wg_s, wu_s, wd_s, wsem):
    j = pl.program_id(0)
    prev = be_ref[jnp.maximum(j - 1, 0)]
    fresh = ((j == 0) | (be_ref[j] != prev)) & (j < nu_ref[0])
    streams = ((wg_hbm, stage_g, wg_s), (wu_hbm, stage_u, wu_s), (wd_hbm, stage_d, wd_s))

    def weight_copies(expert):
        return [pltpu.make_async_copy(hbm.at[expert], stage, wsem.at[n]) for n, (hbm, stage, _) in enumerate(streams)]

    @pl.when(j == 0)
    def _():
        for copy in weight_copies(be_ref[0]):
            copy.start()

    @pl.when(fresh)
    def _():
        for copy, (_, stage, dst) in zip(weight_copies(be_ref[j]), streams):
            copy.wait()
            dst[...] = stage[...].astype(BF16)

        @pl.when(nxt_ref[j] >= 0)
        def _():
            for copy in weight_copies(nxt_ref[j]):
                copy.start()

    @pl.when(j < nu_ref[0])
    def _():
        x = _from_row_tiles(x_ref, 0, FFN_ROWS, wg_s.shape[0]).astype(BF16)
        gate = jnp.minimum(_dot(x, wg_s[...]) + bg_ref[0], SWIGLU_LIMIT)
        up = jnp.clip(_dot(x, wu_s[...]) + bu_ref[0], -SWIGLU_LIMIT, SWIGLU_LIMIT)
        act = (up + 1.0) * (gate * jax.nn.sigmoid(SWIGLU_ALPHA * gate))
        _to_row_tiles(o_ref, _dot(act.astype(BF16), wd_s[...]) + bd_ref[0])

    @pl.when(j >= nu_ref[0])
    def _():
        o_ref[...] = jnp.zeros_like(o_ref)


def _ffn(block_e, n_used, next_e, xb, w_gate, b_gate, w_up, b_up, w_down, b_down):
    d, de = w_gate.shape[1:]
    blk = FFN_ROWS * d // LANES
    n_blocks = xb.shape[0] // blk
    xrow = lambda j, be, nu, nxt: (jnp.minimum(j, nu[0] - 1), 0)
    bsel = lambda j, be, nu, nxt: (be[j], 0, 0)
    any_spec = pl.BlockSpec(memory_space=pl.ANY)
    return pl.pallas_call(
        _ffn_kernel,
        grid_spec=pltpu.PrefetchScalarGridSpec(
            num_scalar_prefetch=3,
            grid=(n_blocks,),
            in_specs=[
                pl.BlockSpec((blk, LANES), xrow),
                any_spec,
                pl.BlockSpec((1, 1, de), bsel),
                any_spec,
                pl.BlockSpec((1, 1, de), bsel),
                any_spec,
                pl.BlockSpec((1, 1, d), bsel),
            ],
            out_specs=pl.BlockSpec((blk, LANES), lambda j, be, nu, nxt: (j, 0)),
            scratch_shapes=[pltpu.VMEM((d, de), F32), pltpu.VMEM((d, de), F32), pltpu.VMEM((de, d), F32),
                            pltpu.VMEM((d, de), BF16), pltpu.VMEM((d, de), BF16), pltpu.VMEM((de, d), BF16),
                            pltpu.SemaphoreType.DMA((3,))],
        ),
        out_shape=jax.ShapeDtypeStruct(xb.shape, F32),
        compiler_params=pltpu.CompilerParams(vmem_limit_bytes=VMEM_LIMIT),
        name="ffn",
    )(block_e, n_used, next_e, xb, w_gate, b_gate, w_up, b_up, w_down, b_down)


def _combine_kernel(tiles_per_sample, pos_ref, x1_ref, route_ref, mod_ref, yb_hbm, o_ref, buf, sem):
    i = pl.program_id(0)
    n = pl.num_programs(0)
    rows = TOP_K * CMB_TOK
    d = x1_ref.shape[1]
    sub = d // LANES

    def issue(tile, slot):
        def body(r2, carry):
            for u in range(DMA_QUEUES):
                r = r2 * DMA_QUEUES + u
                _row_tile_copy(yb_hbm, pos_ref[tile * rows + r], buf.at[slot], r, sub, sem.at[slot]).start(priority=u)
            return carry
        lax.fori_loop(0, rows // DMA_QUEUES, body, 0)

    @pl.when(i == 0)
    def _():
        issue(0, 0)

    @pl.when(i + 1 < n)
    def _():
        issue(i + 1, (i + 1) % 2)

    slot = i % 2
    pltpu.make_async_copy(yb_hbm.at[pl.ds(0, rows * sub)], buf.at[slot], sem.at[slot]).wait()

    b = i // tiles_per_sample
    g_m = mod_ref[b][5:6, :]
    route = route_ref[...]
    acc = jnp.zeros(x1_ref.shape, F32)
    for k in range(TOP_K):
        acc = acc + route[:, TOP_K + k:TOP_K + k + 1] * _from_row_tiles(buf.at[slot], k * CMB_TOK, CMB_TOK, d)
    o_ref[...] = x1_ref[...] + g_m * acc


def _combine(pos_tiles, x1, route, mod, yb, tiles_per_sample):
    t, d = x1.shape
    row = lambda i, pos: (i, 0)
    return pl.pallas_call(
        functools.partial(_combine_kernel, tiles_per_sample),
        grid_spec=pltpu.PrefetchScalarGridSpec(
            num_scalar_prefetch=1,
            grid=(t // CMB_TOK,),
            in_specs=[
                pl.BlockSpec((CMB_TOK, d), row),
                pl.BlockSpec((CMB_TOK, route.shape[1]), row),
                pl.BlockSpec(mod.shape, lambda i, pos: (0, 0, 0)),
                pl.BlockSpec(memory_space=pl.ANY),
            ],
            out_specs=pl.BlockSpec((CMB_TOK, d), row),
            scratch_shapes=[pltpu.VMEM((2, TOP_K * CMB_TOK * d // LANES, LANES), F32),
                            pltpu.SemaphoreType.DMA((2,))],
        ),
        out_shape=jax.ShapeDtypeStruct((t, d), F32),
        compiler_params=pltpu.CompilerParams(vmem_limit_bytes=VMEM_LIMIT),
        name="combine",
    )(pos_tiles, x1, route, mod, yb)


def _rope_tables(s):
    rows = s // GRID_W
    row = jnp.repeat(jnp.arange(rows, dtype=F32), GRID_W)
    col = jnp.tile(jnp.arange(GRID_W, dtype=F32), rows)
    inv = 1.0 / (ROPE_THETA ** (jnp.arange(AXIS_PAIRS, dtype=F32) / AXIS_PAIRS))
    ang = jnp.concatenate([row[:, None] * inv, col[:, None] * inv], axis=-1)
    cos, sin = jnp.cos(ang), jnp.sin(ang)
    cos_i = jnp.repeat(cos, 2, axis=-1)
    sin_i = jnp.stack([-sin, sin], axis=-1).reshape(s, HEAD_DIM)
    reps = LANES // HEAD_DIM
    return jnp.tile(cos_i, (1, reps)), jnp.tile(sin_i, (1, reps))


def kernel(x, c, ctx, c_ctx, w_ada, b_ada, norm_attn, w_in, q_norm_a, k_norm_a, q_norm_b, k_norm_b, lambda_q1,
           lambda_k1, lambda_q2, lambda_k2, subln_b, w_oa, w_ob, w_out, norm_mlp, router_w, router_b, w_gate,
           b_gate, w_up, b_up, w_down, b_down):
    bsz, s, d = x.shape
    assert ctx.shape[1] == TOK_TILE and s % MERGE_TOK == 0 and s % TQ_A == 0 and s % TQ_B == 0 and s % GRID_W == 0
    assert w_ada.shape[0] == 1, "single-layer block"
    t = bsz * s
    assert t % CMB_TOK == 0 and (t * TOP_K) % FFN_ROWS == 0

    mod_rows = 8
    assert bsz + 1 <= mod_rows
    cvec = jnp.concatenate([c, c_ctx[None, :], jnp.zeros((mod_rows - bsz - 1, d), F32)], axis=0)
    mod = _adaln(cvec, w_ada[0], b_ada[0][None, :]).reshape(mod_rows, N_MOD, d)

    cos, sin = _rope_tables(s)
    tile_gain = lambda g: jnp.tile(g, LANES // HEAD_DIM)[None, :]
    gidx = jnp.arange(MXU_DIM) // HEAD_DIM
    gmat = jnp.where(gidx[:, None] == gidx[None, :], 1.0 / HEAD_DIM, 0.0).astype(BF16)
    w_qkv = w_in[0][:, :QKV_COLS].astype(BF16)
    w_g = w_in[0][:, QKV_COLS:].astype(BF16)

    qat, ka, vat, qbt, kb, vbt = _project(
        x, ctx, mod, norm_attn, w_qkv, cos, sin, tile_gain(q_norm_a[0]), tile_gain(k_norm_a[0]),
        tile_gain(q_norm_b[0]), tile_gain(k_norm_b[0]), gmat)

    ya = _attn_a(qat, ka, vat)
    yb = _attn_b(qbt, kb, vbt, lambda_q1, lambda_k1, lambda_q2, lambda_k2, subln_b[0][:, None])

    tri = jnp.arange(TOK_TILE)
    utri = (tri[:, None] < tri[None, :]).astype(BF16)
    x1, h2, route_t, counts = _merge(
        x.reshape(t, d), ya.reshape(t, A_Q_COLS), yb.reshape(t, B_WIDTH), mod, norm_attn, norm_mlp, w_g,
        w_oa[0].astype(BF16), w_ob[0].astype(BF16), w_out[0].astype(BF16), router_w[0].T, router_b[0][:, None], utri,
        s // MERGE_TOK)

    counts = counts[:, 0].astype(I32)
    padded = (counts + FFN_ROWS - 1) // FFN_ROWS * FFN_ROWS
    pad_end = jnp.cumsum(padded)
    pad_start = pad_end - padded
    top_e = route_t[:TOP_K].astype(I32)
    rank = route_t[2 * TOP_K:3 * TOP_K].astype(I32)
    experts = jnp.arange(N_EXPERTS, dtype=I32)
    start_of = jnp.sum(jnp.where(top_e[..., None] == experts, pad_start, 0), axis=-1)
    pos_t = start_of + rank
    n_blocks = t * TOP_K // FFN_ROWS + N_EXPERTS
    block_start = jnp.arange(n_blocks, dtype=I32) * FFN_ROWS
    block_e = jnp.minimum(jnp.sum((pad_end[None, :] <= block_start[:, None]).astype(I32), axis=1), N_EXPERTS - 1)
    n_used = (pad_end[-1:] // FFN_ROWS).astype(I32)
    first_at_or_after = lax.cummin(jnp.where(padded > 0, jnp.arange(N_EXPERTS, dtype=I32), N_EXPERTS), reverse=True)
    next_nonempty = jnp.concatenate([first_at_or_after[1:], jnp.full((1,), N_EXPERTS, I32)])
    next_of = jnp.where(next_nonempty < N_EXPERTS, next_nonempty, -1)
    next_e = jnp.sum(jnp.where(block_e[:, None] == experts, next_of, 0), axis=-1)

    xb = _dispatch(pos_t.T.reshape(-1), pad_start + counts, padded - counts, n_used, h2, n_blocks)
    de = w_gate.shape[-1]
    yb_rows = _ffn(block_e, n_used, next_e, xb, w_gate[0], b_gate[0].reshape(N_EXPERTS, 1, de), w_up[0],
                   b_up[0].reshape(N_EXPERTS, 1, de), w_down[0], b_down[0].reshape(N_EXPERTS, 1, d))
    pos_tiles = pos_t.reshape(TOP_K, t // CMB_TOK, CMB_TOK).transpose(1, 0, 2).reshape(-1)
    out = _combine(pos_tiles, x1, route_t.T, mod, yb_rows, s // CMB_TOK)
    return out.reshape(bsz, s, d)
```

```python
import functools
import math

import jax
import jax.numpy as jnp
from jax import lax
from jax.experimental import pallas as pl
from jax.experimental.pallas import tpu as pltpu

F32 = jnp.float32
BF16 = jnp.bfloat16
I32 = jnp.int32

GRID_W = 64
HEAD_DIM = 64
A_HEADS = 8
A_KV_HEADS = 2
A_GROUP = A_HEADS // A_KV_HEADS
B_HEADS = 4
B_V_DIM = 2 * HEAD_DIM
N_EXPERTS = 32
TOP_K = 4
N_MOD = 6
ROPE_THETA = 10000.0
AXIS_PAIRS = HEAD_DIM // 4
SWIGLU_LIMIT = 7.0
SWIGLU_ALPHA = 1.702
EPS = 1e-6
SUBLN_EPS = 1e-5
LAM_INIT = 0.8 - 0.6 * math.exp(-0.3 * 0)

A_Q_COLS = A_HEADS * HEAD_DIM
A_KV_COLS = A_KV_HEADS * HEAD_DIM
B_QK_COLS = B_HEADS * 2 * HEAD_DIM
B_WIDTH = B_HEADS * B_V_DIM
QKV_COLS = A_Q_COLS + 2 * A_KV_COLS + 2 * B_QK_COLS + B_WIDTH

LANES = 128
MXU_DIM = 256
TOK_TILE = 256
MERGE_TOK = 4 * TOK_TILE
TQ_A = 256
TQ_B = 512
SCORE_TILES_PER_TRIP = 128
ONES_ROWS = 16
A_VROWS = HEAD_DIM + ONES_ROWS
B_VROWS = B_V_DIM + ONES_ROWS
ADA_COLS = 1536
FFN_ROWS = 256
CMB_TOK = 128
ROUTE_ROWS = 16
ISSUE_UNROLL = 8
DMA_QUEUES = 2
VMEM_LIMIT = 56 * 1024 * 1024


def _dot(a, b):
    return jnp.dot(a, b, preferred_element_type=F32)


def _split_bf16(a):
    hi = a.astype(BF16)
    lo = (a - hi.astype(F32)).astype(BF16)
    return hi, lo


def _dot3(a, b):
    a_hi, a_lo = _split_bf16(a)
    b_hi, b_lo = _split_bf16(b)
    return _dot(a_hi, b_hi) + (_dot(a_hi, b_lo) + _dot(a_lo, b_hi))


def _dot3_nt(a, b):
    dims = (((1,), (1,)), ((), ()))
    dot_nt = lambda u, v: lax.dot_general(u, v, dims, preferred_element_type=F32)
    a_hi, a_lo = _split_bf16(a)
    b_hi, b_lo = _split_bf16(b)
    return dot_nt(a_hi, b_hi) + (dot_nt(a_hi, b_lo) + dot_nt(a_lo, b_hi))


def _rms_rows(x, gain, eps):
    ms = jnp.mean(x * x, axis=-1, keepdims=True)
    return x * lax.rsqrt(ms + eps) * gain


def _adaln_kernel(c_ref, w_ref, b_ref, o_ref):
    c = c_ref[...]
    a = c * jax.nn.sigmoid(c)
    o_ref[...] = _dot3(a, w_ref[...]) + b_ref[...]


def _adaln(cvec, w_ada, b_ada):
    rows, d = cvec.shape
    n = w_ada.shape[1]
    return pl.pallas_call(
        _adaln_kernel,
        grid=(n // ADA_COLS,),
        in_specs=[
            pl.BlockSpec((rows, d), lambda j: (0, 0)),
            pl.BlockSpec((d, ADA_COLS), lambda j: (0, j)),
            pl.BlockSpec((1, ADA_COLS), lambda j: (0, j)),
        ],
        out_specs=pl.BlockSpec((rows, ADA_COLS), lambda j: (0, j)),
        out_shape=jax.ShapeDtypeStruct((rows, n), F32),
        compiler_params=pltpu.CompilerParams(vmem_limit_bytes=VMEM_LIMIT),
        name="adaln",
    )(cvec, w_ada, b_ada)


def _group_rms(y, gmat, gain):
    w_total = y.shape[1]
    outs = []
    for c0 in range(0, w_total, MXU_DIM):
        w = min(MXU_DIM, w_total - c0)
        yc = y[:, c0:c0 + w]
        ms = _dot((yc * yc).astype(BF16), gmat[:w, :w])
        reps = w // LANES
        g = gain if reps == 1 else jnp.concatenate([gain] * reps, axis=1)
        outs.append(yc * lax.rsqrt(ms + EPS) * g)
    return outs[0] if len(outs) == 1 else jnp.concatenate(outs, axis=1)


def _rope(y, cos, sin):
    rows = y.shape[0]
    lane = lax.broadcasted_iota(I32, (rows, LANES), 1)
    even = (lane % 2) == 0
    outs = []
    for c in range(y.shape[1] // LANES):
        ch = y[:, c * LANES:(c + 1) * LANES]
        partner = jnp.where(even, pltpu.roll(ch, LANES - 1, 1), pltpu.roll(ch, 1, 1))
        outs.append(ch * cos + partner * sin)
    return outs[0] if len(outs) == 1 else jnp.concatenate(outs, axis=1)


def _proj_kernel(x_ref, ctx_ref, mod_ref, gn_ref, w_ref, cos_ref, sin_ref, qna_ref, kna_ref, qnb_ref, knb_ref,
                 gmat_ref, qat_ref, ka_ref, vat_ref, qbt_ref, kb_ref, vbt_ref):
    b = pl.program_id(0)
    j = pl.program_id(1)
    is_ctx = j == 0
    xt = jnp.where(is_ctx, ctx_ref[0], x_ref[0])
    mrow = jnp.where(is_ctx, pl.num_programs(0), b)
    mod = mod_ref[mrow]
    h = _rms_rows(xt, gn_ref[...], EPS) * (1.0 + mod[1:2, :]) + mod[0:1, :]
    y = _dot(h.astype(BF16), w_ref[...])
    cos = jnp.where(is_ctx, 1.0, cos_ref[...])
    sin = jnp.where(is_ctx, 0.0, sin_ref[...])
    gmat = gmat_ref[...]
    scale = HEAD_DIM ** -0.5 * math.log2(math.e)

    o = 0
    qa = _rope(_group_rms(y[:, o:o + A_Q_COLS], gmat, qna_ref[...]), cos, sin) * scale
    o += A_Q_COLS
    ka = _rope(_group_rms(y[:, o:o + A_KV_COLS], gmat, kna_ref[...]), cos, sin)
    o += A_KV_COLS
    va = y[:, o:o + A_KV_COLS]
    o += A_KV_COLS
    qb = _rope(_group_rms(y[:, o:o + B_QK_COLS], gmat, qnb_ref[...]), cos, sin) * scale
    o += B_QK_COLS
    kb = _rope(_group_rms(y[:, o:o + B_QK_COLS], gmat, knb_ref[...]), cos, sin)
    o += B_QK_COLS
    vb = y[:, o:o + B_WIDTH]

    @pl.when(j > 0)
    def _():
        qat_ref[0] = qa.T.astype(BF16)
        qbt_ref[0] = qb.T.astype(BF16)

    ka_ref[0] = ka.astype(BF16)
    kb_ref[0] = kb.astype(BF16)
    ones = jnp.ones((ONES_ROWS, xt.shape[0]), BF16)
    for dst_ref, vt, width in ((vat_ref, va.T.astype(BF16), HEAD_DIM), (vbt_ref, vb.T.astype(BF16), B_V_DIM)):
        stride = width + ONES_ROWS
        for hd in range(vt.shape[0] // width):
            dst_ref[0, 0, hd * stride:hd * stride + width, :] = vt[hd * width:(hd + 1) * width, :]
            dst_ref[0, 0, hd * stride + width:(hd + 1) * stride, :] = ones


def _project(x, ctx, mod, gn, w_qkv, cos, sin, qna, kna, qnb, knb, gmat):
    bsz, s, d = x.shape
    n_lat = s // TOK_TILE
    n_key = n_lat + 1
    lk = n_key * TOK_TILE
    lat = lambda b, j: (b, jnp.maximum(j - 1, 0), 0)
    const2 = lambda b, j: (0, 0)
    tab = lambda b, j: (jnp.maximum(j - 1, 0), 0)
    return pl.pallas_call(
        _proj_kernel,
        grid=(bsz, n_key),
        in_specs=[
            pl.BlockSpec((1, TOK_TILE, d), lat),
            pl.BlockSpec((1, TOK_TILE, d), lambda b, j: (b, 0, 0)),
            pl.BlockSpec(mod.shape, lambda b, j: (0, 0, 0)),
            pl.BlockSpec((1, d), const2),
            pl.BlockSpec((d, QKV_COLS), const2),
            pl.BlockSpec((TOK_TILE, LANES), tab),
            pl.BlockSpec((TOK_TILE, LANES), tab),
            pl.BlockSpec((1, LANES), const2),
            pl.BlockSpec((1, LANES), const2),
            pl.BlockSpec((1, LANES), const2),
            pl.BlockSpec((1, LANES), const2),
            pl.BlockSpec((MXU_DIM, MXU_DIM), const2),
        ],
        out_specs=[
            pl.BlockSpec((1, A_Q_COLS, TOK_TILE), lambda b, j: (b, 0, jnp.maximum(j - 1, 0))),
            pl.BlockSpec((1, TOK_TILE, A_KV_COLS), lambda b, j: (b, j, 0)),
            pl.BlockSpec((1, 1, A_KV_HEADS * A_VROWS, TOK_TILE), lambda b, j: (b, j, 0, 0)),
            pl.BlockSpec((1, B_QK_COLS, TOK_TILE), lambda b, j: (b, 0, jnp.maximum(j - 1, 0))),
            pl.BlockSpec((1, TOK_TILE, B_QK_COLS), lambda b, j: (b, j, 0)),
            pl.BlockSpec((1, 1, B_HEADS * B_VROWS, TOK_TILE), lambda b, j: (b, j, 0, 0)),
        ],
        out_shape=[
            jax.ShapeDtypeStruct((bsz, A_Q_COLS, s), BF16),
            jax.ShapeDtypeStruct((bsz, lk, A_KV_COLS), BF16),
            jax.ShapeDtypeStruct((bsz, n_key, A_KV_HEADS * A_VROWS, TOK_TILE), BF16),
            jax.ShapeDtypeStruct((bsz, B_QK_COLS, s), BF16),
            jax.ShapeDtypeStruct((bsz, lk, B_QK_COLS), BF16),
            jax.ShapeDtypeStruct((bsz, n_key, B_HEADS * B_VROWS, TOK_TILE), BF16),
        ],
        compiler_params=pltpu.CompilerParams(vmem_limit_bytes=VMEM_LIMIT),
        name="proj",
    )(x, ctx, mod, gn, w_qkv, cos, sin, qna, kna, qnb, knb, gmat)


def _flash(k_ref, vt_ref, wq, pv, dv, s_even, s_odd):
    n = wq.shape[1]
    n_tiles = n // MXU_DIM
    n_chunks = vt_ref.shape[1]
    cols = [slice(t * MXU_DIM, (t + 1) * MXU_DIM) for t in range(n_tiles)]
    wqs = [wq[:, cs] for cs in cols]

    def scores(c, t, dst):
        start = pl.multiple_of(c * TOK_TILE, TOK_TILE)
        dst[:, cols[t]] = _dot(k_ref[0, pl.ds(start, TOK_TILE), :], wqs[t])

    def chunk(c, cur, nxt, state):
        out = []
        for t in range(n_tiles):
            if nxt is not None:
                scores(c + 1, t, nxt)
            m, acc = state[t]
            s = cur[:, cols[t]]
            m_new = jnp.maximum(m, jnp.max(s, axis=0, keepdims=True))
            alpha = jnp.exp2(m - m_new)
            p = jnp.exp2(s - m_new).astype(BF16)
            out.append((m_new, alpha * acc + pv(t, vt_ref[0, c], p)))
        return tuple(out)

    for t in range(n_tiles):
        scores(0, t, s_even)

    bufs = (s_even, s_odd)

    per_trip = SCORE_TILES_PER_TRIP // n_tiles
    assert per_trip % 2 == 0, "the two score buffers alternate, so a trip must hold an even number of chunks"

    def trip(i, state):
        for u in range(per_trip):
            state = chunk(i * per_trip + u, bufs[u % 2], bufs[(u + 1) % 2], state)
        return state

    init = tuple((jnp.full((1, MXU_DIM), -jnp.inf, F32), jnp.zeros((dv + ONES_ROWS, MXU_DIM), F32))
                 for _ in range(n_tiles))
    n_trips = (n_chunks - 1) // per_trip
    state = lax.fori_loop(0, n_trips, trip, init)
    for c in range(n_trips * per_trip, n_chunks):
        state = chunk(c, bufs[c % 2], bufs[(c + 1) % 2] if c + 1 < n_chunks else None, state)
    return jnp.concatenate([acc for _, acc in state], axis=1)


def _attn_a_kernel(qt_ref, k_ref, vt_ref, o_ref, s_even, s_odd):
    tq = qt_ref.shape[2]
    half = A_GROUP * tq
    zeros = jnp.zeros((HEAD_DIM, half), BF16)
    rows = []
    for g in range(A_KV_HEADS):
        heads = range(g * A_GROUP, (g + 1) * A_GROUP)
        qg = jnp.concatenate([qt_ref[0, h * HEAD_DIM:(h + 1) * HEAD_DIM, :] for h in heads], axis=1)
        rows.append(jnp.concatenate([qg if gg == g else zeros for gg in range(A_KV_HEADS)], axis=1))
    wq = jnp.concatenate(rows, axis=0)

    def pv(t, vc, p):
        g = t * MXU_DIM // half
        return _dot(vc[g * A_VROWS:(g + 1) * A_VROWS], p)

    acc = _flash(k_ref, vt_ref, wq, pv, HEAD_DIM, s_even, s_odd)
    o = acc[:HEAD_DIM] / acc[HEAD_DIM:HEAD_DIM + 1]
    outs = [o[:, h * tq:(h + 1) * tq].T for h in range(A_HEADS)]
    o_ref[0] = jnp.concatenate(outs, axis=1).astype(o_ref.dtype)


def _attn_a(qat, ka, vat):
    bsz, _, s = qat.shape
    lk = ka.shape[1]
    n_chunks = vat.shape[1]
    return pl.pallas_call(
        _attn_a_kernel,
        grid=(bsz, s // TQ_A),
        in_specs=[
            pl.BlockSpec((1, A_Q_COLS, TQ_A), lambda b, i: (b, 0, i)),
            pl.BlockSpec((1, lk, A_KV_COLS), lambda b, i: (b, 0, 0)),
            pl.BlockSpec((1, n_chunks, A_KV_HEADS * A_VROWS, TOK_TILE), lambda b, i: (b, 0, 0, 0)),
        ],
        out_specs=pl.BlockSpec((1, TQ_A, A_Q_COLS), lambda b, i: (b, i, 0)),
        out_shape=jax.ShapeDtypeStruct((bsz, s, A_Q_COLS), BF16),
        scratch_shapes=[pltpu.VMEM((TOK_TILE, A_HEADS * TQ_A), F32)] * 2,
        compiler_params=pltpu.CompilerParams(vmem_limit_bytes=VMEM_LIMIT),
        name="attn_a",
    )(qat, ka, vat)


def _attn_b_kernel(qt_ref, k_ref, vt_ref, lq1_ref, lk1_ref, lq2_ref, lk2_ref, sub_ref, o_ref, s_even, s_odd):
    tq = qt_ref.shape[2]
    zeros = jnp.zeros((HEAD_DIM, tq), BF16)
    q1 = qt_ref[0, :HEAD_DIM, :]
    q2 = qt_ref[0, HEAD_DIM:, :]
    wq = jnp.concatenate([jnp.concatenate([q1, zeros], axis=1), jnp.concatenate([zeros, q2], axis=1)], axis=0)
    acc = _flash(k_ref, vt_ref, wq, lambda t, vc, p: _dot(vc, p), B_V_DIM, s_even, s_odd)
    r = acc[:B_V_DIM] / acc[B_V_DIM:B_V_DIM + 1]
    lam = (jnp.exp(jnp.sum(lq1_ref[...] * lk1_ref[...], axis=1, keepdims=True))
           - jnp.exp(jnp.sum(lq2_ref[...] * lk2_ref[...], axis=1, keepdims=True)) + LAM_INIT)
    o = r[:, :tq] - lam * r[:, tq:]
    ms = jnp.mean(o * o, axis=0, keepdims=True)
    o = o * lax.rsqrt(ms + SUBLN_EPS) * sub_ref[...] * (1.0 - LAM_INIT)
    o_ref[0] = o.T.astype(o_ref.dtype)


def _attn_b(qbt, kb, vbt, lq1, lk1, lq2, lk2, subln):
    bsz, _, s = qbt.shape
    lk = kb.shape[1]
    n_chunks = vbt.shape[1]
    vec = pl.BlockSpec((1, HEAD_DIM), lambda b, h, i: (0, 0))
    return pl.pallas_call(
        _attn_b_kernel,
        grid=(bsz, B_HEADS, s // TQ_B),
        in_specs=[
            pl.BlockSpec((1, 2 * HEAD_DIM, TQ_B), lambda b, h, i: (b, h, i)),
            pl.BlockSpec((1, lk, 2 * HEAD_DIM), lambda b, h, i: (b, 0, h)),
            pl.BlockSpec((1, n_chunks, B_VROWS, TOK_TILE), lambda b, h, i: (b, 0, h, 0)),
            vec, vec, vec, vec,
            pl.BlockSpec((B_V_DIM, 1), lambda b, h, i: (0, 0)),
        ],
        out_specs=pl.BlockSpec((1, TQ_B, B_V_DIM), lambda b, h, i: (b, i, h)),
        out_shape=jax.ShapeDtypeStruct((bsz, s, B_WIDTH), BF16),
        scratch_shapes=[pltpu.VMEM((TOK_TILE, 2 * TQ_B), F32)] * 2,
        compiler_params=pltpu.CompilerParams(vmem_limit_bytes=VMEM_LIMIT),
        name="attn_b",
    )(qbt, kb, vbt, lq1, lk1, lq2, lk2, subln)


def _merge_kernel(tiles_per_sample, x_ref, ya_ref, yb_ref, mod_ref, gna_ref, gnm_ref, wg_ref, woa_ref, wob_ref,
                  wout_ref, rw_ref, rb_ref, utri_ref, x1_ref, h2_ref, route_ref, cnt_ref, carry_ref):
    i = pl.program_id(0)

    @pl.when(i == 0)
    def _():
        carry_ref[...] = jnp.zeros_like(carry_ref)

    mod = mod_ref[i // tiles_per_sample]
    d = x_ref.shape[1]
    subs = [slice(u * TOK_TILE, (u + 1) * TOK_TILE) for u in range(x_ref.shape[0] // TOK_TILE)]
    xs = [x_ref[r, :] for r in subs]
    hs = [(_rms_rows(x, gna_ref[...], EPS) * (1.0 + mod[1:2, :]) + mod[0:1, :]).astype(BF16) for x in xs]
    gates = [_dot(h, wg_ref[...]) for h in hs]
    branch = [(_dot(ya_ref[r, :], woa_ref[...]), _dot(yb_ref[r, :], wob_ref[...])) for r in subs]
    merged = [(jax.nn.sigmoid(g[:, :d]) * pa + jax.nn.sigmoid(g[:, d:]) * pb).astype(BF16)
              for g, (pa, pb) in zip(gates, branch)]
    x1s = [x + mod[2:3, :] * _dot(m, wout_ref[...]) for x, m in zip(xs, merged)]
    h2s = [_rms_rows(x1, gnm_ref[...], EPS) * (1.0 + mod[4:5, :]) + mod[3:4, :] for x1 in x1s]
    for r, x1, h2 in zip(subs, x1s, h2s):
        x1_ref[r, :] = x1
        h2_ref[r, :] = h2

    all_logits = [_dot3_nt(rw_ref[...], h2) + rb_ref[...] for h2 in h2s]
    eiota = lax.broadcasted_iota(I32, (N_EXPERTS, TOK_TILE), 0)
    riota = lax.broadcasted_iota(I32, (ROUTE_ROWS, TOK_TILE), 0)
    carry = carry_ref[...]
    for r, logits in zip(subs, all_logits):
        work = logits
        idxs, vals = [], []
        for _ in range(TOP_K):
            mx = jnp.max(work, axis=0, keepdims=True)
            idx = jnp.min(jnp.where(work == mx, eiota, N_EXPERTS), axis=0, keepdims=True)
            idxs.append(idx)
            vals.append(mx)
            work = jnp.where(eiota == idx, -jnp.inf, work)
        exps = [jnp.exp(v - vals[0]) for v in vals]
        denom = exps[0] + exps[1] + exps[2] + exps[3]
        weights = [e / denom for e in exps]

        hits = [(eiota == idx) for idx in idxs]
        onehot = (hits[0] | hits[1] | hits[2] | hits[3]).astype(F32)
        before = _dot(onehot.astype(BF16), utri_ref[...]) + carry
        ranks = [jnp.sum(jnp.where(hit, before, 0.0), axis=0, keepdims=True) for hit in hits]
        carry = carry + jnp.sum(onehot, axis=1, keepdims=True)

        lines = [v.astype(F32) for v in idxs] + weights + ranks
        route = jnp.zeros((ROUTE_ROWS, TOK_TILE), F32)
        for k, line in enumerate(lines):
            route = jnp.where(riota == k, line, route)
        route_ref[:, r] = route
    carry_ref[...] = carry
    cnt_ref[...] = carry


def _merge(x2, ya2, yb2, mod, gna, gnm, wg, woa, wob, wout, rw, rb, utri, tiles_per_sample):
    t, d = x2.shape
    const2 = lambda i: (0, 0)
    row = lambda i: (i, 0)
    full = lambda a: pl.BlockSpec(a.shape, const2)
    return pl.pallas_call(
        functools.partial(_merge_kernel, tiles_per_sample),
        grid=(t // MERGE_TOK,),
        in_specs=[
            pl.BlockSpec((MERGE_TOK, d), row),
            pl.BlockSpec((MERGE_TOK, ya2.shape[1]), row),
            pl.BlockSpec((MERGE_TOK, yb2.shape[1]), row),
            pl.BlockSpec(mod.shape, lambda i: (0, 0, 0)),
            full(gna), full(gnm), full(wg), full(woa), full(wob), full(wout), full(rw), full(rb), full(utri),
        ],
        out_specs=[
            pl.BlockSpec((MERGE_TOK, d), row),
            pl.BlockSpec((MERGE_TOK, d), row),
            pl.BlockSpec((ROUTE_ROWS, MERGE_TOK), lambda i: (0, i)),
            pl.BlockSpec((N_EXPERTS, 1), const2),
        ],
        out_shape=[
            jax.ShapeDtypeStruct((t, d), F32),
            jax.ShapeDtypeStruct((t, d), F32),
            jax.ShapeDtypeStruct((ROUTE_ROWS, t), F32),
            jax.ShapeDtypeStruct((N_EXPERTS, 1), F32),
        ],
        scratch_shapes=[pltpu.VMEM((N_EXPERTS, 1), F32)],
        compiler_params=pltpu.CompilerParams(vmem_limit_bytes=VMEM_LIMIT),
        name="merge",
    )(x2, ya2, yb2, mod, gna, gnm, wg, woa, wob, wout, rw, rb, utri)


def _to_row_tiles(dst_ref, val):
    n, d = val.shape
    sub = d // LANES
    for c in range(sub):
        dst_ref[pl.ds(c, n, stride=sub), :] = val[:, c * LANES:(c + 1) * LANES]


def _from_row_tiles(src_ref, first_row, n, d):
    sub = d // LANES
    return jnp.concatenate([src_ref[pl.ds(first_row * sub + c, n, stride=sub), :] for c in range(sub)], axis=1)


def _row_tile_copy(src, src_row, dst, dst_row, sub, sem):
    src_at = pl.ds(pl.multiple_of(src_row * sub, sub), sub)
    dst_at = pl.ds(pl.multiple_of(dst_row * sub, sub), sub)
    return pltpu.make_async_copy(src.at[src_at], dst.at[dst_at], sem)


def _dispatch_kernel(pos_ref, padfrom_ref, padcnt_ref, nu_ref, h_ref, xb_out, tiles, zeros, sem, zsem):
    i = pl.program_id(0)
    n = pl.num_programs(0)
    slot = i % 2
    n_tok, d = h_ref.shape
    sub = d // LANES
    blk = zeros.shape[0]
    n_blocks = xb_out.shape[0] // blk

    def zero_fill(wait):
        def go(copy):
            copy.wait() if wait else copy.start()

        def per_expert(e, carry):
            cnt = padcnt_ref[e]
            for bit in reversed(range((FFN_ROWS - 1).bit_length())):
                rows = 1 << bit

                @pl.when((cnt & rows) != 0)
                def _():
                    done = (cnt >> (bit + 1)) << (bit + 1)
                    at = pl.multiple_of((padfrom_ref[e] + done) * sub, sub)
                    go(pltpu.make_async_copy(zeros.at[pl.ds(0, rows * sub)], xb_out.at[pl.ds(at, rows * sub)], zsem))
            return carry

        lax.fori_loop(0, N_EXPERTS, per_expert, 0)

        def per_block(j, carry):
            go(pltpu.make_async_copy(zeros, xb_out.at[pl.ds(pl.multiple_of(j * blk, blk), blk)], zsem))
            return carry

        lax.fori_loop(nu_ref[0], n_blocks, per_block, 0)

    @pl.when(i == 0)
    def _():
        zeros[...] = jnp.zeros_like(zeros)
        zero_fill(wait=False)
        zero_fill(wait=True)

    def drain(sl):
        for _ in range(TOP_K):
            pltpu.make_async_copy(tiles.at[sl], xb_out.at[pl.ds(0, n_tok * sub)], sem.at[sl]).wait()

    @pl.when(i >= 2)
    def _():
        drain(slot)

    _to_row_tiles(tiles.at[slot], h_ref[...])
    base = i * n_tok

    def issue(t, carry):
        for k in range(TOP_K):
            _row_tile_copy(tiles.at[slot], t, xb_out, pos_ref[(base + t) * TOP_K + k], sub,
                           sem.at[slot]).start(priority=k % DMA_QUEUES)
        return carry

    lax.fori_loop(0, n_tok, issue, 0, unroll=ISSUE_UNROLL // TOP_K)

    @pl.when((i == n - 1) & (n >= 2))
    def _():
        drain(1 - slot)

    @pl.when(i == n - 1)
    def _():
        drain(slot)


def _dispatch(pos_flat, pad_from, pad_cnt, n_used, h2, n_blocks):
    t, d = h2.shape
    sub = d // LANES
    return pl.pallas_call(
        _dispatch_kernel,
        grid_spec=pltpu.PrefetchScalarGridSpec(
            num_scalar_prefetch=4,
            grid=(t // TOK_TILE,),
            in_specs=[pl.BlockSpec((TOK_TILE, d), lambda i, *_: (i, 0))],
            out_specs=pl.BlockSpec(memory_space=pl.ANY),
            scratch_shapes=[pltpu.VMEM((2, TOK_TILE * sub, LANES), F32), pltpu.VMEM((FFN_ROWS * sub, LANES), F32),
                            pltpu.SemaphoreType.DMA((2,)), pltpu.SemaphoreType.DMA(())],
        ),
        out_shape=jax.ShapeDtypeStruct((n_blocks * FFN_ROWS * sub, LANES), F32),
        compiler_params=pltpu.CompilerParams(vmem_limit_bytes=VMEM_LIMIT),
        name="dispatch",
    )(pos_flat, pad_from, pad_cnt, n_used, h2)


def _ffn_kernel(be_ref, nu_ref, nxt_ref, x_ref, wg_hbm, bg_ref, wu_hbm, bu_ref, wd_hbm, bd_ref, o_ref,
                stage_g, stage_u, stage_d, wg_s, wu_s, wd_s, wsem):
    j = pl.program_id(0)
    prev = be_ref[jnp.maximum(j - 1, 0)]
    fresh = ((j == 0) | (be_ref[j] != prev)) & (j < nu_ref[0])
    streams = ((wg_hbm, stage_g, wg_s), (wu_hbm, stage_u, wu_s), (wd_hbm, stage_d, wd_s))

    def weight_copies(expert):
        return [pltpu.make_async_copy(hbm.at[expert], stage, wsem.at[n]) for n, (hbm, stage, _) in enumerate(streams)]

    @pl.when(j == 0)
    def _():
        for copy in weight_copies(be_ref[0]):
            copy.start()

    @pl.when(fresh)
    def _():
        for copy, (_, stage, dst) in zip(weight_copies(be_ref[j]), streams):
            copy.wait()
            dst[...] = stage[...].astype(BF16)

        @pl.when(nxt_ref[j] >= 0)
        def _():
            for copy in weight_copies(nxt_ref[j]):
                copy.start()

    @pl.when(j < nu_ref[0])
    def _():
        x = _from_row_tiles(x_ref, 0, FFN_ROWS, wg_s.shape[0]).astype(BF16)
        gate = jnp.minimum(_dot(x, wg_s[...]) + bg_ref[0], SWIGLU_LIMIT)
        up = jnp.clip(_dot(x, wu_s[...]) + bu_ref[0], -SWIGLU_LIMIT, SWIGLU_LIMIT)
        act = (up + 1.0) * (gate * jax.nn.sigmoid(SWIGLU_ALPHA * gate))
        _to_row_tiles(o_ref, _dot(act.astype(BF16), wd_s[...]) + bd_ref[0])

    @pl.when(j >= nu_ref[0])
    def _():
        o_ref[...] = jnp.zeros_like(o_ref)


def _ffn(block_e, n_used, next_e, xb, w_gate, b_gate, w_up, b_up, w_down, b_down):
    d, de = w_gate.shape[1:]
    blk = FFN_ROWS * d // LANES
    n_blocks = xb.shape[0] // blk
    xrow = lambda j, be, nu, nxt: (jnp.minimum(j, nu[0] - 1), 0)
    bsel = lambda j, be, nu, nxt: (be[j], 0, 0)
    any_spec = pl.BlockSpec(memory_space=pl.ANY)
    return pl.pallas_call(
        _ffn_kernel,
        grid_spec=pltpu.PrefetchScalarGridSpec(
            num_scalar_prefetch=3,
            grid=(n_blocks,),
            in_specs=[
                pl.BlockSpec((blk, LANES), xrow),
                any_spec,
                pl.BlockSpec((1, 1, de), bsel),
                any_spec,
                pl.BlockSpec((1, 1, de), bsel),
                any_spec,
                pl.BlockSpec((1, 1, d), bsel),
            ],
            out_specs=pl.BlockSpec((blk, LANES), lambda j, be, nu, nxt: (j, 0)),
            scratch_shapes=[pltpu.VMEM((d, de), F32), pltpu.VMEM((d, de), F32), pltpu.VMEM((de, d), F32),
                            pltpu.VMEM((d, de), BF16), pltpu.VMEM((d, de), BF16), pltpu.VMEM((de, d), BF16),
                            pltpu.SemaphoreType.DMA((3,))],
        ),
        out_shape=jax.ShapeDtypeStruct(xb.shape, F32),
        compiler_params=pltpu.CompilerParams(vmem_limit_bytes=VMEM_LIMIT),
        name="ffn",
    )(block_e, n_used, next_e, xb, w_gate, b_gate, w_up, b_up, w_down, b_down)


def _combine_kernel(tiles_per_sample, pos_ref, x1_ref, route_ref, mod_ref, yb_hbm, o_ref, buf, sem):
    i = pl.program_id(0)
    n = pl.num_programs(0)
    rows = TOP_K * CMB_TOK
    d = x1_ref.shape[1]
    sub = d // LANES

    def issue(tile, slot):
        def body(g, carry):
            for u in range(ISSUE_UNROLL):
                r = g * ISSUE_UNROLL + u
                _row_tile_copy(yb_hbm, pos_ref[tile * rows + r], buf.at[slot], r, sub,
                               sem.at[slot]).start(priority=u % DMA_QUEUES)
            return carry
        lax.fori_loop(0, rows // ISSUE_UNROLL, body, 0)

    @pl.when(i == 0)
    def _():
        issue(0, 0)

    @pl.when(i + 1 < n)
    def _():
        issue(i + 1, (i + 1) % 2)

    slot = i % 2
    pltpu.make_async_copy(yb_hbm.at[pl.ds(0, rows * sub)], buf.at[slot], sem.at[slot]).wait()

    b = i // tiles_per_sample
    g_m = mod_ref[b][5:6, :]
    route = route_ref[...]
    acc = jnp.zeros(x1_ref.shape, F32)
    for k in range(TOP_K):
        acc = acc + route[:, TOP_K + k:TOP_K + k + 1] * _from_row_tiles(buf.at[slot], k * CMB_TOK, CMB_TOK, d)
    o_ref[...] = x1_ref[...] + g_m * acc


def _combine(pos_tiles, x1, route, mod, yb, tiles_per_sample):
    t, d = x1.shape
    row = lambda i, pos: (i, 0)
    return pl.pallas_call(
        functools.partial(_combine_kernel, tiles_per_sample),
        grid_spec=pltpu.PrefetchScalarGridSpec(
            num_scalar_prefetch=1,
            grid=(t // CMB_TOK,),
            in_specs=[
                pl.BlockSpec((CMB_TOK, d), row),
                pl.BlockSpec((CMB_TOK, route.shape[1]), row),
                pl.BlockSpec(mod.shape, lambda i, pos: (0, 0, 0)),
                pl.BlockSpec(memory_space=pl.ANY),
            ],
            out_specs=pl.BlockSpec((CMB_TOK, d), row),
            scratch_shapes=[pltpu.VMEM((2, TOP_K * CMB_TOK * d // LANES, LANES), F32),
                            pltpu.SemaphoreType.DMA((2,))],
        ),
        out_shape=jax.ShapeDtypeStruct((t, d), F32),
        compiler_params=pltpu.CompilerParams(vmem_limit_bytes=VMEM_LIMIT),
        name="combine",
    )(pos_tiles, x1, route, mod, yb)


def _rope_tables(s):
    rows = s // GRID_W
    row = jnp.repeat(jnp.arange(rows, dtype=F32), GRID_W)
    col = jnp.tile(jnp.arange(GRID_W, dtype=F32), rows)
    inv = 1.0 / (ROPE_THETA ** (jnp.arange(AXIS_PAIRS, dtype=F32) / AXIS_PAIRS))
    ang = jnp.concatenate([row[:, None] * inv, col[:, None] * inv], axis=-1)
    cos, sin = jnp.cos(ang), jnp.sin(ang)
    cos_i = jnp.repeat(cos, 2, axis=-1)
    sin_i = jnp.stack([-sin, sin], axis=-1).reshape(s, HEAD_DIM)
    reps = LANES // HEAD_DIM
    return jnp.tile(cos_i, (1, reps)), jnp.tile(sin_i, (1, reps))


def kernel(x, c, ctx, c_ctx, w_ada, b_ada, norm_attn, w_in, q_norm_a, k_norm_a, q_norm_b, k_norm_b, lambda_q1,
           lambda_k1, lambda_q2, lambda_k2, subln_b, w_oa, w_ob, w_out, norm_mlp, router_w, router_b, w_gate,
           b_gate, w_up, b_up, w_down, b_down):
    bsz, s, d = x.shape
    assert ctx.shape[1] == TOK_TILE and s % MERGE_TOK == 0 and s % TQ_A == 0 and s % TQ_B == 0 and s % GRID_W == 0
    assert w_ada.shape[0] == 1, "single-layer block"
    t = bsz * s
    assert t % CMB_TOK == 0 and (t * TOP_K) % FFN_ROWS == 0

    mod_rows = 8
    assert bsz + 1 <= mod_rows
    cvec = jnp.concatenate([c, c_ctx[None, :], jnp.zeros((mod_rows - bsz - 1, d), F32)], axis=0)
    mod = _adaln(cvec, w_ada[0], b_ada[0][None, :]).reshape(mod_rows, N_MOD, d)

    cos, sin = _rope_tables(s)
    tile_gain = lambda g: jnp.tile(g, LANES // HEAD_DIM)[None, :]
    gidx = jnp.arange(MXU_DIM) // HEAD_DIM
    gmat = jnp.where(gidx[:, None] == gidx[None, :], 1.0 / HEAD_DIM, 0.0).astype(BF16)
    w_qkv = w_in[0][:, :QKV_COLS].astype(BF16)
    w_g = w_in[0][:, QKV_COLS:].astype(BF16)

    qat, ka, vat, qbt, kb, vbt = _project(
        x, ctx, mod, norm_attn, w_qkv, cos, sin, tile_gain(q_norm_a[0]), tile_gain(k_norm_a[0]),
        tile_gain(q_norm_b[0]), tile_gain(k_norm_b[0]), gmat)

    ya = _attn_a(qat, ka, vat)
    yb = _attn_b(qbt, kb, vbt, lambda_q1, lambda_k1, lambda_q2, lambda_k2, subln_b[0][:, None])

    tri = jnp.arange(TOK_TILE)
    utri = (tri[:, None] < tri[None, :]).astype(BF16)
    x1, h2, route_t, counts = _merge(
        x.reshape(t, d), ya.reshape(t, A_Q_COLS), yb.reshape(t, B_WIDTH), mod, norm_attn, norm_mlp, w_g,
        w_oa[0].astype(BF16), w_ob[0].astype(BF16), w_out[0].astype(BF16), router_w[0].T, router_b[0][:, None], utri,
        s // MERGE_TOK)

    counts = counts[:, 0].astype(I32)
    padded = (counts + FFN_ROWS - 1) // FFN_ROWS * FFN_ROWS
    pad_end = jnp.cumsum(padded)
    pad_start = pad_end - padded
    top_e = route_t[:TOP_K].astype(I32)
    rank = route_t[2 * TOP_K:3 * TOP_K].astype(I32)
    experts = jnp.arange(N_EXPERTS, dtype=I32)
    start_of = jnp.sum(jnp.where(top_e[..., None] == experts, pad_start, 0), axis=-1)
    pos_t = start_of + rank
    n_blocks = t * TOP_K // FFN_ROWS + N_EXPERTS
    block_start = jnp.arange(n_blocks, dtype=I32) * FFN_ROWS
    block_e = jnp.minimum(jnp.sum((pad_end[None, :] <= block_start[:, None]).astype(I32), axis=1), N_EXPERTS - 1)
    n_used = (pad_end[-1:] // FFN_ROWS).astype(I32)
    first_at_or_after = lax.cummin(jnp.where(padded > 0, jnp.arange(N_EXPERTS, dtype=I32), N_EXPERTS), reverse=True)
    next_nonempty = jnp.concatenate([first_at_or_after[1:], jnp.full((1,), N_EXPERTS, I32)])
    next_of = jnp.where(next_nonempty < N_EXPERTS, next_nonempty, -1)
    next_e = jnp.sum(jnp.where(block_e[:, None] == experts, next_of, 0), axis=-1)

    xb = _dispatch(pos_t.T.reshape(-1), pad_start + counts, padded - counts, n_used, h2, n_blocks)
    de = w_gate.shape[-1]
    yb_rows = _ffn(block_e, n_used, next_e, xb, w_gate[0], b_gate[0].reshape(N_EXPERTS, 1, de), w_up[0],
                   b_up[0].reshape(N_EXPERTS, 1, de), w_down[0], b_down[0].reshape(N_EXPERTS, 1, d))
    pos_tiles = pos_t.reshape(TOP_K, t // CMB_TOK, CMB_TOK).transpose(1, 0, 2).reshape(-1)
    out = _combine(pos_tiles, x1, route_t.T, mod, yb_rows, s // CMB_TOK)
    return out.reshape(bsz, s, d)
```

```python
import functools
import math

import jax
import jax.numpy as jnp
from jax import lax
from jax.experimental import pallas as pl
from jax.experimental.pallas import tpu as pltpu

F32 = jnp.float32
BF16 = jnp.bfloat16
I32 = jnp.int32

GRID_W = 64
HEAD_DIM = 64
A_HEADS = 8
A_KV_HEADS = 2
A_GROUP = A_HEADS // A_KV_HEADS
B_HEADS = 4
B_V_DIM = 2 * HEAD_DIM
N_EXPERTS = 32
TOP_K = 4
N_MOD = 6
ROPE_THETA = 10000.0
AXIS_PAIRS = HEAD_DIM // 4
SWIGLU_LIMIT = 7.0
SWIGLU_ALPHA = 1.702
EPS = 1e-6
SUBLN_EPS = 1e-5
LAM_INIT = 0.8 - 0.6 * math.exp(-0.3 * 0)

A_Q_COLS = A_HEADS * HEAD_DIM
A_KV_COLS = A_KV_HEADS * HEAD_DIM
B_QK_COLS = B_HEADS * 2 * HEAD_DIM
B_WIDTH = B_HEADS * B_V_DIM
QKV_COLS = A_Q_COLS + 2 * A_KV_COLS + 2 * B_QK_COLS + B_WIDTH

LANES = 128
MXU_DIM = 256
TOK_TILE = 256
MERGE_TOK = 4 * TOK_TILE
TQ_A = 256
TQ_B = 512
SCORE_TILES_PER_TRIP = 128
ONES_ROWS = 16
A_VROWS = HEAD_DIM + ONES_ROWS
B_VROWS = B_V_DIM + ONES_ROWS
ADA_COLS = 1536
FFN_ROWS = 256
CMB_TOK = 128
ROUTE_ROWS = 16
ISSUE_UNROLL = 16
DMA_QUEUES = 2
VMEM_LIMIT = 56 * 1024 * 1024


def _dot(a, b):
    return jnp.dot(a, b, preferred_element_type=F32)


def _split_bf16(a):
    hi = a.astype(BF16)
    lo = (a - hi.astype(F32)).astype(BF16)
    return hi, lo


def _dot3(a, b):
    a_hi, a_lo = _split_bf16(a)
    b_hi, b_lo = _split_bf16(b)
    return _dot(a_hi, b_hi) + (_dot(a_hi, b_lo) + _dot(a_lo, b_hi))


def _dot3_nt(a, b):
    dims = (((1,), (1,)), ((), ()))
    dot_nt = lambda u, v: lax.dot_general(u, v, dims, preferred_element_type=F32)
    a_hi, a_lo = _split_bf16(a)
    b_hi, b_lo = _split_bf16(b)
    return dot_nt(a_hi, b_hi) + (dot_nt(a_hi, b_lo) + dot_nt(a_lo, b_hi))


def _rms_rows(x, gain, eps):
    ms = jnp.mean(x * x, axis=-1, keepdims=True)
    return x * lax.rsqrt(ms + eps) * gain


def _adaln_kernel(c_ref, w_ref, b_ref, o_ref):
    c = c_ref[...]
    a = c * jax.nn.sigmoid(c)
    o_ref[...] = _dot3(a, w_ref[...]) + b_ref[...]


def _adaln(cvec, w_ada, b_ada):
    rows, d = cvec.shape
    n = w_ada.shape[1]
    return pl.pallas_call(
        _adaln_kernel,
        grid=(n // ADA_COLS,),
        in_specs=[
            pl.BlockSpec((rows, d), lambda j: (0, 0)),
            pl.BlockSpec((d, ADA_COLS), lambda j: (0, j)),
            pl.BlockSpec((1, ADA_COLS), lambda j: (0, j)),
        ],
        out_specs=pl.BlockSpec((rows, ADA_COLS), lambda j: (0, j)),
        out_shape=jax.ShapeDtypeStruct((rows, n), F32),
        compiler_params=pltpu.CompilerParams(vmem_limit_bytes=VMEM_LIMIT),
        name="adaln",
    )(cvec, w_ada, b_ada)


def _group_rms(y, gmat, gain):
    w_total = y.shape[1]
    outs = []
    for c0 in range(0, w_total, MXU_DIM):
        w = min(MXU_DIM, w_total - c0)
        yc = y[:, c0:c0 + w]
        ms = _dot((yc * yc).astype(BF16), gmat[:w, :w])
        reps = w // LANES
        g = gain if reps == 1 else jnp.concatenate([gain] * reps, axis=1)
        outs.append(yc * lax.rsqrt(ms + EPS) * g)
    return outs[0] if len(outs) == 1 else jnp.concatenate(outs, axis=1)


def _rope(y, cos, sin):
    rows = y.shape[0]
    lane = lax.broadcasted_iota(I32, (rows, LANES), 1)
    even = (lane % 2) == 0
    outs = []
    for c in range(y.shape[1] // LANES):
        ch = y[:, c * LANES:(c + 1) * LANES]
        partner = jnp.where(even, pltpu.roll(ch, LANES - 1, 1), pltpu.roll(ch, 1, 1))
        outs.append(ch * cos + partner * sin)
    return outs[0] if len(outs) == 1 else jnp.concatenate(outs, axis=1)


def _proj_kernel(x_ref, ctx_ref, mod_ref, gn_ref, w_ref, cos_ref, sin_ref, qna_ref, kna_ref, qnb_ref, knb_ref,
                 gmat_ref, qat_ref, ka_ref, vat_ref, qbt_ref, kb_ref, vbt_ref):
    b = pl.program_id(0)
    j = pl.program_id(1)
    is_ctx = j == 0
    xt = jnp.where(is_ctx, ctx_ref[0], x_ref[0])
    mrow = jnp.where(is_ctx, pl.num_programs(0), b)
    mod = mod_ref[mrow]
    h = _rms_rows(xt, gn_ref[...], EPS) * (1.0 + mod[1:2, :]) + mod[0:1, :]
    y = _dot(h.astype(BF16), w_ref[...])
    cos = jnp.where(is_ctx, 1.0, cos_ref[...])
    sin = jnp.where(is_ctx, 0.0, sin_ref[...])
    gmat = gmat_ref[...]
    scale = HEAD_DIM ** -0.5 * math.log2(math.e)

    o = 0
    qa = _rope(_group_rms(y[:, o:o + A_Q_COLS], gmat, qna_ref[...]), cos, sin) * scale
    o += A_Q_COLS
    ka = _rope(_group_rms(y[:, o:o + A_KV_COLS], gmat, kna_ref[...]), cos, sin)
    o += A_KV_COLS
    va = y[:, o:o + A_KV_COLS]
    o += A_KV_COLS
    qb = _rope(_group_rms(y[:, o:o + B_QK_COLS], gmat, qnb_ref[...]), cos, sin) * scale
    o += B_QK_COLS
    kb = _rope(_group_rms(y[:, o:o + B_QK_COLS], gmat, knb_ref[...]), cos, sin)
    o += B_QK_COLS
    vb = y[:, o:o + B_WIDTH]

    @pl.when(j > 0)
    def _():
        qat_ref[0] = qa.T.astype(BF16)
        qbt_ref[0] = qb.T.astype(BF16)

    ka_ref[0] = ka.astype(BF16)
    kb_ref[0] = kb.astype(BF16)
    ones = jnp.ones((ONES_ROWS, xt.shape[0]), BF16)
    for dst_ref, vt, width in ((vat_ref, va.T.astype(BF16), HEAD_DIM), (vbt_ref, vb.T.astype(BF16), B_V_DIM)):
        stride = width + ONES_ROWS
        for hd in range(vt.shape[0] // width):
            dst_ref[0, 0, hd * stride:hd * stride + width, :] = vt[hd * width:(hd + 1) * width, :]
            dst_ref[0, 0, hd * stride + width:(hd + 1) * stride, :] = ones


def _project(x, ctx, mod, gn, w_qkv, cos, sin, qna, kna, qnb, knb, gmat):
    bsz, s, d = x.shape
    n_lat = s // TOK_TILE
    n_key = n_lat + 1
    lk = n_key * TOK_TILE
    lat = lambda b, j: (b, jnp.maximum(j - 1, 0), 0)
    const2 = lambda b, j: (0, 0)
    tab = lambda b, j: (jnp.maximum(j - 1, 0), 0)
    return pl.pallas_call(
        _proj_kernel,
        grid=(bsz, n_key),
        in_specs=[
            pl.BlockSpec((1, TOK_TILE, d), lat),
            pl.BlockSpec((1, TOK_TILE, d), lambda b, j: (b, 0, 0)),
            pl.BlockSpec(mod.shape, lambda b, j: (0, 0, 0)),
            pl.BlockSpec((1, d), const2),
            pl.BlockSpec((d, QKV_COLS), const2),
            pl.BlockSpec((TOK_TILE, LANES), tab),
            pl.BlockSpec((TOK_TILE, LANES), tab),
            pl.BlockSpec((1, LANES), const2),
            pl.BlockSpec((1, LANES), const2),
            pl.BlockSpec((1, LANES), const2),
            pl.BlockSpec((1, LANES), const2),
            pl.BlockSpec((MXU_DIM, MXU_DIM), const2),
        ],
        out_specs=[
            pl.BlockSpec((1, A_Q_COLS, TOK_TILE), lambda b, j: (b, 0, jnp.maximum(j - 1, 0))),
            pl.BlockSpec((1, TOK_TILE, A_KV_COLS), lambda b, j: (b, j, 0)),
            pl.BlockSpec((1, 1, A_KV_HEADS * A_VROWS, TOK_TILE), lambda b, j: (b, j, 0, 0)),
            pl.BlockSpec((1, B_QK_COLS, TOK_TILE), lambda b, j: (b, 0, jnp.maximum(j - 1, 0))),
            pl.BlockSpec((1, TOK_TILE, B_QK_COLS), lambda b, j: (b, j, 0)),
            pl.BlockSpec((1, 1, B_HEADS * B_VROWS, TOK_TILE), lambda b, j: (b, j, 0, 0)),
        ],
        out_shape=[
            jax.ShapeDtypeStruct((bsz, A_Q_COLS, s), BF16),
            jax.ShapeDtypeStruct((bsz, lk, A_KV_COLS), BF16),
            jax.ShapeDtypeStruct((bsz, n_key, A_KV_HEADS * A_VROWS, TOK_TILE), BF16),
            jax.ShapeDtypeStruct((bsz, B_QK_COLS, s), BF16),
            jax.ShapeDtypeStruct((bsz, lk, B_QK_COLS), BF16),
            jax.ShapeDtypeStruct((bsz, n_key, B_HEADS * B_VROWS, TOK_TILE), BF16),
        ],
        compiler_params=pltpu.CompilerParams(vmem_limit_bytes=VMEM_LIMIT),
        name="proj",
    )(x, ctx, mod, gn, w_qkv, cos, sin, qna, kna, qnb, knb, gmat)


def _flash(k_ref, vt_ref, wq, pv, dv, s_even, s_odd):
    n = wq.shape[1]
    n_tiles = n // MXU_DIM
    n_chunks = vt_ref.shape[1]
    cols = [slice(t * MXU_DIM, (t + 1) * MXU_DIM) for t in range(n_tiles)]
    wqs = [wq[:, cs] for cs in cols]

    def scores(c, t, dst):
        start = pl.multiple_of(c * TOK_TILE, TOK_TILE)
        dst[:, cols[t]] = _dot(k_ref[0, pl.ds(start, TOK_TILE), :], wqs[t])

    def chunk(c, cur, nxt, state):
        out = []
        for t in range(n_tiles):
            if nxt is not None:
                scores(c + 1, t, nxt)
            m, acc = state[t]
            s = cur[:, cols[t]]
            m_new = jnp.maximum(m, jnp.max(s, axis=0, keepdims=True))
            alpha = jnp.exp2(m - m_new)
            p = jnp.exp2(s - m_new).astype(BF16)
            out.append((m_new, alpha * acc + pv(t, vt_ref[0, c], p)))
        return tuple(out)

    for t in range(n_tiles):
        scores(0, t, s_even)

    bufs = (s_even, s_odd)

    per_trip = SCORE_TILES_PER_TRIP // n_tiles
    assert per_trip % 2 == 0, "the two score buffers alternate, so a trip must hold an even number of chunks"

    def trip(i, state):
        for u in range(per_trip):
            state = chunk(i * per_trip + u, bufs[u % 2], bufs[(u + 1) % 2], state)
        return state

    init = tuple((jnp.full((1, MXU_DIM), -jnp.inf, F32), jnp.zeros((dv + ONES_ROWS, MXU_DIM), F32))
                 for _ in range(n_tiles))
    n_trips = (n_chunks - 1) // per_trip
    state = lax.fori_loop(0, n_trips, trip, init)
    for c in range(n_trips * per_trip, n_chunks):
        state = chunk(c, bufs[c % 2], bufs[(c + 1) % 2] if c + 1 < n_chunks else None, state)
    return jnp.concatenate([acc for _, acc in state], axis=1)


def _attn_a_kernel(qt_ref, k_ref, vt_ref, o_ref, s_even, s_odd):
    tq = qt_ref.shape[2]
    half = A_GROUP * tq
    zeros = jnp.zeros((HEAD_DIM, half), BF16)
    rows = []
    for g in range(A_KV_HEADS):
        heads = range(g * A_GROUP, (g + 1) * A_GROUP)
        qg = jnp.concatenate([qt_ref[0, h * HEAD_DIM:(h + 1) * HEAD_DIM, :] for h in heads], axis=1)
        rows.append(jnp.concatenate([qg if gg == g else zeros for gg in range(A_KV_HEADS)], axis=1))
    wq = jnp.concatenate(rows, axis=0)

    def pv(t, vc, p):
        g = t * MXU_DIM // half
        return _dot(vc[g * A_VROWS:(g + 1) * A_VROWS], p)

    acc = _flash(k_ref, vt_ref, wq, pv, HEAD_DIM, s_even, s_odd)
    o = acc[:HEAD_DIM] / acc[HEAD_DIM:HEAD_DIM + 1]
    outs = [o[:, h * tq:(h + 1) * tq].T for h in range(A_HEADS)]
    o_ref[0] = jnp.concatenate(outs, axis=1).astype(o_ref.dtype)


def _attn_a(qat, ka, vat):
    bsz, _, s = qat.shape
    lk = ka.shape[1]
    n_chunks = vat.shape[1]
    return pl.pallas_call(
        _attn_a_kernel,
        grid=(bsz, s // TQ_A),
        in_specs=[
            pl.BlockSpec((1, A_Q_COLS, TQ_A), lambda b, i: (b, 0, i)),
            pl.BlockSpec((1, lk, A_KV_COLS), lambda b, i: (b, 0, 0)),
            pl.BlockSpec((1, n_chunks, A_KV_HEADS * A_VROWS, TOK_TILE), lambda b, i: (b, 0, 0, 0)),
        ],
        out_specs=pl.BlockSpec((1, TQ_A, A_Q_COLS), lambda b, i: (b, i, 0)),
        out_shape=jax.ShapeDtypeStruct((bsz, s, A_Q_COLS), BF16),
        scratch_shapes=[pltpu.VMEM((TOK_TILE, A_HEADS * TQ_A), F32)] * 2,
        compiler_params=pltpu.CompilerParams(vmem_limit_bytes=VMEM_LIMIT),
        name="attn_a",
    )(qat, ka, vat)


def _attn_b_kernel(qt_ref, k_ref, vt_ref, lq1_ref, lk1_ref, lq2_ref, lk2_ref, sub_ref, o_ref, s_even, s_odd):
    tq = qt_ref.shape[2]
    zeros = jnp.zeros((HEAD_DIM, tq), BF16)
    q1 = qt_ref[0, :HEAD_DIM, :]
    q2 = qt_ref[0, HEAD_DIM:, :]
    wq = jnp.concatenate([jnp.concatenate([q1, zeros], axis=1), jnp.concatenate([zeros, q2], axis=1)], axis=0)
    acc = _flash(k_ref, vt_ref, wq, lambda t, vc, p: _dot(vc, p), B_V_DIM, s_even, s_odd)
    r = acc[:B_V_DIM] / acc[B_V_DIM:B_V_DIM + 1]
    lam = (jnp.exp(jnp.sum(lq1_ref[...] * lk1_ref[...], axis=1, keepdims=True))
           - jnp.exp(jnp.sum(lq2_ref[...] * lk2_ref[...], axis=1, keepdims=True)) + LAM_INIT)
    o = r[:, :tq] - lam * r[:, tq:]
    ms = jnp.mean(o * o, axis=0, keepdims=True)
    o = o * lax.rsqrt(ms + SUBLN_EPS) * sub_ref[...] * (1.0 - LAM_INIT)
    o_ref[0] = o.T.astype(o_ref.dtype)


def _attn_b(qbt, kb, vbt, lq1, lk1, lq2, lk2, subln):
    bsz, _, s = qbt.shape
    lk = kb.shape[1]
    n_chunks = vbt.shape[1]
    vec = pl.BlockSpec((1, HEAD_DIM), lambda b, h, i: (0, 0))
    return pl.pallas_call(
        _attn_b_kernel,
        grid=(bsz, B_HEADS, s // TQ_B),
        in_specs=[
            pl.BlockSpec((1, 2 * HEAD_DIM, TQ_B), lambda b, h, i: (b, h, i)),
            pl.BlockSpec((1, lk, 2 * HEAD_DIM), lambda b, h, i: (b, 0, h)),
            pl.BlockSpec((1, n_chunks, B_VROWS, TOK_TILE), lambda b, h, i: (b, 0, h, 0)),
            vec, vec, vec, vec,
            pl.BlockSpec((B_V_DIM, 1), lambda b, h, i: (0, 0)),
        ],
        out_specs=pl.BlockSpec((1, TQ_B, B_V_DIM), lambda b, h, i: (b, i, h)),
        out_shape=jax.ShapeDtypeStruct((bsz, s, B_WIDTH), BF16),
        scratch_shapes=[pltpu.VMEM((TOK_TILE, 2 * TQ_B), F32)] * 2,
        compiler_params=pltpu.CompilerParams(vmem_limit_bytes=VMEM_LIMIT),
        name="attn_b",
    )(qbt, kb, vbt, lq1, lk1, lq2, lk2, subln)


def _merge_kernel(tiles_per_sample, x_ref, ya_ref, yb_ref, mod_ref, gna_ref, gnm_ref, wg_ref, woa_ref, wob_ref,
                  wout_ref, rw_ref, rb_ref, utri_ref, x1_ref, h2_ref, route_ref, cnt_ref, carry_ref):
    i = pl.program_id(0)

    @pl.when(i == 0)
    def _():
        carry_ref[...] = jnp.zeros_like(carry_ref)

    mod = mod_ref[i // tiles_per_sample]
    d = x_ref.shape[1]
    subs = [slice(u * TOK_TILE, (u + 1) * TOK_TILE) for u in range(x_ref.shape[0] // TOK_TILE)]
    xs = [x_ref[r, :] for r in subs]
    hs = [(_rms_rows(x, gna_ref[...], EPS) * (1.0 + mod[1:2, :]) + mod[0:1, :]).astype(BF16) for x in xs]
    gates = [_dot(h, wg_ref[...]) for h in hs]
    branch = [(_dot(ya_ref[r, :], woa_ref[...]), _dot(yb_ref[r, :], wob_ref[...])) for r in subs]
    merged = [(jax.nn.sigmoid(g[:, :d]) * pa + jax.nn.sigmoid(g[:, d:]) * pb).astype(BF16)
              for g, (pa, pb) in zip(gates, branch)]
    x1s = [x + mod[2:3, :] * _dot(m, wout_ref[...]) for x, m in zip(xs, merged)]
    h2s = [_rms_rows(x1, gnm_ref[...], EPS) * (1.0 + mod[4:5, :]) + mod[3:4, :] for x1 in x1s]
    for r, x1, h2 in zip(subs, x1s, h2s):
        x1_ref[r, :] = x1
        h2_ref[r, :] = h2

    all_logits = [_dot3_nt(rw_ref[...], h2) + rb_ref[...] for h2 in h2s]
    eiota = lax.broadcasted_iota(I32, (N_EXPERTS, TOK_TILE), 0)
    riota = lax.broadcasted_iota(I32, (ROUTE_ROWS, TOK_TILE), 0)
    carry = carry_ref[...]
    for r, logits in zip(subs, all_logits):
        work = logits
        idxs, vals = [], []
        for _ in range(TOP_K):
            mx = jnp.max(work, axis=0, keepdims=True)
            idx = jnp.min(jnp.where(work == mx, eiota, N_EXPERTS), axis=0, keepdims=True)
            idxs.append(idx)
            vals.append(mx)
            work = jnp.where(eiota == idx, -jnp.inf, work)
        exps = [jnp.exp(v - vals[0]) for v in vals]
        denom = exps[0] + exps[1] + exps[2] + exps[3]
        weights = [e / denom for e in exps]

        hits = [(eiota == idx) for idx in idxs]
        onehot = (hits[0] | hits[1] | hits[2] | hits[3]).astype(F32)
        before = _dot(onehot.astype(BF16), utri_ref[...]) + carry
        ranks = [jnp.sum(jnp.where(hit, before, 0.0), axis=0, keepdims=True) for hit in hits]
        carry = carry + jnp.sum(onehot, axis=1, keepdims=True)

        lines = [v.astype(F32) for v in idxs] + weights + ranks
        route = jnp.zeros((ROUTE_ROWS, TOK_TILE), F32)
        for k, line in enumerate(lines):
            route = jnp.where(riota == k, line, route)
        route_ref[:, r] = route
    carry_ref[...] = carry
    cnt_ref[...] = carry


def _merge(x2, ya2, yb2, mod, gna, gnm, wg, woa, wob, wout, rw, rb, utri, tiles_per_sample):
    t, d = x2.shape
    const2 = lambda i: (0, 0)
    row = lambda i: (i, 0)
    full = lambda a: pl.BlockSpec(a.shape, const2)
    return pl.pallas_call(
        functools.partial(_merge_kernel, tiles_per_sample),
        grid=(t // MERGE_TOK,),
        in_specs=[
            pl.BlockSpec((MERGE_TOK, d), row),
            pl.BlockSpec((MERGE_TOK, ya2.shape[1]), row),
            pl.BlockSpec((MERGE_TOK, yb2.shape[1]), row),
            pl.BlockSpec(mod.shape, lambda i: (0, 0, 0)),
            full(gna), full(gnm), full(wg), full(woa), full(wob), full(wout), full(rw), full(rb), full(utri),
        ],
        out_specs=[
            pl.BlockSpec((MERGE_TOK, d), row),
            pl.BlockSpec((MERGE_TOK, d), row),
            pl.BlockSpec((ROUTE_ROWS, MERGE_TOK), lambda i: (0, i)),
            pl.BlockSpec((N_EXPERTS, 1), const2),
        ],
        out_shape=[
            jax.ShapeDtypeStruct((t, d), F32),
            jax.ShapeDtypeStruct((t, d), F32),
            jax.ShapeDtypeStruct((ROUTE_ROWS, t), F32),
            jax.ShapeDtypeStruct((N_EXPERTS, 1), F32),
        ],
        scratch_shapes=[pltpu.VMEM((N_EXPERTS, 1), F32)],
        compiler_params=pltpu.CompilerParams(vmem_limit_bytes=VMEM_LIMIT),
        name="merge",
    )(x2, ya2, yb2, mod, gna, gnm, wg, woa, wob, wout, rw, rb, utri)


def _to_row_tiles(dst_ref, val):
    n, d = val.shape
    sub = d // LANES
    for c in range(sub):
        dst_ref[pl.ds(c, n, stride=sub), :] = val[:, c * LANES:(c + 1) * LANES]


def _from_row_tiles(src_ref, first_row, n, d):
    sub = d // LANES
    return jnp.concatenate([src_ref[pl.ds(first_row * sub + c, n, stride=sub), :] for c in range(sub)], axis=1)


def _row_tile_copy(src, src_row, dst, dst_row, sub, sem):
    src_at = pl.ds(pl.multiple_of(src_row * sub, sub), sub)
    dst_at = pl.ds(pl.multiple_of(dst_row * sub, sub), sub)
    return pltpu.make_async_copy(src.at[src_at], dst.at[dst_at], sem)


def _dispatch_kernel(pos_ref, padfrom_ref, padcnt_ref, nu_ref, h_ref, xb_out, tiles, zeros, sem, zsem):
    i = pl.program_id(0)
    n = pl.num_programs(0)
    slot = i % 2
    n_tok, d = h_ref.shape
    sub = d // LANES
    blk = zeros.shape[0]
    n_blocks = xb_out.shape[0] // blk

    def zero_fill(wait):
        def go(copy):
            copy.wait() if wait else copy.start()

        def per_expert(e, carry):
            cnt = padcnt_ref[e]
            for bit in reversed(range((FFN_ROWS - 1).bit_length())):
                rows = 1 << bit

                @pl.when((cnt & rows) != 0)
                def _():
                    done = (cnt >> (bit + 1)) << (bit + 1)
                    at = pl.multiple_of((padfrom_ref[e] + done) * sub, sub)
                    go(pltpu.make_async_copy(zeros.at[pl.ds(0, rows * sub)], xb_out.at[pl.ds(at, rows * sub)], zsem))
            return carry

        lax.fori_loop(0, N_EXPERTS, per_expert, 0)

        def per_block(j, carry):
            go(pltpu.make_async_copy(zeros, xb_out.at[pl.ds(pl.multiple_of(j * blk, blk), blk)], zsem))
            return carry

        lax.fori_loop(nu_ref[0], n_blocks, per_block, 0)

    @pl.when(i == 0)
    def _():
        zeros[...] = jnp.zeros_like(zeros)
        zero_fill(wait=False)
        zero_fill(wait=True)

    def drain(sl):
        for _ in range(TOP_K):
            pltpu.make_async_copy(tiles.at[sl], xb_out.at[pl.ds(0, n_tok * sub)], sem.at[sl]).wait()

    @pl.when(i >= 2)
    def _():
        drain(slot)

    _to_row_tiles(tiles.at[slot], h_ref[...])
    base = i * n_tok
    n_all = n * n_tok

    def issue(t, carry):
        for k in range(TOP_K):
            _row_tile_copy(tiles.at[slot], t, xb_out, pos_ref[k * n_all + base + t], sub,
                           sem.at[slot]).start(priority=k % DMA_QUEUES)
        return carry

    lax.fori_loop(0, n_tok, issue, 0, unroll=ISSUE_UNROLL // TOP_K)

    @pl.when((i == n - 1) & (n >= 2))
    def _():
        drain(1 - slot)

    @pl.when(i == n - 1)
    def _():
        drain(slot)


def _dispatch(pos_flat, pad_from, pad_cnt, n_used, h2, n_blocks):
    t, d = h2.shape
    sub = d // LANES
    return pl.pallas_call(
        _dispatch_kernel,
        grid_spec=pltpu.PrefetchScalarGridSpec(
            num_scalar_prefetch=4,
            grid=(t // TOK_TILE,),
            in_specs=[pl.BlockSpec((TOK_TILE, d), lambda i, *_: (i, 0))],
            out_specs=pl.BlockSpec(memory_space=pl.ANY),
            scratch_shapes=[pltpu.VMEM((2, TOK_TILE * sub, LANES), F32), pltpu.VMEM((FFN_ROWS * sub, LANES), F32),
                            pltpu.SemaphoreType.DMA((2,)), pltpu.SemaphoreType.DMA(())],
        ),
        out_shape=jax.ShapeDtypeStruct((n_blocks * FFN_ROWS * sub, LANES), F32),
        compiler_params=pltpu.CompilerParams(vmem_limit_bytes=VMEM_LIMIT),
        name="dispatch",
    )(pos_flat, pad_from, pad_cnt, n_used, h2)


def _ffn_kernel(be_ref, nu_ref, nxt_ref, x_ref, wg_hbm, bg_ref, wu_hbm, bu_ref, wd_hbm, bd_ref, o_ref,
                stage_g, stage_u, stage_d, wg_s, wu_s, wd_s, wsem):
    j = pl.program_id(0)
    prev = be_ref[jnp.maximum(j - 1, 0)]
    fresh = ((j == 0) | (be_ref[j] != prev)) & (j < nu_ref[0])
    streams = ((wg_hbm, stage_g, wg_s), (wu_hbm, stage_u, wu_s), (wd_hbm, stage_d, wd_s))

    def weight_copies(expert):
        return [pltpu.make_async_copy(hbm.at[expert], stage, wsem.at[n]) for n, (hbm, stage, _) in enumerate(streams)]

    @pl.when(j == 0)
    def _():
        for copy in weight_copies(be_ref[0]):
            copy.start()

    @pl.when(fresh)
    def _():
        for copy, (_, stage, dst) in zip(weight_copies(be_ref[j]), streams):
            copy.wait()
            dst[...] = stage[...].astype(BF16)

        @pl.when(nxt_ref[j] >= 0)
        def _():
            for copy in weight_copies(nxt_ref[j]):
                copy.start()

    @pl.when(j < nu_ref[0])
    def _():
        x = _from_row_tiles(x_ref, 0, FFN_ROWS, wg_s.shape[0]).astype(BF16)
        gate = jnp.minimum(_dot(x, wg_s[...]) + bg_ref[0], SWIGLU_LIMIT)
        up = jnp.clip(_dot(x, wu_s[...]) + bu_ref[0], -SWIGLU_LIMIT, SWIGLU_LIMIT)
        act = (up + 1.0) * (gate * jax.nn.sigmoid(SWIGLU_ALPHA * gate))
        _to_row_tiles(o_ref, _dot(act.astype(BF16), wd_s[...]) + bd_ref[0])

    @pl.when(j >= nu_ref[0])
    def _():
        o_ref[...] = jnp.zeros_like(o_ref)


def _ffn(block_e, n_used, next_e, xb, w_gate, b_gate, w_up, b_up, w_down, b_down):
    d, de = w_gate.shape[1:]
    blk = FFN_ROWS * d // LANES
    n_blocks = xb.shape[0] // blk
    xrow = lambda j, be, nu, nxt: (jnp.minimum(j, nu[0] - 1), 0)
    bsel = lambda j, be, nu, nxt: (be[j], 0, 0)
    any_spec = pl.BlockSpec(memory_space=pl.ANY)
    return pl.pallas_call(
        _ffn_kernel,
        grid_spec=pltpu.PrefetchScalarGridSpec(
            num_scalar_prefetch=3,
            grid=(n_blocks,),
            in_specs=[
                pl.BlockSpec((blk, LANES), xrow),
                any_spec,
                pl.BlockSpec((1, 1, de), bsel),
                any_spec,
                pl.BlockSpec((1, 1, de), bsel),
                any_spec,
                pl.BlockSpec((1, 1, d), bsel),
            ],
            out_specs=pl.BlockSpec((blk, LANES), lambda j, be, nu, nxt: (j, 0)),
            scratch_shapes=[pltpu.VMEM((d, de), F32), pltpu.VMEM((d, de), F32), pltpu.VMEM((de, d), F32),
                            pltpu.VMEM((d, de), BF16), pltpu.VMEM((d, de), BF16), pltpu.VMEM((de, d), BF16),
                            pltpu.SemaphoreType.DMA((3,))],
        ),
        out_shape=jax.ShapeDtypeStruct(xb.shape, F32),
        compiler_params=pltpu.CompilerParams(vmem_limit_bytes=VMEM_LIMIT),
        name="ffn",
    )(block_e, n_used, next_e, xb, w_gate, b_gate, w_up, b_up, w_down, b_down)


def _combine_kernel(tiles_per_sample, pos_ref, x1_ref, route_ref, mod_ref, yb_hbm, o_ref, buf, sem):
    i = pl.program_id(0)
    n = pl.num_programs(0)
    rows = TOP_K * CMB_TOK
    d = x1_ref.shape[1]
    sub = d // LANES

    def issue(tile, slot):
        def body(g, carry):
            for u in range(ISSUE_UNROLL):
                r = g * ISSUE_UNROLL + u
                _row_tile_copy(yb_hbm, pos_ref[tile * rows + r], buf.at[slot], r, sub,
                               sem.at[slot]).start(priority=u % DMA_QUEUES)
            return carry
        lax.fori_loop(0, rows // ISSUE_UNROLL, body, 0)

    @pl.when(i == 0)
    def _():
        issue(0, 0)

    @pl.when(i + 1 < n)
    def _():
        issue(i + 1, (i + 1) % 2)

    slot = i % 2
    pltpu.make_async_copy(yb_hbm.at[pl.ds(0, rows * sub)], buf.at[slot], sem.at[slot]).wait()

    b = i // tiles_per_sample
    g_m = mod_ref[b][5:6, :]
    route = route_ref[...]
    acc = jnp.zeros(x1_ref.shape, F32)
    for k in range(TOP_K):
        acc = acc + route[:, TOP_K + k:TOP_K + k + 1] * _from_row_tiles(buf.at[slot], k * CMB_TOK, CMB_TOK, d)
    o_ref[...] = x1_ref[...] + g_m * acc


def _combine(pos_tiles, x1, route, mod, yb, tiles_per_sample):
    t, d = x1.shape
    row = lambda i, pos: (i, 0)
    return pl.pallas_call(
        functools.partial(_combine_kernel, tiles_per_sample),
        grid_spec=pltpu.PrefetchScalarGridSpec(
            num_scalar_prefetch=1,
            grid=(t // CMB_TOK,),
            in_specs=[
                pl.BlockSpec((CMB_TOK, d), row),
                pl.BlockSpec((CMB_TOK, route.shape[1]), row),
                pl.BlockSpec(mod.shape, lambda i, pos: (0, 0, 0)),
                pl.BlockSpec(memory_space=pl.ANY),
            ],
            out_specs=pl.BlockSpec((CMB_TOK, d), row),
            scratch_shapes=[pltpu.VMEM((2, TOP_K * CMB_TOK * d // LANES, LANES), F32),
                            pltpu.SemaphoreType.DMA((2,))],
        ),
        out_shape=jax.ShapeDtypeStruct((t, d), F32),
        compiler_params=pltpu.CompilerParams(vmem_limit_bytes=VMEM_LIMIT),
        name="combine",
    )(pos_tiles, x1, route, mod, yb)


def _rope_tables(s):
    rows = s // GRID_W
    row = jnp.repeat(jnp.arange(rows, dtype=F32), GRID_W)
    col = jnp.tile(jnp.arange(GRID_W, dtype=F32), rows)
    inv = 1.0 / (ROPE_THETA ** (jnp.arange(AXIS_PAIRS, dtype=F32) / AXIS_PAIRS))
    ang = jnp.concatenate([row[:, None] * inv, col[:, None] * inv], axis=-1)
    cos, sin = jnp.cos(ang), jnp.sin(ang)
    cos_i = jnp.repeat(cos, 2, axis=-1)
    sin_i = jnp.stack([-sin, sin], axis=-1).reshape(s, HEAD_DIM)
    reps = LANES // HEAD_DIM
    return jnp.tile(cos_i, (1, reps)), jnp.tile(sin_i, (1, reps))


def kernel(x, c, ctx, c_ctx, w_ada, b_ada, norm_attn, w_in, q_norm_a, k_norm_a, q_norm_b, k_norm_b, lambda_q1,
           lambda_k1, lambda_q2, lambda_k2, subln_b, w_oa, w_ob, w_out, norm_mlp, router_w, router_b, w_gate,
           b_gate, w_up, b_up, w_down, b_down):
    bsz, s, d = x.shape
    assert ctx.shape[1] == TOK_TILE and s % MERGE_TOK == 0 and s % TQ_A == 0 and s % TQ_B == 0 and s % GRID_W == 0
    assert w_ada.shape[0] == 1, "single-layer block"
    t = bsz * s
    assert t % CMB_TOK == 0 and (t * TOP_K) % FFN_ROWS == 0

    mod_rows = 8
    assert bsz + 1 <= mod_rows
    cvec = jnp.concatenate([c, c_ctx[None, :], jnp.zeros((mod_rows - bsz - 1, d), F32)], axis=0)
    mod = _adaln(cvec, w_ada[0], b_ada[0][None, :]).reshape(mod_rows, N_MOD, d)

    cos, sin = _rope_tables(s)
    tile_gain = lambda g: jnp.tile(g, LANES // HEAD_DIM)[None, :]
    gidx = jnp.arange(MXU_DIM) // HEAD_DIM
    gmat = jnp.where(gidx[:, None] == gidx[None, :], 1.0 / HEAD_DIM, 0.0).astype(BF16)
    w_qkv = w_in[0][:, :QKV_COLS].astype(BF16)
    w_g = w_in[0][:, QKV_COLS:].astype(BF16)

    qat, ka, vat, qbt, kb, vbt = _project(
        x, ctx, mod, norm_attn, w_qkv, cos, sin, tile_gain(q_norm_a[0]), tile_gain(k_norm_a[0]),
        tile_gain(q_norm_b[0]), tile_gain(k_norm_b[0]), gmat)

    ya = _attn_a(qat, ka, vat)
    yb = _attn_b(qbt, kb, vbt, lambda_q1, lambda_k1, lambda_q2, lambda_k2, subln_b[0][:, None])

    tri = jnp.arange(TOK_TILE)
    utri = (tri[:, None] < tri[None, :]).astype(BF16)
    x1, h2, route_t, counts = _merge(
        x.reshape(t, d), ya.reshape(t, A_Q_COLS), yb.reshape(t, B_WIDTH), mod, norm_attn, norm_mlp, w_g,
        w_oa[0].astype(BF16), w_ob[0].astype(BF16), w_out[0].astype(BF16), router_w[0].T, router_b[0][:, None], utri,
        s // MERGE_TOK)

    counts = counts[:, 0].astype(I32)
    padded = (counts + FFN_ROWS - 1) // FFN_ROWS * FFN_ROWS
    pad_end = jnp.cumsum(padded)
    pad_start = pad_end - padded
    top_e = route_t[:TOP_K].astype(I32)
    rank = route_t[2 * TOP_K:3 * TOP_K].astype(I32)
    experts = jnp.arange(N_EXPERTS, dtype=I32)
    start_of = jnp.sum(jnp.where(top_e[..., None] == experts, pad_start, 0), axis=-1)
    pos_t = start_of + rank
    n_blocks = t * TOP_K // FFN_ROWS + N_EXPERTS
    block_start = jnp.arange(n_blocks, dtype=I32) * FFN_ROWS
    block_e = jnp.minimum(jnp.sum((pad_end[None, :] <= block_start[:, None]).astype(I32), axis=1), N_EXPERTS - 1)
    n_used = (pad_end[-1:] // FFN_ROWS).astype(I32)
    first_at_or_after = lax.cummin(jnp.where(padded > 0, jnp.arange(N_EXPERTS, dtype=I32), N_EXPERTS), reverse=True)
    next_nonempty = jnp.concatenate([first_at_or_after[1:], jnp.full((1,), N_EXPERTS, I32)])
    next_of = jnp.where(next_nonempty < N_EXPERTS, next_nonempty, -1)
    next_e = jnp.sum(jnp.where(block_e[:, None] == experts, next_of, 0), axis=-1)

    xb = _dispatch(pos_t.reshape(-1), pad_start + counts, padded - counts, n_used, h2, n_blocks)
    de = w_gate.shape[-1]
    yb_rows = _ffn(block_e, n_used, next_e, xb, w_gate[0], b_gate[0].reshape(N_EXPERTS, 1, de), w_up[0],
                   b_up[0].reshape(N_EXPERTS, 1, de), w_down[0], b_down[0].reshape(N_EXPERTS, 1, d))
    pos_tiles = pos_t.reshape(TOP_K, t // CMB_TOK, CMB_TOK).transpose(1, 0, 2).reshape(-1)
    out = _combine(pos_tiles, x1, route_t.T, mod, yb_rows, s // CMB_TOK)
    return out.reshape(bsz, s, d)
```

```python
import functools
import math

import jax
import jax.numpy as jnp
from jax import lax
from jax.experimental import pallas as pl
from jax.experimental.pallas import tpu as pltpu

F32 = jnp.float32
BF16 = jnp.bfloat16
I32 = jnp.int32

GRID_W = 64
HEAD_DIM = 64
A_HEADS = 8
A_KV_HEADS = 2
A_GROUP = A_HEADS // A_KV_HEADS
B_HEADS = 4
B_V_DIM = 2 * HEAD_DIM
N_EXPERTS = 32
TOP_K = 4
N_MOD = 6
ROPE_THETA = 10000.0
AXIS_PAIRS = HEAD_DIM // 4
SWIGLU_LIMIT = 7.0
SWIGLU_ALPHA = 1.702
EPS = 1e-6
SUBLN_EPS = 1e-5
LAM_INIT = 0.8 - 0.6 * math.exp(-0.3 * 0)

A_Q_COLS = A_HEADS * HEAD_DIM
A_KV_COLS = A_KV_HEADS * HEAD_DIM
B_QK_COLS = B_HEADS * 2 * HEAD_DIM
B_WIDTH = B_HEADS * B_V_DIM
QKV_COLS = A_Q_COLS + 2 * A_KV_COLS + 2 * B_QK_COLS + B_WIDTH

LANES = 128
MXU_DIM = 256
TOK_TILE = 256
MERGE_TOK = 4 * TOK_TILE
TQ_A = 256
TQ_B = 1024
SCORE_TILES_PER_TRIP = 256
ONES_ROWS = 16
A_VROWS = HEAD_DIM + ONES_ROWS
B_VROWS = B_V_DIM + ONES_ROWS
ADA_COLS = 1536
FFN_ROWS = 256
CMB_TOK = 128
ROUTE_ROWS = 16
ISSUE_UNROLL = 16
DMA_QUEUES = 2
VMEM_LIMIT = 56 * 1024 * 1024


def _dot(a, b):
    return jnp.dot(a, b, preferred_element_type=F32)


def _split_bf16(a):
    hi = a.astype(BF16)
    lo = (a - hi.astype(F32)).astype(BF16)
    return hi, lo


def _dot3(a, b):
    a_hi, a_lo = _split_bf16(a)
    b_hi, b_lo = _split_bf16(b)
    return _dot(a_hi, b_hi) + (_dot(a_hi, b_lo) + _dot(a_lo, b_hi))


def _dot3_nt(a, b):
    dims = (((1,), (1,)), ((), ()))
    dot_nt = lambda u, v: lax.dot_general(u, v, dims, preferred_element_type=F32)
    a_hi, a_lo = _split_bf16(a)
    b_hi, b_lo = _split_bf16(b)
    return dot_nt(a_hi, b_hi) + (dot_nt(a_hi, b_lo) + dot_nt(a_lo, b_hi))


def _rms_rows(x, gain, eps):
    ms = jnp.mean(x * x, axis=-1, keepdims=True)
    return x * lax.rsqrt(ms + eps) * gain


def _adaln_kernel(c_ref, w_ref, b_ref, o_ref):
    c = c_ref[...]
    a = c * jax.nn.sigmoid(c)
    o_ref[...] = _dot3(a, w_ref[...]) + b_ref[...]


def _adaln(cvec, w_ada, b_ada):
    rows, d = cvec.shape
    n = w_ada.shape[1]
    return pl.pallas_call(
        _adaln_kernel,
        grid=(n // ADA_COLS,),
        in_specs=[
            pl.BlockSpec((rows, d), lambda j: (0, 0)),
            pl.BlockSpec((d, ADA_COLS), lambda j: (0, j)),
            pl.BlockSpec((1, ADA_COLS), lambda j: (0, j)),
        ],
        out_specs=pl.BlockSpec((rows, ADA_COLS), lambda j: (0, j)),
        out_shape=jax.ShapeDtypeStruct((rows, n), F32),
        compiler_params=pltpu.CompilerParams(vmem_limit_bytes=VMEM_LIMIT),
        name="adaln",
    )(cvec, w_ada, b_ada)


def _group_rms(y, gmat, gain):
    w_total = y.shape[1]
    outs = []
    for c0 in range(0, w_total, MXU_DIM):
        w = min(MXU_DIM, w_total - c0)
        yc = y[:, c0:c0 + w]
        ms = _dot((yc * yc).astype(BF16), gmat[:w, :w])
        reps = w // LANES
        g = gain if reps == 1 else jnp.concatenate([gain] * reps, axis=1)
        outs.append(yc * lax.rsqrt(ms + EPS) * g)
    return outs[0] if len(outs) == 1 else jnp.concatenate(outs, axis=1)


def _rope(y, cos, sin):
    rows = y.shape[0]
    lane = lax.broadcasted_iota(I32, (rows, LANES), 1)
    even = (lane % 2) == 0
    outs = []
    for c in range(y.shape[1] // LANES):
        ch = y[:, c * LANES:(c + 1) * LANES]
        partner = jnp.where(even, pltpu.roll(ch, LANES - 1, 1), pltpu.roll(ch, 1, 1))
        outs.append(ch * cos + partner * sin)
    return outs[0] if len(outs) == 1 else jnp.concatenate(outs, axis=1)


def _proj_kernel(x_ref, ctx_ref, mod_ref, gn_ref, w_ref, cos_ref, sin_ref, qna_ref, kna_ref, qnb_ref, knb_ref,
                 gmat_ref, qat_ref, ka_ref, vat_ref, qbt_ref, kb_ref, vbt_ref):
    b = pl.program_id(0)
    j = pl.program_id(1)
    is_ctx = j == 0
    xt = jnp.where(is_ctx, ctx_ref[0], x_ref[0])
    mrow = jnp.where(is_ctx, pl.num_programs(0), b)
    mod = mod_ref[mrow]
    h = _rms_rows(xt, gn_ref[...], EPS) * (1.0 + mod[1:2, :]) + mod[0:1, :]
    y = _dot(h.astype(BF16), w_ref[...])
    cos = jnp.where(is_ctx, 1.0, cos_ref[...])
    sin = jnp.where(is_ctx, 0.0, sin_ref[...])
    gmat = gmat_ref[...]
    scale = HEAD_DIM ** -0.5 * math.log2(math.e)

    o = 0
    qa = _rope(_group_rms(y[:, o:o + A_Q_COLS], gmat, qna_ref[...]), cos, sin) * scale
    o += A_Q_COLS
    ka = _rope(_group_rms(y[:, o:o + A_KV_COLS], gmat, kna_ref[...]), cos, sin)
    o += A_KV_COLS
    va = y[:, o:o + A_KV_COLS]
    o += A_KV_COLS
    qb = _rope(_group_rms(y[:, o:o + B_QK_COLS], gmat, qnb_ref[...]), cos, sin) * scale
    o += B_QK_COLS
    kb = _rope(_group_rms(y[:, o:o + B_QK_COLS], gmat, knb_ref[...]), cos, sin)
    o += B_QK_COLS
    vb = y[:, o:o + B_WIDTH]

    @pl.when(j > 0)
    def _():
        qat_ref[0] = qa.T.astype(BF16)
        qbt_ref[0] = qb.T.astype(BF16)

    ka_ref[0] = ka.astype(BF16)
    kb_ref[0] = kb.astype(BF16)
    ones = jnp.ones((ONES_ROWS, xt.shape[0]), BF16)
    for dst_ref, vt, width in ((vat_ref, va.T.astype(BF16), HEAD_DIM), (vbt_ref, vb.T.astype(BF16), B_V_DIM)):
        stride = width + ONES_ROWS
        for hd in range(vt.shape[0] // width):
            dst_ref[0, 0, hd * stride:hd * stride + width, :] = vt[hd * width:(hd + 1) * width, :]
            dst_ref[0, 0, hd * stride + width:(hd + 1) * stride, :] = ones


def _project(x, ctx, mod, gn, w_qkv, cos, sin, qna, kna, qnb, knb, gmat):
    bsz, s, d = x.shape
    n_lat = s // TOK_TILE
    n_key = n_lat + 1
    lk = n_key * TOK_TILE
    lat = lambda b, j: (b, jnp.maximum(j - 1, 0), 0)
    const2 = lambda b, j: (0, 0)
    tab = lambda b, j: (jnp.maximum(j - 1, 0), 0)
    return pl.pallas_call(
        _proj_kernel,
        grid=(bsz, n_key),
        in_specs=[
            pl.BlockSpec((1, TOK_TILE, d), lat),
            pl.BlockSpec((1, TOK_TILE, d), lambda b, j: (b, 0, 0)),
            pl.BlockSpec(mod.shape, lambda b, j: (0, 0, 0)),
            pl.BlockSpec((1, d), const2),
            pl.BlockSpec((d, QKV_COLS), const2),
            pl.BlockSpec((TOK_TILE, LANES), tab),
            pl.BlockSpec((TOK_TILE, LANES), tab),
            pl.BlockSpec((1, LANES), const2),
            pl.BlockSpec((1, LANES), const2),
            pl.BlockSpec((1, LANES), const2),
            pl.BlockSpec((1, LANES), const2),
            pl.BlockSpec((MXU_DIM, MXU_DIM), const2),
        ],
        out_specs=[
            pl.BlockSpec((1, A_Q_COLS, TOK_TILE), lambda b, j: (b, 0, jnp.maximum(j - 1, 0))),
            pl.BlockSpec((1, TOK_TILE, A_KV_COLS), lambda b, j: (b, j, 0)),
            pl.BlockSpec((1, 1, A_KV_HEADS * A_VROWS, TOK_TILE), lambda b, j: (b, j, 0, 0)),
            pl.BlockSpec((1, B_QK_COLS, TOK_TILE), lambda b, j: (b, 0, jnp.maximum(j - 1, 0))),
            pl.BlockSpec((1, TOK_TILE, B_QK_COLS), lambda b, j: (b, j, 0)),
            pl.BlockSpec((1, 1, B_HEADS * B_VROWS, TOK_TILE), lambda b, j: (b, j, 0, 0)),
        ],
        out_shape=[
            jax.ShapeDtypeStruct((bsz, A_Q_COLS, s), BF16),
            jax.ShapeDtypeStruct((bsz, lk, A_KV_COLS), BF16),
            jax.ShapeDtypeStruct((bsz, n_key, A_KV_HEADS * A_VROWS, TOK_TILE), BF16),
            jax.ShapeDtypeStruct((bsz, B_QK_COLS, s), BF16),
            jax.ShapeDtypeStruct((bsz, lk, B_QK_COLS), BF16),
            jax.ShapeDtypeStruct((bsz, n_key, B_HEADS * B_VROWS, TOK_TILE), BF16),
        ],
        compiler_params=pltpu.CompilerParams(vmem_limit_bytes=VMEM_LIMIT),
        name="proj",
    )(x, ctx, mod, gn, w_qkv, cos, sin, qna, kna, qnb, knb, gmat)


def _flash(k_ref, vt_ref, wq, pv, dv, s_even, s_odd):
    n = wq.shape[1]
    n_tiles = n // MXU_DIM
    n_chunks = vt_ref.shape[1]
    cols = [slice(t * MXU_DIM, (t + 1) * MXU_DIM) for t in range(n_tiles)]
    wqs = [wq[:, cs] for cs in cols]

    def scores(c, t, dst):
        start = pl.multiple_of(c * TOK_TILE, TOK_TILE)
        dst[:, cols[t]] = _dot(k_ref[0, pl.ds(start, TOK_TILE), :], wqs[t])

    def chunk(c, cur, nxt, state):
        out = []
        for t in range(n_tiles):
            if nxt is not None:
                scores(c + 1, t, nxt)
            m, acc = state[t]
            s = cur[:, cols[t]]
            m_new = jnp.maximum(m, jnp.max(s, axis=0, keepdims=True))
            alpha = jnp.exp2(m - m_new)
            p = jnp.exp2(s - m_new).astype(BF16)
            out.append((m_new, alpha * acc + pv(t, vt_ref[0, c], p)))
        return tuple(out)

    for t in range(n_tiles):
        scores(0, t, s_even)

    bufs = (s_even, s_odd)

    per_trip = SCORE_TILES_PER_TRIP // n_tiles
    assert per_trip % 2 == 0, "the two score buffers alternate, so a trip must hold an even number of chunks"

    def trip(i, state):
        for u in range(per_trip):
            state = chunk(i * per_trip + u, bufs[u % 2], bufs[(u + 1) % 2], state)
        return state

    init = tuple((jnp.full((1, MXU_DIM), -jnp.inf, F32), jnp.zeros((dv + ONES_ROWS, MXU_DIM), F32))
                 for _ in range(n_tiles))
    n_trips = (n_chunks - 1) // per_trip
    state = lax.fori_loop(0, n_trips, trip, init)
    for c in range(n_trips * per_trip, n_chunks):
        state = chunk(c, bufs[c % 2], bufs[(c + 1) % 2] if c + 1 < n_chunks else None, state)
    return jnp.concatenate([acc for _, acc in state], axis=1)


def _attn_a_kernel(qt_ref, k_ref, vt_ref, o_ref, s_even, s_odd):
    tq = qt_ref.shape[2]
    half = A_GROUP * tq
    zeros = jnp.zeros((HEAD_DIM, half), BF16)
    rows = []
    for g in range(A_KV_HEADS):
        heads = range(g * A_GROUP, (g + 1) * A_GROUP)
        qg = jnp.concatenate([qt_ref[0, h * HEAD_DIM:(h + 1) * HEAD_DIM, :] for h in heads], axis=1)
        rows.append(jnp.concatenate([qg if gg == g else zeros for gg in range(A_KV_HEADS)], axis=1))
    wq = jnp.concatenate(rows, axis=0)

    def pv(t, vc, p):
        g = t * MXU_DIM // half
        return _dot(vc[g * A_VROWS:(g + 1) * A_VROWS], p)

    acc = _flash(k_ref, vt_ref, wq, pv, HEAD_DIM, s_even, s_odd)
    o = acc[:HEAD_DIM] / acc[HEAD_DIM:HEAD_DIM + 1]
    outs = [o[:, h * tq:(h + 1) * tq].T for h in range(A_HEADS)]
    o_ref[0] = jnp.concatenate(outs, axis=1).astype(o_ref.dtype)


def _attn_a(qat, ka, vat):
    bsz, _, s = qat.shape
    lk = ka.shape[1]
    n_chunks = vat.shape[1]
    return pl.pallas_call(
        _attn_a_kernel,
        grid=(bsz, s // TQ_A),
        in_specs=[
            pl.BlockSpec((1, A_Q_COLS, TQ_A), lambda b, i: (b, 0, i)),
            pl.BlockSpec((1, lk, A_KV_COLS), lambda b, i: (b, 0, 0)),
            pl.BlockSpec((1, n_chunks, A_KV_HEADS * A_VROWS, TOK_TILE), lambda b, i: (b, 0, 0, 0)),
        ],
        out_specs=pl.BlockSpec((1, TQ_A, A_Q_COLS), lambda b, i: (b, i, 0)),
        out_shape=jax.ShapeDtypeStruct((bsz, s, A_Q_COLS), BF16),
        scratch_shapes=[pltpu.VMEM((TOK_TILE, A_HEADS * TQ_A), F32)] * 2,
        compiler_params=pltpu.CompilerParams(vmem_limit_bytes=VMEM_LIMIT),
        name="attn_a",
    )(qat, ka, vat)


def _attn_b_kernel(qt_ref, k_ref, vt_ref, lq1_ref, lk1_ref, lq2_ref, lk2_ref, sub_ref, o_ref, s_even, s_odd):
    tq = qt_ref.shape[2]
    zeros = jnp.zeros((HEAD_DIM, tq), BF16)
    q1 = qt_ref[0, :HEAD_DIM, :]
    q2 = qt_ref[0, HEAD_DIM:, :]
    wq = jnp.concatenate([jnp.concatenate([q1, zeros], axis=1), jnp.concatenate([zeros, q2], axis=1)], axis=0)
    acc = _flash(k_ref, vt_ref, wq, lambda t, vc, p: _dot(vc, p), B_V_DIM, s_even, s_odd)
    r = acc[:B_V_DIM] / acc[B_V_DIM:B_V_DIM + 1]
    lam = (jnp.exp(jnp.sum(lq1_ref[...] * lk1_ref[...], axis=1, keepdims=True))
           - jnp.exp(jnp.sum(lq2_ref[...] * lk2_ref[...], axis=1, keepdims=True)) + LAM_INIT)
    o = r[:, :tq] - lam * r[:, tq:]
    ms = jnp.mean(o * o, axis=0, keepdims=True)
    o = o * lax.rsqrt(ms + SUBLN_EPS) * sub_ref[...] * (1.0 - LAM_INIT)
    o_ref[0] = o.T.astype(o_ref.dtype)


def _attn_b(qbt, kb, vbt, lq1, lk1, lq2, lk2, subln):
    bsz, _, s = qbt.shape
    lk = kb.shape[1]
    n_chunks = vbt.shape[1]
    vec = pl.BlockSpec((1, HEAD_DIM), lambda b, h, i: (0, 0))
    return pl.pallas_call(
        _attn_b_kernel,
        grid=(bsz, B_HEADS, s // TQ_B),
        in_specs=[
            pl.BlockSpec((1, 2 * HEAD_DIM, TQ_B), lambda b, h, i: (b, h, i)),
            pl.BlockSpec((1, lk, 2 * HEAD_DIM), lambda b, h, i: (b, 0, h)),
            pl.BlockSpec((1, n_chunks, B_VROWS, TOK_TILE), lambda b, h, i: (b, 0, h, 0)),
            vec, vec, vec, vec,
            pl.BlockSpec((B_V_DIM, 1), lambda b, h, i: (0, 0)),
        ],
        out_specs=pl.BlockSpec((1, TQ_B, B_V_DIM), lambda b, h, i: (b, i, h)),
        out_shape=jax.ShapeDtypeStruct((bsz, s, B_WIDTH), BF16),
        scratch_shapes=[pltpu.VMEM((TOK_TILE, 2 * TQ_B), F32)] * 2,
        compiler_params=pltpu.CompilerParams(vmem_limit_bytes=VMEM_LIMIT),
        name="attn_b",
    )(qbt, kb, vbt, lq1, lk1, lq2, lk2, subln)


def _merge_kernel(tiles_per_sample, x_ref, ya_ref, yb_ref, mod_ref, gna_ref, gnm_ref, wg_ref, woa_ref, wob_ref,
                  wout_ref, rw_ref, rb_ref, utri_ref, x1_ref, h2_ref, route_ref, cnt_ref, carry_ref):
    i = pl.program_id(0)

    @pl.when(i == 0)
    def _():
        carry_ref[...] = jnp.zeros_like(carry_ref)

    mod = mod_ref[i // tiles_per_sample]
    d = x_ref.shape[1]
    subs = [slice(u * TOK_TILE, (u + 1) * TOK_TILE) for u in range(x_ref.shape[0] // TOK_TILE)]
    xs = [x_ref[r, :] for r in subs]
    hs = [(_rms_rows(x, gna_ref[...], EPS) * (1.0 + mod[1:2, :]) + mod[0:1, :]).astype(BF16) for x in xs]
    gates = [_dot(h, wg_ref[...]) for h in hs]
    branch = [(_dot(ya_ref[r, :], woa_ref[...]), _dot(yb_ref[r, :], wob_ref[...])) for r in subs]
    merged = [(jax.nn.sigmoid(g[:, :d]) * pa + jax.nn.sigmoid(g[:, d:]) * pb).astype(BF16)
              for g, (pa, pb) in zip(gates, branch)]
    x1s = [x + mod[2:3, :] * _dot(m, wout_ref[...]) for x, m in zip(xs, merged)]
    h2s = [_rms_rows(x1, gnm_ref[...], EPS) * (1.0 + mod[4:5, :]) + mod[3:4, :] for x1 in x1s]
    for r, x1, h2 in zip(subs, x1s, h2s):
        x1_ref[r, :] = x1
        h2_ref[r, :] = h2

    all_logits = [_dot3_nt(rw_ref[...], h2) + rb_ref[...] for h2 in h2s]
    eiota = lax.broadcasted_iota(I32, (N_EXPERTS, TOK_TILE), 0)
    riota = lax.broadcasted_iota(I32, (ROUTE_ROWS, TOK_TILE), 0)
    carry = carry_ref[...]
    for r, logits in zip(subs, all_logits):
        work = logits
        idxs, vals = [], []
        for _ in range(TOP_K):
            mx = jnp.max(work, axis=0, keepdims=True)
            idx = jnp.min(jnp.where(work == mx, eiota, N_EXPERTS), axis=0, keepdims=True)
            idxs.append(idx)
            vals.append(mx)
            work = jnp.where(eiota == idx, -jnp.inf, work)
        exps = [jnp.exp(v - vals[0]) for v in vals]
        denom = exps[0] + exps[1] + exps[2] + exps[3]
        weights = [e / denom for e in exps]

        hits = [(eiota == idx) for idx in idxs]
        onehot = (hits[0] | hits[1] | hits[2] | hits[3]).astype(F32)
        before = _dot(onehot.astype(BF16), utri_ref[...]) + carry
        ranks = [jnp.sum(jnp.where(hit, before, 0.0), axis=0, keepdims=True) for hit in hits]
        carry = carry + jnp.sum(onehot, axis=1, keepdims=True)

        lines = [v.astype(F32) for v in idxs] + weights + ranks
        route = jnp.zeros((ROUTE_ROWS, TOK_TILE), F32)
        for k, line in enumerate(lines):
            route = jnp.where(riota == k, line, route)
        route_ref[:, r] = route
    carry_ref[...] = carry
    cnt_ref[...] = carry


def _merge(x2, ya2, yb2, mod, gna, gnm, wg, woa, wob, wout, rw, rb, utri, tiles_per_sample):
    t, d = x2.shape
    const2 = lambda i: (0, 0)
    row = lambda i: (i, 0)
    full = lambda a: pl.BlockSpec(a.shape, const2)
    return pl.pallas_call(
        functools.partial(_merge_kernel, tiles_per_sample),
        grid=(t // MERGE_TOK,),
        in_specs=[
            pl.BlockSpec((MERGE_TOK, d), row),
            pl.BlockSpec((MERGE_TOK, ya2.shape[1]), row),
            pl.BlockSpec((MERGE_TOK, yb2.shape[1]), row),
            pl.BlockSpec(mod.shape, lambda i: (0, 0, 0)),
            full(gna), full(gnm), full(wg), full(woa), full(wob), full(wout), full(rw), full(rb), full(utri),
        ],
        out_specs=[
            pl.BlockSpec((MERGE_TOK, d), row),
            pl.BlockSpec((MERGE_TOK, d), row),
            pl.BlockSpec((ROUTE_ROWS, MERGE_TOK), lambda i: (0, i)),
            pl.BlockSpec((N_EXPERTS, 1), const2),
        ],
        out_shape=[
            jax.ShapeDtypeStruct((t, d), F32),
            jax.ShapeDtypeStruct((t, d), F32),
            jax.ShapeDtypeStruct((ROUTE_ROWS, t), F32),
            jax.ShapeDtypeStruct((N_EXPERTS, 1), F32),
        ],
        scratch_shapes=[pltpu.VMEM((N_EXPERTS, 1), F32)],
        compiler_params=pltpu.CompilerParams(vmem_limit_bytes=VMEM_LIMIT),
        name="merge",
    )(x2, ya2, yb2, mod, gna, gnm, wg, woa, wob, wout, rw, rb, utri)


def _to_row_tiles(dst_ref, val):
    n, d = val.shape
    sub = d // LANES
    for c in range(sub):
        dst_ref[pl.ds(c, n, stride=sub), :] = val[:, c * LANES:(c + 1) * LANES]


def _from_row_tiles(src_ref, first_row, n, d):
    sub = d // LANES
    return jnp.concatenate([src_ref[pl.ds(first_row * sub + c, n, stride=sub), :] for c in range(sub)], axis=1)


def _row_tile_copy(src, src_row, dst, dst_row, sub, sem):
    src_at = pl.ds(pl.multiple_of(src_row * sub, sub), sub)
    dst_at = pl.ds(pl.multiple_of(dst_row * sub, sub), sub)
    return pltpu.make_async_copy(src.at[src_at], dst.at[dst_at], sem)


def _dispatch_kernel(pos_ref, padfrom_ref, padcnt_ref, nu_ref, h_ref, xb_out, tiles, zeros, sem, zsem):
    i = pl.program_id(0)
    n = pl.num_programs(0)
    slot = i % 2
    n_tok, d = h_ref.shape
    sub = d // LANES
    blk = zeros.shape[0]
    n_blocks = xb_out.shape[0] // blk

    def zero_fill(wait):
        def go(copy):
            copy.wait() if wait else copy.start()

        def per_expert(e, carry):
            cnt = padcnt_ref[e]
            for bit in reversed(range((FFN_ROWS - 1).bit_length())):
                rows = 1 << bit

                @pl.when((cnt & rows) != 0)
                def _():
                    done = (cnt >> (bit + 1)) << (bit + 1)
                    at = pl.multiple_of((padfrom_ref[e] + done) * sub, sub)
                    go(pltpu.make_async_copy(zeros.at[pl.ds(0, rows * sub)], xb_out.at[pl.ds(at, rows * sub)], zsem))
            return carry

        lax.fori_loop(0, N_EXPERTS, per_expert, 0)

        def per_block(j, carry):
            go(pltpu.make_async_copy(zeros, xb_out.at[pl.ds(pl.multiple_of(j * blk, blk), blk)], zsem))
            return carry

        lax.fori_loop(nu_ref[0], n_blocks, per_block, 0)

    @pl.when(i == 0)
    def _():
        zeros[...] = jnp.zeros_like(zeros)
        zero_fill(wait=False)
        zero_fill(wait=True)

    def drain(sl):
        for _ in range(TOP_K):
            pltpu.make_async_copy(tiles.at[sl], xb_out.at[pl.ds(0, n_tok * sub)], sem.at[sl]).wait()

    @pl.when(i >= 2)
    def _():
        drain(slot)

    _to_row_tiles(tiles.at[slot], h_ref[...])
    base = i * n_tok
    n_all = n * n_tok

    def issue(t, carry):
        for k in range(TOP_K):
            _row_tile_copy(tiles.at[slot], t, xb_out, pos_ref[k * n_all + base + t], sub,
                           sem.at[slot]).start(priority=k % DMA_QUEUES)
        return carry

    lax.fori_loop(0, n_tok, issue, 0, unroll=ISSUE_UNROLL // TOP_K)

    @pl.when((i == n - 1) & (n >= 2))
    def _():
        drain(1 - slot)

    @pl.when(i == n - 1)
    def _():
        drain(slot)


def _dispatch(pos_flat, pad_from, pad_cnt, n_used, h2, n_blocks):
    t, d = h2.shape
    sub = d // LANES
    return pl.pallas_call(
        _dispatch_kernel,
        grid_spec=pltpu.PrefetchScalarGridSpec(
            num_scalar_prefetch=4,
            grid=(t // TOK_TILE,),
            in_specs=[pl.BlockSpec((TOK_TILE, d), lambda i, *_: (i, 0))],
            out_specs=pl.BlockSpec(memory_space=pl.ANY),
            scratch_shapes=[pltpu.VMEM((2, TOK_TILE * sub, LANES), F32), pltpu.VMEM((FFN_ROWS * sub, LANES), F32),
                            pltpu.SemaphoreType.DMA((2,)), pltpu.SemaphoreType.DMA(())],
        ),
        out_shape=jax.ShapeDtypeStruct((n_blocks * FFN_ROWS * sub, LANES), F32),
        compiler_params=pltpu.CompilerParams(vmem_limit_bytes=VMEM_LIMIT),
        name="dispatch",
    )(pos_flat, pad_from, pad_cnt, n_used, h2)


def _ffn_kernel(be_ref, nu_ref, nxt_ref, x_ref, wg_hbm, bg_ref, wu_hbm, bu_ref, wd_hbm, bd_ref, o_ref,
                stage_g, stage_u, stage_d, wg_s, wu_s, wd_s, wsem):
    j = pl.program_id(0)
    prev = be_ref[jnp.maximum(j - 1, 0)]
    fresh = ((j == 0) | (be_ref[j] != prev)) & (j < nu_ref[0])
    streams = ((wg_hbm, stage_g, wg_s), (wu_hbm, stage_u, wu_s), (wd_hbm, stage_d, wd_s))

    def weight_copies(expert):
        return [pltpu.make_async_copy(hbm.at[expert], stage, wsem.at[n]) for n, (hbm, stage, _) in enumerate(streams)]

    @pl.when(j == 0)
    def _():
        for copy in weight_copies(be_ref[0]):
            copy.start()

    @pl.when(fresh)
    def _():
        for copy, (_, stage, dst) in zip(weight_copies(be_ref[j]), streams):
            copy.wait()
            dst[...] = stage[...].astype(BF16)

        @pl.when(nxt_ref[j] >= 0)
        def _():
            for copy in weight_copies(nxt_ref[j]):
                copy.start()

    @pl.when(j < nu_ref[0])
    def _():
        x = _from_row_tiles(x_ref, 0, FFN_ROWS, wg_s.shape[0]).astype(BF16)
        gate = jnp.minimum(_dot(x, wg_s[...]) + bg_ref[0], SWIGLU_LIMIT)
        up = jnp.clip(_dot(x, wu_s[...]) + bu_ref[0], -SWIGLU_LIMIT, SWIGLU_LIMIT)
        act = (up + 1.0) * (gate * jax.nn.sigmoid(SWIGLU_ALPHA * gate))
        _to_row_tiles(o_ref, _dot(act.astype(BF16), wd_s[...]) + bd_ref[0])

    @pl.when(j >= nu_ref[0])
    def _():
        o_ref[...] = jnp.zeros_like(o_ref)


def _ffn(block_e, n_used, next_e, xb, w_gate, b_gate, w_up, b_up, w_down, b_down):
    d, de = w_gate.shape[1:]
    blk = FFN_ROWS * d // LANES
    n_blocks = xb.shape[0] // blk
    xrow = lambda j, be, nu, nxt: (jnp.minimum(j, nu[0] - 1), 0)
    bsel = lambda j, be, nu, nxt: (be[j], 0, 0)
    any_spec = pl.BlockSpec(memory_space=pl.ANY)
    return pl.pallas_call(
        _ffn_kernel,
        grid_spec=pltpu.PrefetchScalarGridSpec(
            num_scalar_prefetch=3,
            grid=(n_blocks,),
            in_specs=[
                pl.BlockSpec((blk, LANES), xrow),
                any_spec,
                pl.BlockSpec((1, 1, de), bsel),
                any_spec,
                pl.BlockSpec((1, 1, de), bsel),
                any_spec,
                pl.BlockSpec((1, 1, d), bsel),
            ],
            out_specs=pl.BlockSpec((blk, LANES), lambda j, be, nu, nxt: (j, 0)),
            scratch_shapes=[pltpu.VMEM((d, de), F32), pltpu.VMEM((d, de), F32), pltpu.VMEM((de, d), F32),
                            pltpu.VMEM((d, de), BF16), pltpu.VMEM((d, de), BF16), pltpu.VMEM((de, d), BF16),
                            pltpu.SemaphoreType.DMA((3,))],
        ),
        out_shape=jax.ShapeDtypeStruct(xb.shape, F32),
        compiler_params=pltpu.CompilerParams(vmem_limit_bytes=VMEM_LIMIT),
        name="ffn",
    )(block_e, n_used, next_e, xb, w_gate, b_gate, w_up, b_up, w_down, b_down)


def _combine_kernel(tiles_per_sample, pos_ref, x1_ref, route_ref, mod_ref, yb_hbm, o_ref, buf, sem):
    i = pl.program_id(0)
    n = pl.num_programs(0)
    rows = TOP_K * CMB_TOK
    d = x1_ref.shape[1]
    sub = d // LANES

    def issue(tile, slot):
        def body(g, carry):
            for u in range(ISSUE_UNROLL):
                r = g * ISSUE_UNROLL + u
                _row_tile_copy(yb_hbm, pos_ref[tile * rows + r], buf.at[slot], r, sub,
                               sem.at[slot]).start(priority=u % DMA_QUEUES)
            return carry
        lax.fori_loop(0, rows // ISSUE_UNROLL, body, 0)

    @pl.when(i == 0)
    def _():
        issue(0, 0)

    @pl.when(i + 1 < n)
    def _():
        issue(i + 1, (i + 1) % 2)

    slot = i % 2
    pltpu.make_async_copy(yb_hbm.at[pl.ds(0, rows * sub)], buf.at[slot], sem.at[slot]).wait()

    b = i // tiles_per_sample
    g_m = mod_ref[b][5:6, :]
    route = route_ref[...]
    acc = jnp.zeros(x1_ref.shape, F32)
    for k in range(TOP_K):
        acc = acc + route[:, TOP_K + k:TOP_K + k + 1] * _from_row_tiles(buf.at[slot], k * CMB_TOK, CMB_TOK, d)
    o_ref[...] = x1_ref[...] + g_m * acc


def _combine(pos_tiles, x1, route, mod, yb, tiles_per_sample):
    t, d = x1.shape
    row = lambda i, pos: (i, 0)
    return pl.pallas_call(
        functools.partial(_combine_kernel, tiles_per_sample),
        grid_spec=pltpu.PrefetchScalarGridSpec(
            num_scalar_prefetch=1,
            grid=(t // CMB_TOK,),
            in_specs=[
                pl.BlockSpec((CMB_TOK, d), row),
                pl.BlockSpec((CMB_TOK, route.shape[1]), row),
                pl.BlockSpec(mod.shape, lambda i, pos: (0, 0, 0)),
                pl.BlockSpec(memory_space=pl.ANY),
            ],
            out_specs=pl.BlockSpec((CMB_TOK, d), row),
            scratch_shapes=[pltpu.VMEM((2, TOP_K * CMB_TOK * d // LANES, LANES), F32),
                            pltpu.SemaphoreType.DMA((2,))],
        ),
        out_shape=jax.ShapeDtypeStruct((t, d), F32),
        compiler_params=pltpu.CompilerParams(vmem_limit_bytes=VMEM_LIMIT),
        name="combine",
    )(pos_tiles, x1, route, mod, yb)


def _rope_tables(s):
    rows = s // GRID_W
    row = jnp.repeat(jnp.arange(rows, dtype=F32), GRID_W)
    col = jnp.tile(jnp.arange(GRID_W, dtype=F32), rows)
    inv = 1.0 / (ROPE_THETA ** (jnp.arange(AXIS_PAIRS, dtype=F32) / AXIS_PAIRS))
    ang = jnp.concatenate([row[:, None] * inv, col[:, None] * inv], axis=-1)
    cos, sin = jnp.cos(ang), jnp.sin(ang)
    cos_i = jnp.repeat(cos, 2, axis=-1)
    sin_i = jnp.stack([-sin, sin], axis=-1).reshape(s, HEAD_DIM)
    reps = LANES // HEAD_DIM
    return jnp.tile(cos_i, (1, reps)), jnp.tile(sin_i, (1, reps))


def kernel(x, c, ctx, c_ctx, w_ada, b_ada, norm_attn, w_in, q_norm_a, k_norm_a, q_norm_b, k_norm_b, lambda_q1,
           lambda_k1, lambda_q2, lambda_k2, subln_b, w_oa, w_ob, w_out, norm_mlp, router_w, router_b, w_gate,
           b_gate, w_up, b_up, w_down, b_down):
    bsz, s, d = x.shape
    assert ctx.shape[1] == TOK_TILE and s % MERGE_TOK == 0 and s % TQ_A == 0 and s % TQ_B == 0 and s % GRID_W == 0
    assert w_ada.shape[0] == 1, "single-layer block"
    t = bsz * s
    assert t % CMB_TOK == 0 and (t * TOP_K) % FFN_ROWS == 0

    mod_rows = 8
    assert bsz + 1 <= mod_rows
    cvec = jnp.concatenate([c, c_ctx[None, :], jnp.zeros((mod_rows - bsz - 1, d), F32)], axis=0)
    mod = _adaln(cvec, w_ada[0], b_ada[0][None, :]).reshape(mod_rows, N_MOD, d)

    cos, sin = _rope_tables(s)
    tile_gain = lambda g: jnp.tile(g, LANES // HEAD_DIM)[None, :]
    gidx = jnp.arange(MXU_DIM) // HEAD_DIM
    gmat = jnp.where(gidx[:, None] == gidx[None, :], 1.0 / HEAD_DIM, 0.0).astype(BF16)
    w_qkv = w_in[0][:, :QKV_COLS].astype(BF16)
    w_g = w_in[0][:, QKV_COLS:].astype(BF16)

    qat, ka, vat, qbt, kb, vbt = _project(
        x, ctx, mod, norm_attn, w_qkv, cos, sin, tile_gain(q_norm_a[0]), tile_gain(k_norm_a[0]),
        tile_gain(q_norm_b[0]), tile_gain(k_norm_b[0]), gmat)

    ya = _attn_a(qat, ka, vat)
    yb = _attn_b(qbt, kb, vbt, lambda_q1, lambda_k1, lambda_q2, lambda_k2, subln_b[0][:, None])

    tri = jnp.arange(TOK_TILE)
    utri = (tri[:, None] < tri[None, :]).astype(BF16)
    x1, h2, route_t, counts = _merge(
        x.reshape(t, d), ya.reshape(t, A_Q_COLS), yb.reshape(t, B_WIDTH), mod, norm_attn, norm_mlp, w_g,
        w_oa[0].astype(BF16), w_ob[0].astype(BF16), w_out[0].astype(BF16), router_w[0].T, router_b[0][:, None], utri,
        s // MERGE_TOK)

    counts = counts[:, 0].astype(I32)
    padded = (counts + FFN_ROWS - 1) // FFN_ROWS * FFN_ROWS
    pad_end = jnp.cumsum(padded)
    pad_start = pad_end - padded
    top_e = route_t[:TOP_K].astype(I32)
    rank = route_t[2 * TOP_K:3 * TOP_K].astype(I32)
    experts = jnp.arange(N_EXPERTS, dtype=I32)
    start_of = jnp.sum(jnp.where(top_e[..., None] == experts, pad_start, 0), axis=-1)
    pos_t = start_of + rank
    n_blocks = t * TOP_K // FFN_ROWS + N_EXPERTS
    block_start = jnp.arange(n_blocks, dtype=I32) * FFN_ROWS
    block_e = jnp.minimum(jnp.sum((pad_end[None, :] <= block_start[:, None]).astype(I32), axis=1), N_EXPERTS - 1)
    n_used = (pad_end[-1:] // FFN_ROWS).astype(I32)
    first_at_or_after = lax.cummin(jnp.where(padded > 0, jnp.arange(N_EXPERTS, dtype=I32), N_EXPERTS), reverse=True)
    next_nonempty = jnp.concatenate([first_at_or_after[1:], jnp.full((1,), N_EXPERTS, I32)])
    next_of = jnp.where(next_nonempty < N_EXPERTS, next_nonempty, -1)
    next_e = jnp.sum(jnp.where(block_e[:, None] == experts, next_of, 0), axis=-1)

    xb = _dispatch(pos_t.reshape(-1), pad_start + counts, padded - counts, n_used, h2, n_blocks)
    de = w_gate.shape[-1]
    yb_rows = _ffn(block_e, n_used, next_e, xb, w_gate[0], b_gate[0].reshape(N_EXPERTS, 1, de), w_up[0],
                   b_up[0].reshape(N_EXPERTS, 1, de), w_down[0], b_down[0].reshape(N_EXPERTS, 1, d))
    pos_tiles = pos_t.reshape(TOP_K, t // CMB_TOK, CMB_TOK).transpose(1, 0, 2).reshape(-1)
    out = _combine(pos_tiles, x1, route_t.T, mod, yb_rows, s // CMB_TOK)
    return out.reshape(bsz, s, d)
```

```python
import functools
import math

import jax
import jax.numpy as jnp
from jax import lax
from jax.experimental import pallas as pl
from jax.experimental.pallas import tpu as pltpu

F32 = jnp.float32
BF16 = jnp.bfloat16
I32 = jnp.int32

GRID_W = 64
HEAD_DIM = 64
A_HEADS = 8
A_KV_HEADS = 2
A_GROUP = A_HEADS // A_KV_HEADS
B_HEADS = 4
B_V_DIM = 2 * HEAD_DIM
N_EXPERTS = 32
TOP_K = 4
N_MOD = 6
ROPE_THETA = 10000.0
AXIS_PAIRS = HEAD_DIM // 4
SWIGLU_LIMIT = 7.0
SWIGLU_ALPHA = 1.702
EPS = 1e-6
SUBLN_EPS = 1e-5
LAM_INIT = 0.8 - 0.6 * math.exp(-0.3 * 0)

A_Q_COLS = A_HEADS * HEAD_DIM
A_KV_COLS = A_KV_HEADS * HEAD_DIM
B_QK_COLS = B_HEADS * 2 * HEAD_DIM
B_WIDTH = B_HEADS * B_V_DIM
QKV_COLS = A_Q_COLS + 2 * A_KV_COLS + 2 * B_QK_COLS + B_WIDTH

LANES = 128
MXU_DIM = 256
TOK_TILE = 256
MERGE_TOK = 4 * TOK_TILE
TQ_A = 512
TQ_B = 1024
SCORE_TILES_PER_TRIP = 512
ONES_ROWS = 16
A_VROWS = HEAD_DIM + ONES_ROWS
B_VROWS = B_V_DIM + ONES_ROWS
ADA_COLS = 1536
FFN_ROWS = 256
CMB_TOK = 128
ROUTE_ROWS = 16
ISSUE_UNROLL = 16
DMA_QUEUES = 2
VMEM_LIMIT = 56 * 1024 * 1024


def _dot(a, b):
    return jnp.dot(a, b, preferred_element_type=F32)


def _split_bf16(a):
    hi = a.astype(BF16)
    lo = (a - hi.astype(F32)).astype(BF16)
    return hi, lo


def _dot3(a, b):
    a_hi, a_lo = _split_bf16(a)
    b_hi, b_lo = _split_bf16(b)
    return _dot(a_hi, b_hi) + (_dot(a_hi, b_lo) + _dot(a_lo, b_hi))


def _dot3_nt(a, b):
    dims = (((1,), (1,)), ((), ()))
    dot_nt = lambda u, v: lax.dot_general(u, v, dims, preferred_element_type=F32)
    a_hi, a_lo = _split_bf16(a)
    b_hi, b_lo = _split_bf16(b)
    return dot_nt(a_hi, b_hi) + (dot_nt(a_hi, b_lo) + dot_nt(a_lo, b_hi))


def _rms_rows(x, gain, eps):
    ms = jnp.mean(x * x, axis=-1, keepdims=True)
    return x * lax.rsqrt(ms + eps) * gain


def _adaln_kernel(c_ref, w_ref, b_ref, o_ref):
    c = c_ref[...]
    a = c * jax.nn.sigmoid(c)
    o_ref[...] = _dot3(a, w_ref[...]) + b_ref[...]


def _adaln(cvec, w_ada, b_ada):
    rows, d = cvec.shape
    n = w_ada.shape[1]
    return pl.pallas_call(
        _adaln_kernel,
        grid=(n // ADA_COLS,),
        in_specs=[
            pl.BlockSpec((rows, d), lambda j: (0, 0)),
            pl.BlockSpec((d, ADA_COLS), lambda j: (0, j)),
            pl.BlockSpec((1, ADA_COLS), lambda j: (0, j)),
        ],
        out_specs=pl.BlockSpec((rows, ADA_COLS), lambda j: (0, j)),
        out_shape=jax.ShapeDtypeStruct((rows, n), F32),
        compiler_params=pltpu.CompilerParams(vmem_limit_bytes=VMEM_LIMIT),
        name="adaln",
    )(cvec, w_ada, b_ada)


def _group_rms(y, gmat, gain):
    w_total = y.shape[1]
    outs = []
    for c0 in range(0, w_total, MXU_DIM):
        w = min(MXU_DIM, w_total - c0)
        yc = y[:, c0:c0 + w]
        ms = _dot((yc * yc).astype(BF16), gmat[:w, :w])
        reps = w // LANES
        g = gain if reps == 1 else jnp.concatenate([gain] * reps, axis=1)
        outs.append(yc * lax.rsqrt(ms + EPS) * g)
    return outs[0] if len(outs) == 1 else jnp.concatenate(outs, axis=1)


def _rope(y, cos, sin):
    rows = y.shape[0]
    lane = lax.broadcasted_iota(I32, (rows, LANES), 1)
    even = (lane % 2) == 0
    outs = []
    for c in range(y.shape[1] // LANES):
        ch = y[:, c * LANES:(c + 1) * LANES]
        partner = jnp.where(even, pltpu.roll(ch, LANES - 1, 1), pltpu.roll(ch, 1, 1))
        outs.append(ch * cos + partner * sin)
    return outs[0] if len(outs) == 1 else jnp.concatenate(outs, axis=1)


def _proj_kernel(x_ref, ctx_ref, mod_ref, gn_ref, w_ref, cos_ref, sin_ref, qna_ref, kna_ref, qnb_ref, knb_ref,
                 gmat_ref, qat_ref, ka_ref, vat_ref, qbt_ref, kb_ref, vbt_ref):
    b = pl.program_id(0)
    j = pl.program_id(1)
    is_ctx = j == 0
    xt = jnp.where(is_ctx, ctx_ref[0], x_ref[0])
    mrow = jnp.where(is_ctx, pl.num_programs(0), b)
    mod = mod_ref[mrow]
    h = _rms_rows(xt, gn_ref[...], EPS) * (1.0 + mod[1:2, :]) + mod[0:1, :]
    y = _dot(h.astype(BF16), w_ref[...])
    cos = jnp.where(is_ctx, 1.0, cos_ref[...])
    sin = jnp.where(is_ctx, 0.0, sin_ref[...])
    gmat = gmat_ref[...]
    scale = HEAD_DIM ** -0.5 * math.log2(math.e)

    o = 0
    qa = _rope(_group_rms(y[:, o:o + A_Q_COLS], gmat, qna_ref[...]), cos, sin) * scale
    o += A_Q_COLS
    ka = _rope(_group_rms(y[:, o:o + A_KV_COLS], gmat, kna_ref[...]), cos, sin)
    o += A_KV_COLS
    va = y[:, o:o + A_KV_COLS]
    o += A_KV_COLS
    qb = _rope(_group_rms(y[:, o:o + B_QK_COLS], gmat, qnb_ref[...]), cos, sin) * scale
    o += B_QK_COLS
    kb = _rope(_group_rms(y[:, o:o + B_QK_COLS], gmat, knb_ref[...]), cos, sin)
    o += B_QK_COLS
    vb = y[:, o:o + B_WIDTH]

    @pl.when(j > 0)
    def _():
        qat_ref[0] = qa.T.astype(BF16)
        qbt_ref[0] = qb.T.astype(BF16)

    ka_ref[0] = ka.astype(BF16)
    kb_ref[0] = kb.astype(BF16)
    ones = jnp.ones((ONES_ROWS, xt.shape[0]), BF16)
    for dst_ref, vt, width in ((vat_ref, va.T.astype(BF16), HEAD_DIM), (vbt_ref, vb.T.astype(BF16), B_V_DIM)):
        stride = width + ONES_ROWS
        for hd in range(vt.shape[0] // width):
            dst_ref[0, 0, hd * stride:hd * stride + width, :] = vt[hd * width:(hd + 1) * width, :]
            dst_ref[0, 0, hd * stride + width:(hd + 1) * stride, :] = ones


def _project(x, ctx, mod, gn, w_qkv, cos, sin, qna, kna, qnb, knb, gmat):
    bsz, s, d = x.shape
    n_lat = s // TOK_TILE
    n_key = n_lat + 1
    lk = n_key * TOK_TILE
    lat = lambda b, j: (b, jnp.maximum(j - 1, 0), 0)
    const2 = lambda b, j: (0, 0)
    tab = lambda b, j: (jnp.maximum(j - 1, 0), 0)
    return pl.pallas_call(
        _proj_kernel,
        grid=(bsz, n_key),
        in_specs=[
            pl.BlockSpec((1, TOK_TILE, d), lat),
            pl.BlockSpec((1, TOK_TILE, d), lambda b, j: (b, 0, 0)),
            pl.BlockSpec(mod.shape, lambda b, j: (0, 0, 0)),
            pl.BlockSpec((1, d), const2),
            pl.BlockSpec((d, QKV_COLS), const2),
            pl.BlockSpec((TOK_TILE, LANES), tab),
            pl.BlockSpec((TOK_TILE, LANES), tab),
            pl.BlockSpec((1, LANES), const2),
            pl.BlockSpec((1, LANES), const2),
            pl.BlockSpec((1, LANES), const2),
            pl.BlockSpec((1, LANES), const2),
            pl.BlockSpec((MXU_DIM, MXU_DIM), const2),
        ],
        out_specs=[
            pl.BlockSpec((1, A_Q_COLS, TOK_TILE), lambda b, j: (b, 0, jnp.maximum(j - 1, 0))),
            pl.BlockSpec((1, TOK_TILE, A_KV_COLS), lambda b, j: (b, j, 0)),
            pl.BlockSpec((1, 1, A_KV_HEADS * A_VROWS, TOK_TILE), lambda b, j: (b, j, 0, 0)),
            pl.BlockSpec((1, B_QK_COLS, TOK_TILE), lambda b, j: (b, 0, jnp.maximum(j - 1, 0))),
            pl.BlockSpec((1, TOK_TILE, B_QK_COLS), lambda b, j: (b, j, 0)),
            pl.BlockSpec((1, 1, B_HEADS * B_VROWS, TOK_TILE), lambda b, j: (b, j, 0, 0)),
        ],
        out_shape=[
            jax.ShapeDtypeStruct((bsz, A_Q_COLS, s), BF16),
            jax.ShapeDtypeStruct((bsz, lk, A_KV_COLS), BF16),
            jax.ShapeDtypeStruct((bsz, n_key, A_KV_HEADS * A_VROWS, TOK_TILE), BF16),
            jax.ShapeDtypeStruct((bsz, B_QK_COLS, s), BF16),
            jax.ShapeDtypeStruct((bsz, lk, B_QK_COLS), BF16),
            jax.ShapeDtypeStruct((bsz, n_key, B_HEADS * B_VROWS, TOK_TILE), BF16),
        ],
        compiler_params=pltpu.CompilerParams(vmem_limit_bytes=VMEM_LIMIT),
        name="proj",
    )(x, ctx, mod, gn, w_qkv, cos, sin, qna, kna, qnb, knb, gmat)


def _flash(k_ref, vt_ref, wq, pv, dv, s_even, s_odd):
    n = wq.shape[1]
    n_tiles = n // MXU_DIM
    n_chunks = vt_ref.shape[1]
    cols = [slice(t * MXU_DIM, (t + 1) * MXU_DIM) for t in range(n_tiles)]
    wqs = [wq[:, cs] for cs in cols]

    def scores(c, t, dst):
        start = pl.multiple_of(c * TOK_TILE, TOK_TILE)
        dst[:, cols[t]] = _dot(k_ref[0, pl.ds(start, TOK_TILE), :], wqs[t])

    def chunk(c, cur, nxt, state):
        out = []
        for t in range(n_tiles):
            if nxt is not None:
                scores(c + 1, t, nxt)
            m, acc = state[t]
            s = cur[:, cols[t]]
            m_new = jnp.maximum(m, jnp.max(s, axis=0, keepdims=True))
            alpha = jnp.exp2(m - m_new)
            p = jnp.exp2(s - m_new).astype(BF16)
            out.append((m_new, alpha * acc + pv(t, vt_ref[0, c], p)))
        return tuple(out)

    for t in range(n_tiles):
        scores(0, t, s_even)

    bufs = (s_even, s_odd)

    per_trip = SCORE_TILES_PER_TRIP // n_tiles
    assert per_trip % 2 == 0, "the two score buffers alternate, so a trip must hold an even number of chunks"

    def trip(i, state):
        for u in range(per_trip):
            state = chunk(i * per_trip + u, bufs[u % 2], bufs[(u + 1) % 2], state)
        return state

    init = tuple((jnp.full((1, MXU_DIM), -jnp.inf, F32), jnp.zeros((dv + ONES_ROWS, MXU_DIM), F32))
                 for _ in range(n_tiles))
    n_trips = (n_chunks - 1) // per_trip
    state = lax.fori_loop(0, n_trips, trip, init)
    for c in range(n_trips * per_trip, n_chunks):
        state = chunk(c, bufs[c % 2], bufs[(c + 1) % 2] if c + 1 < n_chunks else None, state)
    return jnp.concatenate([acc for _, acc in state], axis=1)


def _attn_a_kernel(qt_ref, k_ref, vt_ref, o_ref, s_even, s_odd):
    tq = qt_ref.shape[2]
    half = A_GROUP * tq
    zeros = jnp.zeros((HEAD_DIM, half), BF16)
    rows = []
    for g in range(A_KV_HEADS):
        heads = range(g * A_GROUP, (g + 1) * A_GROUP)
        qg = jnp.concatenate([qt_ref[0, h * HEAD_DIM:(h + 1) * HEAD_DIM, :] for h in heads], axis=1)
        rows.append(jnp.concatenate([qg if gg == g else zeros for gg in range(A_KV_HEADS)], axis=1))
    wq = jnp.concatenate(rows, axis=0)

    def pv(t, vc, p):
        g = t * MXU_DIM // half
        return _dot(vc[g * A_VROWS:(g + 1) * A_VROWS], p)

    acc = _flash(k_ref, vt_ref, wq, pv, HEAD_DIM, s_even, s_odd)
    o = acc[:HEAD_DIM] / acc[HEAD_DIM:HEAD_DIM + 1]
    outs = [o[:, h * tq:(h + 1) * tq].T for h in range(A_HEADS)]
    o_ref[0] = jnp.concatenate(outs, axis=1).astype(o_ref.dtype)


def _attn_a(qat, ka, vat):
    bsz, _, s = qat.shape
    lk = ka.shape[1]
    n_chunks = vat.shape[1]
    return pl.pallas_call(
        _attn_a_kernel,
        grid=(bsz, s // TQ_A),
        in_specs=[
            pl.BlockSpec((1, A_Q_COLS, TQ_A), lambda b, i: (b, 0, i)),
            pl.BlockSpec((1, lk, A_KV_COLS), lambda b, i: (b, 0, 0)),
            pl.BlockSpec((1, n_chunks, A_KV_HEADS * A_VROWS, TOK_TILE), lambda b, i: (b, 0, 0, 0)),
        ],
        out_specs=pl.BlockSpec((1, TQ_A, A_Q_COLS), lambda b, i: (b, i, 0)),
        out_shape=jax.ShapeDtypeStruct((bsz, s, A_Q_COLS), BF16),
        scratch_shapes=[pltpu.VMEM((TOK_TILE, A_HEADS * TQ_A), F32)] * 2,
        compiler_params=pltpu.CompilerParams(vmem_limit_bytes=VMEM_LIMIT),
        name="attn_a",
    )(qat, ka, vat)


def _attn_b_kernel(qt_ref, k_ref, vt_ref, lq1_ref, lk1_ref, lq2_ref, lk2_ref, sub_ref, o_ref, s_even, s_odd):
    tq = qt_ref.shape[2]
    zeros = jnp.zeros((HEAD_DIM, tq), BF16)
    q1 = qt_ref[0, :HEAD_DIM, :]
    q2 = qt_ref[0, HEAD_DIM:, :]
    wq = jnp.concatenate([jnp.concatenate([q1, zeros], axis=1), jnp.concatenate([zeros, q2], axis=1)], axis=0)
    acc = _flash(k_ref, vt_ref, wq, lambda t, vc, p: _dot(vc, p), B_V_DIM, s_even, s_odd)
    r = acc[:B_V_DIM] / acc[B_V_DIM:B_V_DIM + 1]
    lam = (jnp.exp(jnp.sum(lq1_ref[...] * lk1_ref[...], axis=1, keepdims=True))
           - jnp.exp(jnp.sum(lq2_ref[...] * lk2_ref[...], axis=1, keepdims=True)) + LAM_INIT)
    o = r[:, :tq] - lam * r[:, tq:]
    ms = jnp.mean(o * o, axis=0, keepdims=True)
    o = o * lax.rsqrt(ms + SUBLN_EPS) * sub_ref[...] * (1.0 - LAM_INIT)
    o_ref[0] = o.T.astype(o_ref.dtype)


def _attn_b(qbt, kb, vbt, lq1, lk1, lq2, lk2, subln):
    bsz, _, s = qbt.shape
    lk = kb.shape[1]
    n_chunks = vbt.shape[1]
    vec = pl.BlockSpec((1, HEAD_DIM), lambda b, h, i: (0, 0))
    return pl.pallas_call(
        _attn_b_kernel,
        grid=(bsz, B_HEADS, s // TQ_B),
        in_specs=[
            pl.BlockSpec((1, 2 * HEAD_DIM, TQ_B), lambda b, h, i: (b, h, i)),
            pl.BlockSpec((1, lk, 2 * HEAD_DIM), lambda b, h, i: (b, 0, h)),
            pl.BlockSpec((1, n_chunks, B_VROWS, TOK_TILE), lambda b, h, i: (b, 0, h, 0)),
            vec, vec, vec, vec,
            pl.BlockSpec((B_V_DIM, 1), lambda b, h, i: (0, 0)),
        ],
        out_specs=pl.BlockSpec((1, TQ_B, B_V_DIM), lambda b, h, i: (b, i, h)),
        out_shape=jax.ShapeDtypeStruct((bsz, s, B_WIDTH), BF16),
        scratch_shapes=[pltpu.VMEM((TOK_TILE, 2 * TQ_B), F32)] * 2,
        compiler_params=pltpu.CompilerParams(vmem_limit_bytes=VMEM_LIMIT),
        name="attn_b",
    )(qbt, kb, vbt, lq1, lk1, lq2, lk2, subln)


def _merge_kernel(tiles_per_sample, x_ref, ya_ref, yb_ref, mod_ref, gna_ref, gnm_ref, wg_ref, woa_ref, wob_ref,
                  wout_ref, rw_ref, rb_ref, utri_ref, x1_ref, h2_ref, route_ref, cnt_ref, carry_ref):
    i = pl.program_id(0)

    @pl.when(i == 0)
    def _():
        carry_ref[...] = jnp.zeros_like(carry_ref)

    mod = mod_ref[i // tiles_per_sample]
    d = x_ref.shape[1]
    subs = [slice(u * TOK_TILE, (u + 1) * TOK_TILE) for u in range(x_ref.shape[0] // TOK_TILE)]
    xs = [x_ref[r, :] for r in subs]
    hs = [(_rms_rows(x, gna_ref[...], EPS) * (1.0 + mod[1:2, :]) + mod[0:1, :]).astype(BF16) for x in xs]
    gates = [_dot(h, wg_ref[...]) for h in hs]
    branch = [(_dot(ya_ref[r, :], woa_ref[...]), _dot(yb_ref[r, :], wob_ref[...])) for r in subs]
    merged = [(jax.nn.sigmoid(g[:, :d]) * pa + jax.nn.sigmoid(g[:, d:]) * pb).astype(BF16)
              for g, (pa, pb) in zip(gates, branch)]
    x1s = [x + mod[2:3, :] * _dot(m, wout_ref[...]) for x, m in zip(xs, merged)]
    h2s = [_rms_rows(x1, gnm_ref[...], EPS) * (1.0 + mod[4:5, :]) + mod[3:4, :] for x1 in x1s]
    for r, x1, h2 in zip(subs, x1s, h2s):
        x1_ref[r, :] = x1
        h2_ref[r, :] = h2

    all_logits = [_dot3_nt(rw_ref[...], h2) + rb_ref[...] for h2 in h2s]
    eiota = lax.broadcasted_iota(I32, (N_EXPERTS, TOK_TILE), 0)
    riota = lax.broadcasted_iota(I32, (ROUTE_ROWS, TOK_TILE), 0)
    carry = carry_ref[...]
    for r, logits in zip(subs, all_logits):
        work = logits
        idxs, vals = [], []
        for _ in range(TOP_K):
            mx = jnp.max(work, axis=0, keepdims=True)
            idx = jnp.min(jnp.where(work == mx, eiota, N_EXPERTS), axis=0, keepdims=True)
            idxs.append(idx)
            vals.append(mx)
            work = jnp.where(eiota == idx, -jnp.inf, work)
        exps = [jnp.exp(v - vals[0]) for v in vals]
        denom = exps[0] + exps[1] + exps[2] + exps[3]
        weights = [e / denom for e in exps]

        hits = [(eiota == idx) for idx in idxs]
        onehot = (hits[0] | hits[1] | hits[2] | hits[3]).astype(F32)
        before = _dot(onehot.astype(BF16), utri_ref[...]) + carry
        ranks = [jnp.sum(jnp.where(hit, before, 0.0), axis=0, keepdims=True) for hit in hits]
        carry = carry + jnp.sum(onehot, axis=1, keepdims=True)

        lines = [v.astype(F32) for v in idxs] + weights + ranks
        route = jnp.zeros((ROUTE_ROWS, TOK_TILE), F32)
        for k, line in enumerate(lines):
            route = jnp.where(riota == k, line, route)
        route_ref[:, r] = route
    carry_ref[...] = carry
    cnt_ref[...] = carry


def _merge(x2, ya2, yb2, mod, gna, gnm, wg, woa, wob, wout, rw, rb, utri, tiles_per_sample):
    t, d = x2.shape
    const2 = lambda i: (0, 0)
    row = lambda i: (i, 0)
    full = lambda a: pl.BlockSpec(a.shape, const2)
    return pl.pallas_call(
        functools.partial(_merge_kernel, tiles_per_sample),
        grid=(t // MERGE_TOK,),
        in_specs=[
            pl.BlockSpec((MERGE_TOK, d), row),
            pl.BlockSpec((MERGE_TOK, ya2.shape[1]), row),
            pl.BlockSpec((MERGE_TOK, yb2.shape[1]), row),
            pl.BlockSpec(mod.shape, lambda i: (0, 0, 0)),
            full(gna), full(gnm), full(wg), full(woa), full(wob), full(wout), full(rw), full(rb), full(utri),
        ],
        out_specs=[
            pl.BlockSpec((MERGE_TOK, d), row),
            pl.BlockSpec((MERGE_TOK, d), row),
            pl.BlockSpec((ROUTE_ROWS, MERGE_TOK), lambda i: (0, i)),
            pl.BlockSpec((N_EXPERTS, 1), const2),
        ],
        out_shape=[
            jax.ShapeDtypeStruct((t, d), F32),
            jax.ShapeDtypeStruct((t, d), F32),
            jax.ShapeDtypeStruct((ROUTE_ROWS, t), F32),
            jax.ShapeDtypeStruct((N_EXPERTS, 1), F32),
        ],
        scratch_shapes=[pltpu.VMEM((N_EXPERTS, 1), F32)],
        compiler_params=pltpu.CompilerParams(vmem_limit_bytes=VMEM_LIMIT),
        name="merge",
    )(x2, ya2, yb2, mod, gna, gnm, wg, woa, wob, wout, rw, rb, utri)


def _to_row_tiles(dst_ref, val):
    n, d = val.shape
    sub = d // LANES
    for c in range(sub):
        dst_ref[pl.ds(c, n, stride=sub), :] = val[:, c * LANES:(c + 1) * LANES]


def _from_row_tiles(src_ref, first_row, n, d):
    sub = d // LANES
    return jnp.concatenate([src_ref[pl.ds(first_row * sub + c, n, stride=sub), :] for c in range(sub)], axis=1)


def _row_tile_copy(src, src_row, dst, dst_row, sub, sem):
    src_at = pl.ds(pl.multiple_of(src_row * sub, sub), sub)
    dst_at = pl.ds(pl.multiple_of(dst_row * sub, sub), sub)
    return pltpu.make_async_copy(src.at[src_at], dst.at[dst_at], sem)


def _dispatch_kernel(pos_ref, padfrom_ref, padcnt_ref, nu_ref, h_ref, xb_out, tiles, zeros, sem, zsem):
    i = pl.program_id(0)
    n = pl.num_programs(0)
    slot = i % 2
    n_tok, d = h_ref.shape
    sub = d // LANES
    blk = zeros.shape[0]
    n_blocks = xb_out.shape[0] // blk

    def zero_fill(wait):
        def go(copy):
            copy.wait() if wait else copy.start()

        def per_expert(e, carry):
            cnt = padcnt_ref[e]
            for bit in reversed(range((FFN_ROWS - 1).bit_length())):
                rows = 1 << bit

                @pl.when((cnt & rows) != 0)
                def _():
                    done = (cnt >> (bit + 1)) << (bit + 1)
                    at = pl.multiple_of((padfrom_ref[e] + done) * sub, sub)
                    go(pltpu.make_async_copy(zeros.at[pl.ds(0, rows * sub)], xb_out.at[pl.ds(at, rows * sub)], zsem))
            return carry

        lax.fori_loop(0, N_EXPERTS, per_expert, 0)

        def per_block(j, carry):
            go(pltpu.make_async_copy(zeros, xb_out.at[pl.ds(pl.multiple_of(j * blk, blk), blk)], zsem))
            return carry

        lax.fori_loop(nu_ref[0], n_blocks, per_block, 0)

    @pl.when(i == 0)
    def _():
        zeros[...] = jnp.zeros_like(zeros)
        zero_fill(wait=False)
        zero_fill(wait=True)

    def drain(sl):
        for _ in range(TOP_K):
            pltpu.make_async_copy(tiles.at[sl], xb_out.at[pl.ds(0, n_tok * sub)], sem.at[sl]).wait()

    @pl.when(i >= 2)
    def _():
        drain(slot)

    _to_row_tiles(tiles.at[slot], h_ref[...])
    base = i * n_tok
    n_all = n * n_tok

    def issue(t, carry):
        for k in range(TOP_K):
            _row_tile_copy(tiles.at[slot], t, xb_out, pos_ref[k * n_all + base + t], sub,
                           sem.at[slot]).start(priority=k % DMA_QUEUES)
        return carry

    lax.fori_loop(0, n_tok, issue, 0, unroll=ISSUE_UNROLL // TOP_K)

    @pl.when((i == n - 1) & (n >= 2))
    def _():
        drain(1 - slot)

    @pl.when(i == n - 1)
    def _():
        drain(slot)


def _dispatch(pos_flat, pad_from, pad_cnt, n_used, h2, n_blocks):
    t, d = h2.shape
    sub = d // LANES
    return pl.pallas_call(
        _dispatch_kernel,
        grid_spec=pltpu.PrefetchScalarGridSpec(
            num_scalar_prefetch=4,
            grid=(t // TOK_TILE,),
            in_specs=[pl.BlockSpec((TOK_TILE, d), lambda i, *_: (i, 0))],
            out_specs=pl.BlockSpec(memory_space=pl.ANY),
            scratch_shapes=[pltpu.VMEM((2, TOK_TILE * sub, LANES), F32), pltpu.VMEM((FFN_ROWS * sub, LANES), F32),
                            pltpu.SemaphoreType.DMA((2,)), pltpu.SemaphoreType.DMA(())],
        ),
        out_shape=jax.ShapeDtypeStruct((n_blocks * FFN_ROWS * sub, LANES), F32),
        compiler_params=pltpu.CompilerParams(vmem_limit_bytes=VMEM_LIMIT),
        name="dispatch",
    )(pos_flat, pad_from, pad_cnt, n_used, h2)


def _ffn_kernel(be_ref, nu_ref, nxt_ref, x_ref, wg_hbm, bg_ref, wu_hbm, bu_ref, wd_hbm, bd_ref, o_ref,
                stage_g, stage_u, stage_d, wg_s, wu_s, wd_s, wsem):
    j = pl.program_id(0)
    prev = be_ref[jnp.maximum(j - 1, 0)]
    fresh = ((j == 0) | (be_ref[j] != prev)) & (j < nu_ref[0])
    streams = ((wg_hbm, stage_g, wg_s), (wu_hbm, stage_u, wu_s), (wd_hbm, stage_d, wd_s))

    def weight_copies(expert):
        return [pltpu.make_async_copy(hbm.at[expert], stage, wsem.at[n]) for n, (hbm, stage, _) in enumerate(streams)]

    @pl.when(j == 0)
    def _():
        for copy in weight_copies(be_ref[0]):
            copy.start()

    @pl.when(fresh)
    def _():
        for copy, (_, stage, dst) in zip(weight_copies(be_ref[j]), streams):
            copy.wait()
            dst[...] = stage[...].astype(BF16)

        @pl.when(nxt_ref[j] >= 0)
        def _():
            for copy in weight_copies(nxt_ref[j]):
                copy.start()

    @pl.when(j < nu_ref[0])
    def _():
        x = _from_row_tiles(x_ref, 0, FFN_ROWS, wg_s.shape[0]).astype(BF16)
        gate = jnp.minimum(_dot(x, wg_s[...]) + bg_ref[0], SWIGLU_LIMIT)
        up = jnp.clip(_dot(x, wu_s[...]) + bu_ref[0], -SWIGLU_LIMIT, SWIGLU_LIMIT)
        act = (up + 1.0) * (gate * jax.nn.sigmoid(SWIGLU_ALPHA * gate))
        _to_row_tiles(o_ref, _dot(act.astype(BF16), wd_s[...]) + bd_ref[0])

    @pl.when(j >= nu_ref[0])
    def _():
        o_ref[...] = jnp.zeros_like(o_ref)


def _ffn(block_e, n_used, next_e, xb, w_gate, b_gate, w_up, b_up, w_down, b_down):
    d, de = w_gate.shape[1:]
    blk = FFN_ROWS * d // LANES
    n_blocks = xb.shape[0] // blk
    xrow = lambda j, be, nu, nxt: (jnp.minimum(j, nu[0] - 1), 0)
    bsel = lambda j, be, nu, nxt: (be[j], 0, 0)
    any_spec = pl.BlockSpec(memory_space=pl.ANY)
    return pl.pallas_call(
        _ffn_kernel,
        grid_spec=pltpu.PrefetchScalarGridSpec(
            num_scalar_prefetch=3,
            grid=(n_blocks,),
            in_specs=[
                pl.BlockSpec((blk, LANES), xrow),
                any_spec,
                pl.BlockSpec((1, 1, de), bsel),
                any_spec,
                pl.BlockSpec((1, 1, de), bsel),
                any_spec,
                pl.BlockSpec((1, 1, d), bsel),
            ],
            out_specs=pl.BlockSpec((blk, LANES), lambda j, be, nu, nxt: (j, 0)),
            scratch_shapes=[pltpu.VMEM((d, de), F32), pltpu.VMEM((d, de), F32), pltpu.VMEM((de, d), F32),
                            pltpu.VMEM((d, de), BF16), pltpu.VMEM((d, de), BF16), pltpu.VMEM((de, d), BF16),
                            pltpu.SemaphoreType.DMA((3,))],
        ),
        out_shape=jax.ShapeDtypeStruct(xb.shape, F32),
        compiler_params=pltpu.CompilerParams(vmem_limit_bytes=VMEM_LIMIT),
        name="ffn",
    )(block_e, n_used, next_e, xb, w_gate, b_gate, w_up, b_up, w_down, b_down)


def _combine_kernel(tiles_per_sample, pos_ref, x1_ref, route_ref, mod_ref, yb_hbm, o_ref, buf, sem):
    i = pl.program_id(0)
    n = pl.num_programs(0)
    rows = TOP_K * CMB_TOK
    d = x1_ref.shape[1]
    sub = d // LANES

    def issue(tile, slot):
        def body(g, carry):
            for u in range(ISSUE_UNROLL):
                r = g * ISSUE_UNROLL + u
                _row_tile_copy(yb_hbm, pos_ref[tile * rows + r], buf.at[slot], r, sub,
                               sem.at[slot]).start(priority=u % DMA_QUEUES)
            return carry
        lax.fori_loop(0, rows // ISSUE_UNROLL, body, 0)

    @pl.when(i == 0)
    def _():
        issue(0, 0)

    @pl.when(i + 1 < n)
    def _():
        issue(i + 1, (i + 1) % 2)

    slot = i % 2
    pltpu.make_async_copy(yb_hbm.at[pl.ds(0, rows * sub)], buf.at[slot], sem.at[slot]).wait()

    b = i // tiles_per_sample
    g_m = mod_ref[b][5:6, :]
    route = route_ref[...]
    acc = jnp.zeros(x1_ref.shape, F32)
    for k in range(TOP_K):
        acc = acc + route[:, TOP_K + k:TOP_K + k + 1] * _from_row_tiles(buf.at[slot], k * CMB_TOK, CMB_TOK, d)
    o_ref[...] = x1_ref[...] + g_m * acc


def _combine(pos_tiles, x1, route, mod, yb, tiles_per_sample):
    t, d = x1.shape
    row = lambda i, pos: (i, 0)
    return pl.pallas_call(
        functools.partial(_combine_kernel, tiles_per_sample),
        grid_spec=pltpu.PrefetchScalarGridSpec(
            num_scalar_prefetch=1,
            grid=(t // CMB_TOK,),
            in_specs=[
                pl.BlockSpec((CMB_TOK, d), row),
                pl.BlockSpec((CMB_TOK, route.shape[1]), row),
                pl.BlockSpec(mod.shape, lambda i, pos: (0, 0, 0)),
                pl.BlockSpec(memory_space=pl.ANY),
            ],
            out_specs=pl.BlockSpec((CMB_TOK, d), row),
            scratch_shapes=[pltpu.VMEM((2, TOP_K * CMB_TOK * d // LANES, LANES), F32),
                            pltpu.SemaphoreType.DMA((2,))],
        ),
        out_shape=jax.ShapeDtypeStruct((t, d), F32),
        compiler_params=pltpu.CompilerParams(vmem_limit_bytes=VMEM_LIMIT),
        name="combine",
    )(pos_tiles, x1, route, mod, yb)


def _rope_tables(s):
    rows = s // GRID_W
    row = jnp.repeat(jnp.arange(rows, dtype=F32), GRID_W)
    col = jnp.tile(jnp.arange(GRID_W, dtype=F32), rows)
    inv = 1.0 / (ROPE_THETA ** (jnp.arange(AXIS_PAIRS, dtype=F32) / AXIS_PAIRS))
    ang = jnp.concatenate([row[:, None] * inv, col[:, None] * inv], axis=-1)
    cos, sin = jnp.cos(ang), jnp.sin(ang)
    cos_i = jnp.repeat(cos, 2, axis=-1)
    sin_i = jnp.stack([-sin, sin], axis=-1).reshape(s, HEAD_DIM)
    reps = LANES // HEAD_DIM
    return jnp.tile(cos_i, (1, reps)), jnp.tile(sin_i, (1, reps))


def kernel(x, c, ctx, c_ctx, w_ada, b_ada, norm_attn, w_in, q_norm_a, k_norm_a, q_norm_b, k_norm_b, lambda_q1,
           lambda_k1, lambda_q2, lambda_k2, subln_b, w_oa, w_ob, w_out, norm_mlp, router_w, router_b, w_gate,
           b_gate, w_up, b_up, w_down, b_down):
    bsz, s, d = x.shape
    assert ctx.shape[1] == TOK_TILE and s % MERGE_TOK == 0 and s % TQ_A == 0 and s % TQ_B == 0 and s % GRID_W == 0
    assert w_ada.shape[0] == 1, "single-layer block"
    t = bsz * s
    assert t % CMB_TOK == 0 and (t * TOP_K) % FFN_ROWS == 0

    mod_rows = 8
    assert bsz + 1 <= mod_rows
    cvec = jnp.concatenate([c, c_ctx[None, :], jnp.zeros((mod_rows - bsz - 1, d), F32)], axis=0)
    mod = _adaln(cvec, w_ada[0], b_ada[0][None, :]).reshape(mod_rows, N_MOD, d)

    cos, sin = _rope_tables(s)
    tile_gain = lambda g: jnp.tile(g, LANES // HEAD_DIM)[None, :]
    gidx = jnp.arange(MXU_DIM) // HEAD_DIM
    gmat = jnp.where(gidx[:, None] == gidx[None, :], 1.0 / HEAD_DIM, 0.0).astype(BF16)
    w_qkv = w_in[0][:, :QKV_COLS].astype(BF16)
    w_g = w_in[0][:, QKV_COLS:].astype(BF16)

    qat, ka, vat, qbt, kb, vbt = _project(
        x, ctx, mod, norm_attn, w_qkv, cos, sin, tile_gain(q_norm_a[0]), tile_gain(k_norm_a[0]),
        tile_gain(q_norm_b[0]), tile_gain(k_norm_b[0]), gmat)

    ya = _attn_a(qat, ka, vat)
    yb = _attn_b(qbt, kb, vbt, lambda_q1, lambda_k1, lambda_q2, lambda_k2, subln_b[0][:, None])

    tri = jnp.arange(TOK_TILE)
    utri = (tri[:, None] < tri[None, :]).astype(BF16)
    x1, h2, route_t, counts = _merge(
        x.reshape(t, d), ya.reshape(t, A_Q_COLS), yb.reshape(t, B_WIDTH), mod, norm_attn, norm_mlp, w_g,
        w_oa[0].astype(BF16), w_ob[0].astype(BF16), w_out[0].astype(BF16), router_w[0].T, router_b[0][:, None], utri,
        s // MERGE_TOK)

    counts = counts[:, 0].astype(I32)
    padded = (counts + FFN_ROWS - 1) // FFN_ROWS * FFN_ROWS
    pad_end = jnp.cumsum(padded)
    pad_start = pad_end - padded
    top_e = route_t[:TOP_K].astype(I32)
    rank = route_t[2 * TOP_K:3 * TOP_K].astype(I32)
    experts = jnp.arange(N_EXPERTS, dtype=I32)
    start_of = jnp.sum(jnp.where(top_e[..., None] == experts, pad_start, 0), axis=-1)
    pos_t = start_of + rank
    n_blocks = t * TOP_K // FFN_ROWS + N_EXPERTS
    block_start = jnp.arange(n_blocks, dtype=I32) * FFN_ROWS
    block_e = jnp.minimum(jnp.sum((pad_end[None, :] <= block_start[:, None]).astype(I32), axis=1), N_EXPERTS - 1)
    n_used = (pad_end[-1:] // FFN_ROWS).astype(I32)
    first_at_or_after = lax.cummin(jnp.where(padded > 0, jnp.arange(N_EXPERTS, dtype=I32), N_EXPERTS), reverse=True)
    next_nonempty = jnp.concatenate([first_at_or_after[1:], jnp.full((1,), N_EXPERTS, I32)])
    next_of = jnp.where(next_nonempty < N_EXPERTS, next_nonempty, -1)
    next_e = jnp.sum(jnp.where(block_e[:, None] == experts, next_of, 0), axis=-1)

    xb = _dispatch(pos_t.reshape(-1), pad_start + counts, padded - counts, n_used, h2, n_blocks)
    de = w_gate.shape[-1]
    yb_rows = _ffn(block_e, n_used, next_e, xb, w_gate[0], b_gate[0].reshape(N_EXPERTS, 1, de), w_up[0],
                   b_up[0].reshape(N_EXPERTS, 1, de), w_down[0], b_down[0].reshape(N_EXPERTS, 1, d))
    pos_tiles = pos_t.reshape(TOP_K, t // CMB_TOK, CMB_TOK).transpose(1, 0, 2).reshape(-1)
    out = _combine(pos_tiles, x1, route_t.T, mod, yb_rows, s // CMB_TOK)
    return out.reshape(bsz, s, d)
```

```python
import functools
import math

import jax
import jax.numpy as jnp
from jax import lax
from jax.experimental import pallas as pl
from jax.experimental.pallas import tpu as pltpu

F32 = jnp.float32
BF16 = jnp.bfloat16
I32 = jnp.int32

GRID_W = 64
HEAD_DIM = 64
A_HEADS = 8
A_KV_HEADS = 2
A_GROUP = A_HEADS // A_KV_HEADS
B_HEADS = 4
B_V_DIM = 2 * HEAD_DIM
N_EXPERTS = 32
TOP_K = 4
N_MOD = 6
ROPE_THETA = 10000.0
AXIS_PAIRS = HEAD_DIM // 4
SWIGLU_LIMIT = 7.0
SWIGLU_ALPHA = 1.702
EPS = 1e-6
SUBLN_EPS = 1e-5
LAM_INIT = 0.8 - 0.6 * math.exp(-0.3 * 0)

A_Q_COLS = A_HEADS * HEAD_DIM
A_KV_COLS = A_KV_HEADS * HEAD_DIM
B_QK_COLS = B_HEADS * 2 * HEAD_DIM
B_WIDTH = B_HEADS * B_V_DIM
QKV_COLS = A_Q_COLS + 2 * A_KV_COLS + 2 * B_QK_COLS + B_WIDTH

LANES = 128
MXU_DIM = 256
TOK_TILE = 256
MERGE_TOK = 4 * TOK_TILE
TQ_A = 256
TQ_B = 1024
SCORE_TILES_PER_TRIP = 512
ONES_ROWS = 16
A_VROWS = HEAD_DIM + ONES_ROWS
B_VROWS = B_V_DIM + ONES_ROWS
ADA_COLS = 1536
FFN_ROWS = 256
CMB_TOK = 128
ROUTE_ROWS = 16
ISSUE_UNROLL = 16
DMA_QUEUES = 2
VMEM_LIMIT = 56 * 1024 * 1024


def _dot(a, b):
    return jnp.dot(a, b, preferred_element_type=F32)


def _split_bf16(a):
    hi = a.astype(BF16)
    lo = (a - hi.astype(F32)).astype(BF16)
    return hi, lo


def _dot3(a, b):
    a_hi, a_lo = _split_bf16(a)
    b_hi, b_lo = _split_bf16(b)
    return _dot(a_hi, b_hi) + (_dot(a_hi, b_lo) + _dot(a_lo, b_hi))


def _dot3_nt(a, b):
    dims = (((1,), (1,)), ((), ()))
    dot_nt = lambda u, v: lax.dot_general(u, v, dims, preferred_element_type=F32)
    a_hi, a_lo = _split_bf16(a)
    b_hi, b_lo = _split_bf16(b)
    return dot_nt(a_hi, b_hi) + (dot_nt(a_hi, b_lo) + dot_nt(a_lo, b_hi))


def _rms_rows(x, gain, eps):
    ms = jnp.mean(x * x, axis=-1, keepdims=True)
    return x * lax.rsqrt(ms + eps) * gain


def _adaln_kernel(c_ref, w_ref, b_ref, o_ref):
    c = c_ref[...]
    a = c * jax.nn.sigmoid(c)
    o_ref[...] = _dot3(a, w_ref[...]) + b_ref[...]


def _adaln(cvec, w_ada, b_ada):
    rows, d = cvec.shape
    n = w_ada.shape[1]
    return pl.pallas_call(
        _adaln_kernel,
        grid=(n // ADA_COLS,),
        in_specs=[
            pl.BlockSpec((rows, d), lambda j: (0, 0)),
            pl.BlockSpec((d, ADA_COLS), lambda j: (0, j)),
            pl.BlockSpec((1, ADA_COLS), lambda j: (0, j)),
        ],
        out_specs=pl.BlockSpec((rows, ADA_COLS), lambda j: (0, j)),
        out_shape=jax.ShapeDtypeStruct((rows, n), F32),
        compiler_params=pltpu.CompilerParams(vmem_limit_bytes=VMEM_LIMIT),
        name="adaln",
    )(cvec, w_ada, b_ada)


def _group_rms(y, gmat, gain):
    w_total = y.shape[1]
    outs = []
    for c0 in range(0, w_total, MXU_DIM):
        w = min(MXU_DIM, w_total - c0)
        yc = y[:, c0:c0 + w]
        ms = _dot((yc * yc).astype(BF16), gmat[:w, :w])
        reps = w // LANES
        g = gain if reps == 1 else jnp.concatenate([gain] * reps, axis=1)
        outs.append(yc * lax.rsqrt(ms + EPS) * g)
    return outs[0] if len(outs) == 1 else jnp.concatenate(outs, axis=1)


def _rope(y, cos, sin):
    rows = y.shape[0]
    lane = lax.broadcasted_iota(I32, (rows, LANES), 1)
    even = (lane % 2) == 0
    outs = []
    for c in range(y.shape[1] // LANES):
        ch = y[:, c * LANES:(c + 1) * LANES]
        partner = jnp.where(even, pltpu.roll(ch, LANES - 1, 1), pltpu.roll(ch, 1, 1))
        outs.append(ch * cos + partner * sin)
    return outs[0] if len(outs) == 1 else jnp.concatenate(outs, axis=1)


def _proj_kernel(x_ref, ctx_ref, mod_ref, gn_ref, w_ref, cos_ref, sin_ref, qna_ref, kna_ref, qnb_ref, knb_ref,
                 gmat_ref, qat_ref, ka_ref, vat_ref, qbt_ref, kb_ref, vbt_ref):
    b = pl.program_id(0)
    j = pl.program_id(1)
    is_ctx = j == 0
    xt = jnp.where(is_ctx, ctx_ref[0], x_ref[0])
    mrow = jnp.where(is_ctx, pl.num_programs(0), b)
    mod = mod_ref[mrow]
    h = _rms_rows(xt, gn_ref[...], EPS) * (1.0 + mod[1:2, :]) + mod[0:1, :]
    y = _dot(h.astype(BF16), w_ref[...])
    cos = jnp.where(is_ctx, 1.0, cos_ref[...])
    sin = jnp.where(is_ctx, 0.0, sin_ref[...])
    gmat = gmat_ref[...]
    scale = HEAD_DIM ** -0.5 * math.log2(math.e)

    o = 0
    qa = _rope(_group_rms(y[:, o:o + A_Q_COLS], gmat, qna_ref[...]), cos, sin) * scale
    o += A_Q_COLS
    ka = _rope(_group_rms(y[:, o:o + A_KV_COLS], gmat, kna_ref[...]), cos, sin)
    o += A_KV_COLS
    va = y[:, o:o + A_KV_COLS]
    o += A_KV_COLS
    qb = _rope(_group_rms(y[:, o:o + B_QK_COLS], gmat, qnb_ref[...]), cos, sin) * scale
    o += B_QK_COLS
    kb = _rope(_group_rms(y[:, o:o + B_QK_COLS], gmat, knb_ref[...]), cos, sin)
    o += B_QK_COLS
    vb = y[:, o:o + B_WIDTH]

    @pl.when(j > 0)
    def _():
        qat_ref[0] = qa.T.astype(BF16)
        qbt_ref[0] = qb.T.astype(BF16)

    ka_ref[0] = ka.astype(BF16)
    kb_ref[0] = kb.astype(BF16)
    ones = jnp.ones((ONES_ROWS, xt.shape[0]), BF16)
    for dst_ref, vt, width in ((vat_ref, va.T.astype(BF16), HEAD_DIM), (vbt_ref, vb.T.astype(BF16), B_V_DIM)):
        stride = width + ONES_ROWS
        for hd in range(vt.shape[0] // width):
            dst_ref[0, 0, hd * stride:hd * stride + width, :] = vt[hd * width:(hd + 1) * width, :]
            dst_ref[0, 0, hd * stride + width:(hd + 1) * stride, :] = ones


def _project(x, ctx, mod, gn, w_qkv, cos, sin, qna, kna, qnb, knb, gmat):
    bsz, s, d = x.shape
    n_lat = s // TOK_TILE
    n_key = n_lat + 1
    lk = n_key * TOK_TILE
    lat = lambda b, j: (b, jnp.maximum(j - 1, 0), 0)
    const2 = lambda b, j: (0, 0)
    tab = lambda b, j: (jnp.maximum(j - 1, 0), 0)
    return pl.pallas_call(
        _proj_kernel,
        grid=(bsz, n_key),
        in_specs=[
            pl.BlockSpec((1, TOK_TILE, d), lat),
            pl.BlockSpec((1, TOK_TILE, d), lambda b, j: (b, 0, 0)),
            pl.BlockSpec(mod.shape, lambda b, j: (0, 0, 0)),
            pl.BlockSpec((1, d), const2),
            pl.BlockSpec((d, QKV_COLS), const2),
            pl.BlockSpec((TOK_TILE, LANES), tab),
            pl.BlockSpec((TOK_TILE, LANES), tab),
            pl.BlockSpec((1, LANES), const2),
            pl.BlockSpec((1, LANES), const2),
            pl.BlockSpec((1, LANES), const2),
            pl.BlockSpec((1, LANES), const2),
            pl.BlockSpec((MXU_DIM, MXU_DIM), const2),
        ],
        out_specs=[
            pl.BlockSpec((1, A_Q_COLS, TOK_TILE), lambda b, j: (b, 0, jnp.maximum(j - 1, 0))),
            pl.BlockSpec((1, TOK_TILE, A_KV_COLS), lambda b, j: (b, j, 0)),
            pl.BlockSpec((1, 1, A_KV_HEADS * A_VROWS, TOK_TILE), lambda b, j: (b, j, 0, 0)),
            pl.BlockSpec((1, B_QK_COLS, TOK_TILE), lambda b, j: (b, 0, jnp.maximum(j - 1, 0))),
            pl.BlockSpec((1, TOK_TILE, B_QK_COLS), lambda b, j: (b, j, 0)),
            pl.BlockSpec((1, 1, B_HEADS * B_VROWS, TOK_TILE), lambda b, j: (b, j, 0, 0)),
        ],
        out_shape=[
            jax.ShapeDtypeStruct((bsz, A_Q_COLS, s), BF16),
            jax.ShapeDtypeStruct((bsz, lk, A_KV_COLS), BF16),
            jax.ShapeDtypeStruct((bsz, n_key, A_KV_HEADS * A_VROWS, TOK_TILE), BF16),
            jax.ShapeDtypeStruct((bsz, B_QK_COLS, s), BF16),
            jax.ShapeDtypeStruct((bsz, lk, B_QK_COLS), BF16),
            jax.ShapeDtypeStruct((bsz, n_key, B_HEADS * B_VROWS, TOK_TILE), BF16),
        ],
        compiler_params=pltpu.CompilerParams(vmem_limit_bytes=VMEM_LIMIT),
        name="proj",
    )(x, ctx, mod, gn, w_qkv, cos, sin, qna, kna, qnb, knb, gmat)


def _flash(k_ref, vt_ref, wq, pv, dv, s_even, s_odd):
    n = wq.shape[1]
    n_tiles = n // MXU_DIM
    n_chunks = vt_ref.shape[1]
    cols = [slice(t * MXU_DIM, (t + 1) * MXU_DIM) for t in range(n_tiles)]
    wqs = [wq[:, cs] for cs in cols]

    def scores(c, t, dst):
        start = pl.multiple_of(c * TOK_TILE, TOK_TILE)
        dst[:, cols[t]] = _dot(k_ref[0, pl.ds(start, TOK_TILE), :], wqs[t])

    def chunk(c, cur, nxt, state):
        out = []
        for t in range(n_tiles):
            if nxt is not None:
                scores(c + 1, t, nxt)
            m, acc = state[t]
            s = cur[:, cols[t]]
            m_new = jnp.maximum(m, jnp.max(s, axis=0, keepdims=True))
            alpha = jnp.exp2(m - m_new)
            p = jnp.exp2(s - m_new).astype(BF16)
            out.append((m_new, alpha * acc + pv(t, vt_ref[0, c], p)))
        return tuple(out)

    for t in range(n_tiles):
        scores(0, t, s_even)

    bufs = (s_even, s_odd)

    per_trip = SCORE_TILES_PER_TRIP // n_tiles
    assert per_trip % 2 == 0, "the two score buffers alternate, so a trip must hold an even number of chunks"

    def trip(i, state):
        for u in range(per_trip):
            state = chunk(i * per_trip + u, bufs[u % 2], bufs[(u + 1) % 2], state)
        return state

    init = tuple((jnp.full((1, MXU_DIM), -jnp.inf, F32), jnp.zeros((dv + ONES_ROWS, MXU_DIM), F32))
                 for _ in range(n_tiles))
    n_trips = (n_chunks - 1) // per_trip
    state = lax.fori_loop(0, n_trips, trip, init)
    for c in range(n_trips * per_trip, n_chunks):
        state = chunk(c, bufs[c % 2], bufs[(c + 1) % 2] if c + 1 < n_chunks else None, state)
    return jnp.concatenate([acc for _, acc in state], axis=1)


def _attn_a_kernel(qt_ref, k_ref, vt_ref, o_ref, s_even, s_odd):
    tq = qt_ref.shape[2]
    half = A_GROUP * tq
    zeros = jnp.zeros((HEAD_DIM, half), BF16)
    rows = []
    for g in range(A_KV_HEADS):
        heads = range(g * A_GROUP, (g + 1) * A_GROUP)
        qg = jnp.concatenate([qt_ref[0, h * HEAD_DIM:(h + 1) * HEAD_DIM, :] for h in heads], axis=1)
        rows.append(jnp.concatenate([qg if gg == g else zeros for gg in range(A_KV_HEADS)], axis=1))
    wq = jnp.concatenate(rows, axis=0)

    def pv(t, vc, p):
        g = t * MXU_DIM // half
        return _dot(vc[g * A_VROWS:(g + 1) * A_VROWS], p)

    acc = _flash(k_ref, vt_ref, wq, pv, HEAD_DIM, s_even, s_odd)
    o = acc[:HEAD_DIM] / acc[HEAD_DIM:HEAD_DIM + 1]
    outs = [o[:, h * tq:(h + 1) * tq].T for h in range(A_HEADS)]
    o_ref[0] = jnp.concatenate(outs, axis=1).astype(o_ref.dtype)


def _attn_a(qat, ka, vat):
    bsz, _, s = qat.shape
    lk = ka.shape[1]
    n_chunks = vat.shape[1]
    return pl.pallas_call(
        _attn_a_kernel,
        grid=(bsz, s // TQ_A),
        in_specs=[
            pl.BlockSpec((1, A_Q_COLS, TQ_A), lambda b, i: (b, 0, i)),
            pl.BlockSpec((1, lk, A_KV_COLS), lambda b, i: (b, 0, 0)),
            pl.BlockSpec((1, n_chunks, A_KV_HEADS * A_VROWS, TOK_TILE), lambda b, i: (b, 0, 0, 0)),
        ],
        out_specs=pl.BlockSpec((1, TQ_A, A_Q_COLS), lambda b, i: (b, i, 0)),
        out_shape=jax.ShapeDtypeStruct((bsz, s, A_Q_COLS), BF16),
        scratch_shapes=[pltpu.VMEM((TOK_TILE, A_HEADS * TQ_A), F32)] * 2,
        compiler_params=pltpu.CompilerParams(vmem_limit_bytes=VMEM_LIMIT),
        name="attn_a",
    )(qat, ka, vat)


def _attn_b_kernel(qt_ref, k_ref, vt_ref, lq1_ref, lk1_ref, lq2_ref, lk2_ref, sub_ref, o_ref, s_even, s_odd):
    tq = qt_ref.shape[2]
    zeros = jnp.zeros((HEAD_DIM, tq), BF16)
    q1 = qt_ref[0, :HEAD_DIM, :]
    q2 = qt_ref[0, HEAD_DIM:, :]
    wq = jnp.concatenate([jnp.concatenate([q1, zeros], axis=1), jnp.concatenate([zeros, q2], axis=1)], axis=0)
    acc = _flash(k_ref, vt_ref, wq, lambda t, vc, p: _dot(vc, p), B_V_DIM, s_even, s_odd)
    r = acc[:B_V_DIM] / acc[B_V_DIM:B_V_DIM + 1]
    lam = (jnp.exp(jnp.sum(lq1_ref[...] * lk1_ref[...], axis=1, keepdims=True))
           - jnp.exp(jnp.sum(lq2_ref[...] * lk2_ref[...], axis=1, keepdims=True)) + LAM_INIT)
    o = r[:, :tq] - lam * r[:, tq:]
    ms = jnp.mean(o * o, axis=0, keepdims=True)
    o = o * lax.rsqrt(ms + SUBLN_EPS) * sub_ref[...] * (1.0 - LAM_INIT)
    o_ref[0] = o.T.astype(o_ref.dtype)


def _attn_b(qbt, kb, vbt, lq1, lk1, lq2, lk2, subln):
    bsz, _, s = qbt.shape
    lk = kb.shape[1]
    n_chunks = vbt.shape[1]
    vec = pl.BlockSpec((1, HEAD_DIM), lambda b, h, i: (0, 0))
    return pl.pallas_call(
        _attn_b_kernel,
        grid=(bsz, B_HEADS, s // TQ_B),
        in_specs=[
            pl.BlockSpec((1, 2 * HEAD_DIM, TQ_B), lambda b, h, i: (b, h, i)),
            pl.BlockSpec((1, lk, 2 * HEAD_DIM), lambda b, h, i: (b, 0, h)),
            pl.BlockSpec((1, n_chunks, B_VROWS, TOK_TILE), lambda b, h, i: (b, 0, h, 0)),
            vec, vec, vec, vec,
            pl.BlockSpec((B_V_DIM, 1), lambda b, h, i: (0, 0)),
        ],
        out_specs=pl.BlockSpec((1, TQ_B, B_V_DIM), lambda b, h, i: (b, i, h)),
        out_shape=jax.ShapeDtypeStruct((bsz, s, B_WIDTH), BF16),
        scratch_shapes=[pltpu.VMEM((TOK_TILE, 2 * TQ_B), F32)] * 2,
        compiler_params=pltpu.CompilerParams(vmem_limit_bytes=VMEM_LIMIT),
        name="attn_b",
    )(qbt, kb, vbt, lq1, lk1, lq2, lk2, subln)


def _merge_kernel(tiles_per_sample, x_ref, ya_ref, yb_ref, mod_ref, gna_ref, gnm_ref, wg_ref, woa_ref, wob_ref,
                  wout_ref, rw_ref, rb_ref, utri_ref, x1_ref, h2_ref, route_ref, cnt_ref, carry_ref):
    i = pl.program_id(0)

    @pl.when(i == 0)
    def _():
        carry_ref[...] = jnp.zeros_like(carry_ref)

    mod = mod_ref[i // tiles_per_sample]
    d = x_ref.shape[1]
    subs = [slice(u * TOK_TILE, (u + 1) * TOK_TILE) for u in range(x_ref.shape[0] // TOK_TILE)]
    xs = [x_ref[r, :] for r in subs]
    hs = [(_rms_rows(x, gna_ref[...], EPS) * (1.0 + mod[1:2, :]) + mod[0:1, :]).astype(BF16) for x in xs]
    gates = [_dot(h, wg_ref[...]) for h in hs]
    branch = [(_dot(ya_ref[r, :], woa_ref[...]), _dot(yb_ref[r, :], wob_ref[...])) for r in subs]
    merged = [(jax.nn.sigmoid(g[:, :d]) * pa + jax.nn.sigmoid(g[:, d:]) * pb).astype(BF16)
              for g, (pa, pb) in zip(gates, branch)]
    x1s = [x + mod[2:3, :] * _dot(m, wout_ref[...]) for x, m in zip(xs, merged)]
    h2s = [_rms_rows(x1, gnm_ref[...], EPS) * (1.0 + mod[4:5, :]) + mod[3:4, :] for x1 in x1s]
    for r, x1, h2 in zip(subs, x1s, h2s):
        x1_ref[r, :] = x1
        h2_ref[r, :] = h2

    all_logits = [_dot3_nt(rw_ref[...], h2) + rb_ref[...] for h2 in h2s]
    eiota = lax.broadcasted_iota(I32, (N_EXPERTS, TOK_TILE), 0)
    riota = lax.broadcasted_iota(I32, (ROUTE_ROWS, TOK_TILE), 0)
    carry = carry_ref[...]
    for r, logits in zip(subs, all_logits):
        work = logits
        idxs, vals = [], []
        for _ in range(TOP_K):
            mx = jnp.max(work, axis=0, keepdims=True)
            idx = jnp.min(jnp.where(work == mx, eiota, N_EXPERTS), axis=0, keepdims=True)
            idxs.append(idx)
            vals.append(mx)
            work = jnp.where(eiota == idx, -jnp.inf, work)
        exps = [jnp.exp(v - vals[0]) for v in vals]
        denom = exps[0] + exps[1] + exps[2] + exps[3]
        weights = [e / denom for e in exps]

        hits = [(eiota == idx) for idx in idxs]
        onehot = (hits[0] | hits[1] | hits[2] | hits[3]).astype(F32)
        before = _dot(onehot.astype(BF16), utri_ref[...]) + carry
        ranks = [jnp.sum(jnp.where(hit, before, 0.0), axis=0, keepdims=True) for hit in hits]
        carry = carry + jnp.sum(onehot, axis=1, keepdims=True)

        lines = [v.astype(F32) for v in idxs] + weights + ranks
        route = jnp.zeros((ROUTE_ROWS, TOK_TILE), F32)
        for k, line in enumerate(lines):
            route = jnp.where(riota == k, line, route)
        route_ref[:, r] = route
    carry_ref[...] = carry
    cnt_ref[...] = carry


def _merge(x2, ya2, yb2, mod, gna, gnm, wg, woa, wob, wout, rw, rb, utri, tiles_per_sample):
    t, d = x2.shape
    const2 = lambda i: (0, 0)
    row = lambda i: (i, 0)
    full = lambda a: pl.BlockSpec(a.shape, const2)
    return pl.pallas_call(
        functools.partial(_merge_kernel, tiles_per_sample),
        grid=(t // MERGE_TOK,),
        in_specs=[
            pl.BlockSpec((MERGE_TOK, d), row),
            pl.BlockSpec((MERGE_TOK, ya2.shape[1]), row),
            pl.BlockSpec((MERGE_TOK, yb2.shape[1]), row),
            pl.BlockSpec(mod.shape, lambda i: (0, 0, 0)),
            full(gna), full(gnm), full(wg), full(woa), full(wob), full(wout), full(rw), full(rb), full(utri),
        ],
        out_specs=[
            pl.BlockSpec((MERGE_TOK, d), row),
            pl.BlockSpec((MERGE_TOK, d), row),
            pl.BlockSpec((ROUTE_ROWS, MERGE_TOK), lambda i: (0, i)),
            pl.BlockSpec((N_EXPERTS, 1), const2),
        ],
        out_shape=[
            jax.ShapeDtypeStruct((t, d), F32),
            jax.ShapeDtypeStruct((t, d), F32),
            jax.ShapeDtypeStruct((ROUTE_ROWS, t), F32),
            jax.ShapeDtypeStruct((N_EXPERTS, 1), F32),
        ],
        scratch_shapes=[pltpu.VMEM((N_EXPERTS, 1), F32)],
        compiler_params=pltpu.CompilerParams(vmem_limit_bytes=VMEM_LIMIT),
        name="merge",
    )(x2, ya2, yb2, mod, gna, gnm, wg, woa, wob, wout, rw, rb, utri)


def _to_row_tiles(dst_ref, val):
    n, d = val.shape
    sub = d // LANES
    for c in range(sub):
        dst_ref[pl.ds(c, n, stride=sub), :] = val[:, c * LANES:(c + 1) * LANES]


def _from_row_tiles(src_ref, first_row, n, d):
    sub = d // LANES
    return jnp.concatenate([src_ref[pl.ds(first_row * sub + c, n, stride=sub), :] for c in range(sub)], axis=1)


def _row_tile_copy(src, src_row, dst, dst_row, sub, sem):
    src_at = pl.ds(pl.multiple_of(src_row * sub, sub), sub)
    dst_at = pl.ds(pl.multiple_of(dst_row * sub, sub), sub)
    return pltpu.make_async_copy(src.at[src_at], dst.at[dst_at], sem)


def _dispatch_kernel(pos_ref, padfrom_ref, padcnt_ref, nu_ref, h_ref, xb_out, tiles, zeros, sem, zsem):
    i = pl.program_id(0)
    n = pl.num_programs(0)
    slot = i % 2
    n_tok, d = h_ref.shape
    sub = d // LANES
    blk = zeros.shape[0]
    n_blocks = xb_out.shape[0] // blk

    def zero_fill(wait):
        def go(copy):
            copy.wait() if wait else copy.start()

        def per_expert(e, carry):
            cnt = padcnt_ref[e]
            for bit in reversed(range((FFN_ROWS - 1).bit_length())):
                rows = 1 << bit

                @pl.when((cnt & rows) != 0)
                def _():
                    done = (cnt >> (bit + 1)) << (bit + 1)
                    at = pl.multiple_of((padfrom_ref[e] + done) * sub, sub)
                    go(pltpu.make_async_copy(zeros.at[pl.ds(0, rows * sub)], xb_out.at[pl.ds(at, rows * sub)], zsem))
            return carry

        lax.fori_loop(0, N_EXPERTS, per_expert, 0)

        def per_block(j, carry):
            go(pltpu.make_async_copy(zeros, xb_out.at[pl.ds(pl.multiple_of(j * blk, blk), blk)], zsem))
            return carry

        lax.fori_loop(nu_ref[0], n_blocks, per_block, 0)

    @pl.when(i == 0)
    def _():
        zeros[...] = jnp.zeros_like(zeros)
        zero_fill(wait=False)
        zero_fill(wait=True)

    def drain(sl):
        for _ in range(TOP_K):
            pltpu.make_async_copy(tiles.at[sl], xb_out.at[pl.ds(0, n_tok * sub)], sem.at[sl]).wait()

    @pl.when(i >= 2)
    def _():
        drain(slot)

    _to_row_tiles(tiles.at[slot], h_ref[...])
    base = i * n_tok
    n_all = n * n_tok

    def issue(t, carry):
        for k in range(TOP_K):
            _row_tile_copy(tiles.at[slot], t, xb_out, pos_ref[k * n_all + base + t], sub,
                           sem.at[slot]).start(priority=k % DMA_QUEUES)
        return carry

    lax.fori_loop(0, n_tok, issue, 0, unroll=ISSUE_UNROLL // TOP_K)

    @pl.when((i == n - 1) & (n >= 2))
    def _():
        drain(1 - slot)

    @pl.when(i == n - 1)
    def _():
        drain(slot)


def _dispatch(pos_flat, pad_from, pad_cnt, n_used, h2, n_blocks):
    t, d = h2.shape
    sub = d // LANES
    return pl.pallas_call(
        _dispatch_kernel,
        grid_spec=pltpu.PrefetchScalarGridSpec(
            num_scalar_prefetch=4,
            grid=(t // TOK_TILE,),
            in_specs=[pl.BlockSpec((TOK_TILE, d), lambda i, *_: (i, 0))],
            out_specs=pl.BlockSpec(memory_space=pl.ANY),
            scratch_shapes=[pltpu.VMEM((2, TOK_TILE * sub, LANES), F32), pltpu.VMEM((FFN_ROWS * sub, LANES), F32),
                            pltpu.SemaphoreType.DMA((2,)), pltpu.SemaphoreType.DMA(())],
        ),
        out_shape=jax.ShapeDtypeStruct((n_blocks * FFN_ROWS * sub, LANES), F32),
        compiler_params=pltpu.CompilerParams(vmem_limit_bytes=VMEM_LIMIT),
        name="dispatch",
    )(pos_flat, pad_from, pad_cnt, n_used, h2)


def _ffn_kernel(be_ref, nu_ref, nxt_ref, x_ref, wg_hbm, bg_ref, wu_hbm, bu_ref, wd_hbm, bd_ref, o_ref,
                stage_g, stage_u, stage_d, wg_s, wu_s, wd_s, wsem):
    j = pl.program_id(0)
    prev = be_ref[jnp.maximum(j - 1, 0)]
    fresh = ((j == 0) | (be_ref[j] != prev)) & (j < nu_ref[0])
    streams = ((wg_hbm, stage_g, wg_s), (wu_hbm, stage_u, wu_s), (wd_hbm, stage_d, wd_s))

    def weight_copies(expert):
        return [pltpu.make_async_copy(hbm.at[expert], stage, wsem.at[n]) for n, (hbm, stage, _) in enumerate(streams)]

    @pl.when(j == 0)
    def _():
        for copy in weight_copies(be_ref[0]):
            copy.start()

    @pl.when(fresh)
    def _():
        for copy, (_, stage, dst) in zip(weight_copies(be_ref[j]), streams):
            copy.wait()
            dst[...] = stage[...].astype(BF16)

        @pl.when(nxt_ref[j] >= 0)
        def _():
            for copy in weight_copies(nxt_ref[j]):
                copy.start()

    @pl.when(j < nu_ref[0])
    def _():
        x = _from_row_tiles(x_ref, 0, FFN_ROWS, wg_s.shape[0]).astype(BF16)
        gate = jnp.minimum(_dot(x, wg_s[...]) + bg_ref[0], SWIGLU_LIMIT)
        up = jnp.clip(_dot(x, wu_s[...]) + bu_ref[0], -SWIGLU_LIMIT, SWIGLU_LIMIT)
        act = (up + 1.0) * (gate * jax.nn.sigmoid(SWIGLU_ALPHA * gate))
        _to_row_tiles(o_ref, _dot(act.astype(BF16), wd_s[...]) + bd_ref[0])

    @pl.when(j >= nu_ref[0])
    def _():
        o_ref[...] = jnp.zeros_like(o_ref)


def _ffn(block_e, n_used, next_e, xb, w_gate, b_gate, w_up, b_up, w_down, b_down):
    d, de = w_gate.shape[1:]
    blk = FFN_ROWS * d // LANES
    n_blocks = xb.shape[0] // blk
    xrow = lambda j, be, nu, nxt: (jnp.minimum(j, nu[0] - 1), 0)
    bsel = lambda j, be, nu, nxt: (be[j], 0, 0)
    any_spec = pl.BlockSpec(memory_space=pl.ANY)
    return pl.pallas_call(
        _ffn_kernel,
        grid_spec=pltpu.PrefetchScalarGridSpec(
            num_scalar_prefetch=3,
            grid=(n_blocks,),
            in_specs=[
                pl.BlockSpec((blk, LANES), xrow),
                any_spec,
                pl.BlockSpec((1, 1, de), bsel),
                any_spec,
                pl.BlockSpec((1, 1, de), bsel),
                any_spec,
                pl.BlockSpec((1, 1, d), bsel),
            ],
            out_specs=pl.BlockSpec((blk, LANES), lambda j, be, nu, nxt: (j, 0)),
            scratch_shapes=[pltpu.VMEM((d, de), F32), pltpu.VMEM((d, de), F32), pltpu.VMEM((de, d), F32),
                            pltpu.VMEM((d, de), BF16), pltpu.VMEM((d, de), BF16), pltpu.VMEM((de, d), BF16),
                            pltpu.SemaphoreType.DMA((3,))],
        ),
        out_shape=jax.ShapeDtypeStruct(xb.shape, F32),
        compiler_params=pltpu.CompilerParams(vmem_limit_bytes=VMEM_LIMIT),
        name="ffn",
    )(block_e, n_used, next_e, xb, w_gate, b_gate, w_up, b_up, w_down, b_down)


def _combine_kernel(tiles_per_sample, pos_ref, x1_ref, route_ref, mod_ref, yb_hbm, o_ref, buf, sem):
    i = pl.program_id(0)
    n = pl.num_programs(0)
    rows = TOP_K * CMB_TOK
    d = x1_ref.shape[1]
    sub = d // LANES

    def issue(tile, slot):
        def body(g, carry):
            for u in range(ISSUE_UNROLL):
                r = g * ISSUE_UNROLL + u
                _row_tile_copy(yb_hbm, pos_ref[tile * rows + r], buf.at[slot], r, sub,
                               sem.at[slot]).start(priority=u % DMA_QUEUES)
            return carry
        lax.fori_loop(0, rows // ISSUE_UNROLL, body, 0)

    @pl.when(i == 0)
    def _():
        issue(0, 0)

    @pl.when(i + 1 < n)
    def _():
        issue(i + 1, (i + 1) % 2)

    slot = i % 2
    pltpu.make_async_copy(yb_hbm.at[pl.ds(0, rows * sub)], buf.at[slot], sem.at[slot]).wait()

    b = i // tiles_per_sample
    g_m = mod_ref[b][5:6, :]
    route = route_ref[...]
    acc = jnp.zeros(x1_ref.shape, F32)
    for k in range(TOP_K):
        acc = acc + route[:, TOP_K + k:TOP_K + k + 1] * _from_row_tiles(buf.at[slot], k * CMB_TOK, CMB_TOK, d)
    o_ref[...] = x1_ref[...] + g_m * acc


def _combine(pos_tiles, x1, route, mod, yb, tiles_per_sample):
    t, d = x1.shape
    row = lambda i, pos: (i, 0)
    return pl.pallas_call(
        functools.partial(_combine_kernel, tiles_per_sample),
        grid_spec=pltpu.PrefetchScalarGridSpec(
            num_scalar_prefetch=1,
            grid=(t // CMB_TOK,),
            in_specs=[
                pl.BlockSpec((CMB_TOK, d), row),
                pl.BlockSpec((CMB_TOK, route.shape[1]), row),
                pl.BlockSpec(mod.shape, lambda i, pos: (0, 0, 0)),
                pl.BlockSpec(memory_space=pl.ANY),
            ],
            out_specs=pl.BlockSpec((CMB_TOK, d), row),
            scratch_shapes=[pltpu.VMEM((2, TOP_K * CMB_TOK * d // LANES, LANES), F32),
                            pltpu.SemaphoreType.DMA((2,))],
        ),
        out_shape=jax.ShapeDtypeStruct((t, d), F32),
        compiler_params=pltpu.CompilerParams(vmem_limit_bytes=VMEM_LIMIT),
        name="combine",
    )(pos_tiles, x1, route, mod, yb)


def _rope_tables(s):
    rows = s // GRID_W
    row = jnp.repeat(jnp.arange(rows, dtype=F32), GRID_W)
    col = jnp.tile(jnp.arange(GRID_W, dtype=F32), rows)
    inv = 1.0 / (ROPE_THETA ** (jnp.arange(AXIS_PAIRS, dtype=F32) / AXIS_PAIRS))
    ang = jnp.concatenate([row[:, None] * inv, col[:, None] * inv], axis=-1)
    cos, sin = jnp.cos(ang), jnp.sin(ang)
    cos_i = jnp.repeat(cos, 2, axis=-1)
    sin_i = jnp.stack([-sin, sin], axis=-1).reshape(s, HEAD_DIM)
    reps = LANES // HEAD_DIM
    return jnp.tile(cos_i, (1, reps)), jnp.tile(sin_i, (1, reps))


def kernel(x, c, ctx, c_ctx, w_ada, b_ada, norm_attn, w_in, q_norm_a, k_norm_a, q_norm_b, k_norm_b, lambda_q1,
           lambda_k1, lambda_q2, lambda_k2, subln_b, w_oa, w_ob, w_out, norm_mlp, router_w, router_b, w_gate,
           b_gate, w_up, b_up, w_down, b_down):
    bsz, s, d = x.shape
    assert ctx.shape[1] == TOK_TILE and s % MERGE_TOK == 0 and s % TQ_A == 0 and s % TQ_B == 0 and s % GRID_W == 0
    assert w_ada.shape[0] == 1, "single-layer block"
    t = bsz * s
    assert t % CMB_TOK == 0 and (t * TOP_K) % FFN_ROWS == 0

    mod_rows = 8
    assert bsz + 1 <= mod_rows
    cvec = jnp.concatenate([c, c_ctx[None, :], jnp.zeros((mod_rows - bsz - 1, d), F32)], axis=0)
    mod = _adaln(cvec, w_ada[0], b_ada[0][None, :]).reshape(mod_rows, N_MOD, d)

    cos, sin = _rope_tables(s)
    tile_gain = lambda g: jnp.tile(g, LANES // HEAD_DIM)[None, :]
    gidx = jnp.arange(MXU_DIM) // HEAD_DIM
    gmat = jnp.where(gidx[:, None] == gidx[None, :], 1.0 / HEAD_DIM, 0.0).astype(BF16)
    w_qkv = w_in[0][:, :QKV_COLS].astype(BF16)
    w_g = w_in[0][:, QKV_COLS:].astype(BF16)

    qat, ka, vat, qbt, kb, vbt = _project(
        x, ctx, mod, norm_attn, w_qkv, cos, sin, tile_gain(q_norm_a[0]), tile_gain(k_norm_a[0]),
        tile_gain(q_norm_b[0]), tile_gain(k_norm_b[0]), gmat)

    ya = _attn_a(qat, ka, vat)
    yb = _attn_b(qbt, kb, vbt, lambda_q1, lambda_k1, lambda_q2, lambda_k2, subln_b[0][:, None])

    tri = jnp.arange(TOK_TILE)
    utri = (tri[:, None] < tri[None, :]).astype(BF16)
    x1, h2, route_t, counts = _merge(
        x.reshape(t, d), ya.reshape(t, A_Q_COLS), yb.reshape(t, B_WIDTH), mod, norm_attn, norm_mlp, w_g,
        w_oa[0].astype(BF16), w_ob[0].astype(BF16), w_out[0].astype(BF16), router_w[0].T, router_b[0][:, None], utri,
        s // MERGE_TOK)

    counts = counts[:, 0].astype(I32)
    padded = (counts + FFN_ROWS - 1) // FFN_ROWS * FFN_ROWS
    pad_end = jnp.cumsum(padded)
    pad_start = pad_end - padded
    top_e = route_t[:TOP_K].astype(I32)
    rank = route_t[2 * TOP_K:3 * TOP_K].astype(I32)
    experts = jnp.arange(N_EXPERTS, dtype=I32)
    start_of = jnp.sum(jnp.where(top_e[..., None] == experts, pad_start, 0), axis=-1)
    pos_t = start_of + rank
    n_blocks = t * TOP_K // FFN_ROWS + N_EXPERTS
    block_start = jnp.arange(n_blocks, dtype=I32) * FFN_ROWS
    block_e = jnp.minimum(jnp.sum((pad_end[None, :] <= block_start[:, None]).astype(I32), axis=1), N_EXPERTS - 1)
    n_used = (pad_end[-1:] // FFN_ROWS).astype(I32)
    first_at_or_after = lax.cummin(jnp.where(padded > 0, jnp.arange(N_EXPERTS, dtype=I32), N_EXPERTS), reverse=True)
    next_nonempty = jnp.concatenate([first_at_or_after[1:], jnp.full((1,), N_EXPERTS, I32)])
    next_of = jnp.where(next_nonempty < N_EXPERTS, next_nonempty, -1)
    next_e = jnp.sum(jnp.where(block_e[:, None] == experts, next_of, 0), axis=-1)

    xb = _dispatch(pos_t.reshape(-1), pad_start + counts, padded - counts, n_used, h2, n_blocks)
    de = w_gate.shape[-1]
    yb_rows = _ffn(block_e, n_used, next_e, xb, w_gate[0], b_gate[0].reshape(N_EXPERTS, 1, de), w_up[0],
                   b_up[0].reshape(N_EXPERTS, 1, de), w_down[0], b_down[0].reshape(N_EXPERTS, 1, d))
    pos_tiles = pos_t.reshape(TOP_K, t // CMB_TOK, CMB_TOK).transpose(1, 0, 2).reshape(-1)
    out = _combine(pos_tiles, x1, route_t.T, mod, yb_rows, s // CMB_TOK)
    return out.reshape(bsz, s, d)
```
